```python
import math
import jax, jax.numpy as jnp
from jax import lax
import numpy as np

D_MODEL = 2048
BATCH = 4
SEQ = 4096
DEPTH = 1
DEC_BATCH = 32
DEC_SEQ = 8
PAST_LEN = 16384
PAGE_SIZE = 128

N_HEADS = 16
HEAD_DIM = 64
N_KV = 4
HPG = N_HEADS // N_KV
ATTN_DIM = N_HEADS * HEAD_DIM
KV_DIM = N_KV * HEAD_DIM
CMP_LEN = 32
CMP_STRIDE = 16
CMP_R = CMP_LEN // CMP_STRIDE
PHI_HIDDEN = HEAD_DIM
SEL_BLOCK = 64
N_SELECT = 16
WINDOW = 512
Q_BLOCK = 64
CONV_DIM = D_MODEL // 2
CONV_K = 31
D_FF = 4 * D_MODEL
REL_BUCKETS = 32
REL_MAX_DIST = 128
EPS = 1e-6
NEG = -1e30
IN_DIM = ATTN_DIM + 3 * 2 * KV_DIM + 3 * N_HEADS + 2 * CONV_DIM + 2 * D_MODEL

kernel_name = 'nsa_conformer_hybrid_step'


def rmsnorm(x, g):
    xf = x.astype(jnp.float32)
    y = xf * lax.rsqrt(jnp.mean(xf * xf, axis=-1, keepdims=True) + EPS)
    return (y * g.astype(jnp.float32)).astype(x.dtype)


def layernorm(x, g, b):
    xf = x.astype(jnp.float32)
    mu = jnp.mean(xf, axis=-1, keepdims=True)
    var = jnp.mean(jnp.square(xf - mu), axis=-1, keepdims=True)
    y = (xf - mu) * lax.rsqrt(var + EPS) * g.astype(jnp.float32) + b.astype(jnp.float32)
    return y.astype(x.dtype)


def rel_bucket(dist):
    n = jnp.maximum(dist, 0)
    exact = REL_BUCKETS // 2
    logb = exact + (jnp.log(jnp.maximum(n, 1).astype(jnp.float32) / exact)
                    / math.log(REL_MAX_DIST / exact) * (REL_BUCKETS - exact)).astype(jnp.int32)
    return jnp.where(n < exact, n, jnp.minimum(logb, REL_BUCKETS - 1))


def masked_softmax(s, mask, axis):
    s = jnp.where(mask, s, NEG)
    p = jax.nn.softmax(s, axis=axis)
    return jnp.where(mask, p, 0.0)


def compress_chunks(rows, phi_pe, phi_w1):
    B, L = rows.shape[0], rows.shape[1]
    ch = rows.reshape(B, L // CMP_STRIDE, CMP_STRIDE, 2, N_KV, HEAD_DIM)
    w = phi_w1.reshape(2, CMP_R, CMP_STRIDE, HEAD_DIM, PHI_HIDDEN)
    pe = phi_pe.reshape(2, CMP_R, CMP_STRIDE, HEAD_DIM)
    pe_term = jnp.einsum('crsd,crsde->rce', pe, w)
    return jnp.einsum('bnscgd,crsde->rbncge', ch, w) + pe_term[:, None, None, :, None, :]


def compress_finish(parts, phi_w2):
    nch = parts.shape[2]
    nc = nch - CMP_R + 1
    hid = parts[0, :, 0:nc]
    for r in range(1, CMP_R):
        hid = hid + parts[r, :, r:r + nc]
    return jnp.einsum('bncgh,che->bncge', jax.nn.gelu(hid), phi_w2)


def nsa_attend(q, gate_logits, q_pos0, kv_cmp, kv_blk, win_rows, win_pos0, rel_bias):
    B, T = q.shape[0], q.shape[1]
    f32 = jnp.float32
    n_cmp = kv_cmp.shape[1]
    n_blk = kv_blk.shape[1]
    n_sel = min(N_SELECT, n_blk)
    qb = math.gcd(T, Q_BLOCK)
    qs = q * (HEAD_DIM ** -0.5)
    rel_g = rel_bias.astype(f32).reshape(REL_BUCKETS, N_KV, HPG)
    cmp_start = jnp.arange(n_cmp) * CMP_STRIDE
    cmp_end = cmp_start + CMP_LEN - 1
    blk_start = jnp.arange(n_blk) * SEL_BLOCK
    overlap = jnp.clip(jnp.minimum(cmp_start[:, None] + CMP_LEN, blk_start[None, :] + SEL_BLOCK)
                       - jnp.maximum(cmp_start[:, None], blk_start[None, :]), 0, CMP_LEN).astype(f32) / CMP_LEN
    win_p = jnp.pad(win_rows, ((0, 0), (WINDOW, 0), (0, 0), (0, 0), (0, 0)))
    b_ix = jnp.arange(B)[:, None, None, None]
    g_ix = jnp.arange(N_KV)[None, :, None, None]

    def one_block(i):
        s = i * qb
        qi = lax.dynamic_slice_in_dim(qs, s, qb, axis=1).reshape(B, qb, N_KV, HPG, HEAD_DIM)
        gi = jax.nn.sigmoid(lax.dynamic_slice_in_dim(gate_logits, s, qb, axis=1).astype(f32))
        gi = gi.reshape(B, qb, N_KV, HPG, 3)
        t = q_pos0 + s + jnp.arange(qb)
        sc = jnp.einsum('bqgrd,bngd->bgrqn', qi, kv_cmp[:, :, 0], preferred_element_type=f32)
        sc = sc + jnp.moveaxis(rel_g[rel_bucket(t[:, None] - cmp_end[None, :])], (2, 3), (0, 1))
        pc = masked_softmax(sc, cmp_end[None, :] <= t[:, None], -1)
        oc = jnp.einsum('bgrqn,bngd->bqgrd', pc.astype(q.dtype), kv_cmp[:, :, 1], preferred_element_type=f32)
        imp = jnp.einsum('bgrqn,nj->bgqj', pc, overlap)
        cur = t // SEL_BLOCK
        j = jnp.arange(n_blk)[None, :]
        forced = (j == 0) | (j == cur[:, None]) | (j == cur[:, None] - 1)
        valid = j * SEL_BLOCK <= t[:, None]
        score = jnp.where(forced, jnp.inf, jnp.where(valid, imp, -jnp.inf))
        _, idx = lax.top_k(score, n_sel)
        blk = kv_blk[b_ix, idx, :, :, g_ix, :]
        kpos = idx[..., None] * SEL_BLOCK + jnp.arange(SEL_BLOCK)
        dist = t[None, None, :, None, None] - kpos
        ss = jnp.einsum('bqgrd,bgqnsd->bgrqns', qi, blk[..., 0, :], preferred_element_type=f32)
        ss = ss + jnp.moveaxis(rel_g[rel_bucket(dist), g_ix[..., None]], -1, 2)
        psel = masked_softmax(ss, (dist >= 0)[:, :, None], (-2, -1))
        osel = jnp.einsum('bgrqns,bgqnsd->bqgrd', psel.astype(q.dtype), blk[..., 1, :], preferred_element_type=f32)
        kw = lax.dynamic_slice_in_dim(win_p, q_pos0 - win_pos0 + s, WINDOW + qb, axis=1)
        wpos = q_pos0 + s - WINDOW + jnp.arange(WINDOW + qb)
        dw = t[:, None] - wpos[None, :]
        mw = (dw >= 0) & (dw < WINDOW) & (wpos[None, :] >= win_pos0)
        sw = jnp.einsum('bqgrd,bkgd->bgrqk', qi, kw[:, :, 0], preferred_element_type=f32)
        sw = sw + jnp.moveaxis(rel_g[rel_bucket(dw)], (2, 3), (0, 1))
        pw = masked_softmax(sw, mw, -1)
        ow = jnp.einsum('bgrqk,bkgd->bqgrd', pw.astype(q.dtype), kw[:, :, 1], preferred_element_type=f32)
        o = gi[..., 0:1] * oc + gi[..., 1:2] * osel + gi[..., 2:3] * ow
        return o.astype(q.dtype).reshape(B, qb, N_HEADS, HEAD_DIM)

    out = lax.map(one_block, jnp.arange(T // qb))
    return jnp.moveaxis(out, 0, 1).reshape(B, T, ATTN_DIM)


def decoder_layer(x, q_pos0, past_cmp, past_slc, past_win, conv_hist,
                  w_in, phi_pe, phi_w1, phi_w2, rel_bias, w_attn_out, conv_w, conv_b,
                  conv_ln_g, conv_ln_b, w_conv_out, w_o, w_up, w_down,
                  g_pre_mix, g_post_mix, g_pre_ffn, g_post_ffn):
    B, T = x.shape[0], x.shape[1]
    sizes = (ATTN_DIM, 2 * KV_DIM, 2 * KV_DIM, 2 * KV_DIM, 3 * N_HEADS, 2 * CONV_DIM, 2 * D_MODEL)
    offs = [int(v) for v in np.cumsum(sizes)[:-1]]
    h = rmsnorm(x, g_pre_mix)
    z = h @ w_in
    q, kvc, kvs, kvw, ng, glu, mg = jnp.split(z, offs, axis=-1)
    q = q.reshape(B, T, N_HEADS, HEAD_DIM)
    kvc = kvc.reshape(B, T, 2, N_KV, HEAD_DIM)
    kvs = kvs.reshape(B, T, 2, N_KV, HEAD_DIM)
    kvw = kvw.reshape(B, T, 2, N_KV, HEAD_DIM)
    ng = ng.reshape(B, T, N_HEADS, 3)
    n_full = (T // CMP_STRIDE) * CMP_STRIDE
    parts = jnp.concatenate([compress_chunks(past_cmp, phi_pe, phi_w1),
                             compress_chunks(kvc[:, :n_full], phi_pe, phi_w1)], axis=2)
    kv_cmp = compress_finish(parts, phi_w2)
    L = past_slc.shape[1] + T
    ns = -(-L // SEL_BLOCK)
    rows = jnp.concatenate([past_slc, kvs, jnp.zeros((B, ns * SEL_BLOCK - L, 2, N_KV, HEAD_DIM), kvs.dtype)], axis=1)
    kv_blk = rows.reshape(B, ns, SEL_BLOCK, 2, N_KV, HEAD_DIM)
    win_rows = jnp.concatenate([past_win, kvw], axis=1)
    win_pos0 = q_pos0 - past_win.shape[1]
    attn_branch = nsa_attend(q, ng, q_pos0, kv_cmp, kv_blk, win_rows, win_pos0, rel_bias) @ w_attn_out
    a, b = jnp.split(glu, 2, axis=-1)
    u = a * jax.nn.sigmoid(b)
    up = jnp.concatenate([conv_hist, u], axis=1)
    c = lax.conv_general_dilated(up, conv_w[:, None, :], window_strides=(1,), padding='VALID',
                                 dimension_numbers=('NWC', 'WIO', 'NWC'),
                                 feature_group_count=CONV_DIM) + conv_b
    c = jax.nn.silu(layernorm(c, conv_ln_g, conv_ln_b))
    conv_branch = c @ w_conv_out
    ga, gc = jnp.split(mg, 2, axis=-1)
    mixed = jax.nn.sigmoid(ga) * attn_branch + jax.nn.sigmoid(gc) * conv_branch
    x = x + rmsnorm(mixed @ w_o, g_post_mix)
    f = jnp.square(jax.nn.relu(rmsnorm(x, g_pre_ffn) @ w_up)) @ w_down
    y = x + rmsnorm(f, g_post_ffn)
    n_win = min(WINDOW, win_rows.shape[1])
    new_win = win_rows[:, win_rows.shape[1] - n_win:]
    new_conv = up[:, up.shape[1] - (CONV_K - 1):]
    return y, kvc, kvs, new_win, new_conv


def setup_inputs(seed: int = 0) -> dict:
    key = jax.random.key(seed)
    ks = jax.random.split(key, 25)
    n_pages = PAST_LEN // PAGE_SIZE
    n_used = DEC_BATCH * n_pages
    n_phys = n_used + n_used // 4
    win_buf = min(WINDOW, PAST_LEN)
    f32 = jnp.float32

    def nrm(k, shape, scale):
        return jax.random.normal(k, shape, f32) * scale

    return {
        'x_prompt': nrm(ks[0], (BATCH, SEQ, D_MODEL), 1.0),
        'x_sample': nrm(ks[1], (DEC_BATCH, DEC_SEQ, D_MODEL), 1.0),
        'cache_kv_cmp': nrm(ks[2], (DEPTH, n_phys, PAGE_SIZE, 2, N_KV, HEAD_DIM), 1.0),
        'cache_kv_slc': nrm(ks[3], (DEPTH, n_phys, PAGE_SIZE, 2, N_KV, HEAD_DIM), 1.0),
        'state_kv_win': nrm(ks[4], (DEPTH, DEC_BATCH, win_buf, 2, N_KV, HEAD_DIM), 1.0),
        'state_conv': nrm(ks[5], (DEPTH, DEC_BATCH, CONV_K - 1, CONV_DIM), 1.0),
        'page_table': jax.random.permutation(ks[6], n_phys)[:n_used].reshape(DEC_BATCH, n_pages).astype(jnp.int32),
        'w_in': nrm(ks[7], (DEPTH, D_MODEL, IN_DIM), D_MODEL ** -0.5),
        'phi_pe': nrm(ks[8], (DEPTH, 2, CMP_LEN, HEAD_DIM), 0.5),
        'phi_w1': nrm(ks[9], (DEPTH, 2, CMP_LEN * HEAD_DIM, PHI_HIDDEN), (CMP_LEN * HEAD_DIM) ** -0.5),
        'phi_w2': nrm(ks[10], (DEPTH, 2, PHI_HIDDEN, HEAD_DIM), 2.0 * PHI_HIDDEN ** -0.5),
        'rel_bias': nrm(ks[11], (REL_BUCKETS, N_HEADS), 0.5),
        'w_attn_out': nrm(ks[12], (DEPTH, ATTN_DIM, D_MODEL), ATTN_DIM ** -0.5),
        'conv_w': nrm(ks[13], (DEPTH, CONV_K, CONV_DIM), CONV_K ** -0.5),
        'conv_b': nrm(ks[14], (DEPTH, CONV_DIM), 0.01),
        'conv_ln_g': 1.0 + nrm(ks[15], (DEPTH, CONV_DIM), 0.1),
        'conv_ln_b': nrm(ks[16], (DEPTH, CONV_DIM), 0.01),
        'w_conv_out': nrm(ks[17], (DEPTH, CONV_DIM, D_MODEL), CONV_DIM ** -0.5),
        'w_o': nrm(ks[18], (DEPTH, D_MODEL, D_MODEL), D_MODEL ** -0.5),
        'w_up': nrm(ks[19], (DEPTH, D_MODEL, D_FF), D_MODEL ** -0.5),
        'w_down': nrm(ks[20], (DEPTH, D_FF, D_MODEL), D_FF ** -0.5),
        'g_pre_mix': 1.0 + nrm(ks[21], (DEPTH, D_MODEL), 0.1),
        'g_post_mix': 1.0 + nrm(ks[22], (DEPTH, D_MODEL), 0.1),
        'g_pre_ffn': 1.0 + nrm(ks[23], (DEPTH, D_MODEL), 0.1),
        'g_post_ffn': 1.0 + nrm(ks[24], (DEPTH, D_MODEL), 0.1),
    }


def reference(x_prompt, x_sample, cache_kv_cmp, cache_kv_slc, state_kv_win, state_conv, page_table,
              w_in, phi_pe, phi_w1, phi_w2, rel_bias, w_attn_out, conv_w, conv_b, conv_ln_g, conv_ln_b,
              w_conv_out, w_o, w_up, w_down, g_pre_mix, g_post_mix, g_pre_ffn, g_post_ffn):
    n_pages = PAST_LEN // PAGE_SIZE
    yp, ys = x_prompt, x_sample
    cmp_p, cmp_s, slc_p, slc_s, win_p, win_s, conv_p, conv_s = [], [], [], [], [], [], [], []
    for l in range(DEPTH):
        def run(x, q_pos0, past_cmp, past_slc, past_win, conv_hist):
            return decoder_layer(x, q_pos0, past_cmp, past_slc, past_win, conv_hist,
                                 w_in[l], phi_pe[l], phi_w1[l], phi_w2[l], rel_bias, w_attn_out[l],
                                 conv_w[l], conv_b[l], conv_ln_g[l], conv_ln_b[l], w_conv_out[l],
                                 w_o[l], w_up[l], w_down[l],
                                 g_pre_mix[l], g_post_mix[l], g_pre_ffn[l], g_post_ffn[l])
        empty = jnp.zeros((BATCH, 0, 2, N_KV, HEAD_DIM), yp.dtype)
        yp, kc, kslc, kw, cv = run(yp, 0, empty, empty, empty,
                                   jnp.zeros((BATCH, CONV_K - 1, CONV_DIM), yp.dtype))
        cmp_p.append(kc)
        slc_p.append(kslc)
        win_p.append(kw)
        conv_p.append(cv)
        past_cmp = cache_kv_cmp[l][page_table].reshape(DEC_BATCH, n_pages * PAGE_SIZE, 2, N_KV, HEAD_DIM)
        past_slc = cache_kv_slc[l][page_table].reshape(DEC_BATCH, n_pages * PAGE_SIZE, 2, N_KV, HEAD_DIM)
        ys, kc, kslc, kw, cv = run(ys, PAST_LEN, past_cmp, past_slc, state_kv_win[l], state_conv[l])
        cmp_s.append(kc)
        slc_s.append(kslc)
        win_s.append(kw)
        conv_s.append(cv)
    return (yp, ys, jnp.stack(cmp_p), jnp.stack(cmp_s), jnp.stack(slc_p), jnp.stack(slc_s),
            jnp.stack(win_p), jnp.stack(win_s), jnp.stack(conv_p), jnp.stack(conv_s))
```

```python
import functools
import math

import jax
import jax.numpy as jnp
import numpy as np
from jax import lax
from jax.experimental import pallas as pl
from jax.experimental.pallas import tpu as pltpu

D_MODEL = 2048
N_HEADS = 16
HEAD_DIM = 64
N_KV = 4
HPG = N_HEADS // N_KV
ATTN_DIM = N_HEADS * HEAD_DIM
KV_DIM = N_KV * HEAD_DIM
CMP_LEN = 32
CMP_STRIDE = 16
CMP_R = CMP_LEN // CMP_STRIDE
PHI_HIDDEN = HEAD_DIM
SEL_BLOCK = 64
N_SELECT = 16
WINDOW = 512
Q_BLOCK = 64
CONV_DIM = D_MODEL // 2
CONV_K = 31
D_FF = 4 * D_MODEL
REL_BUCKETS = 32
REL_MAX_DIST = 128
EPS = 1e-6
NEG = -1e30
PAGE_SIZE = 128

LANES = 128
VMEM_LIMIT_BYTES = 56 * 1024 * 1024

NEAR_BLOCKS = 4
NEAR_KEYS = NEAR_BLOCKS * SEL_BLOCK
FAR_TILE = 512
WIN_FAR_KEYS = 384
ROW_CHUNK = 32
HALO = 32

_F32 = jnp.float32
_BF16 = jnp.bfloat16


def _params(*sem):
    return pltpu.CompilerParams(dimension_semantics=sem, vmem_limit_bytes=VMEM_LIMIT_BYTES)


def _dot(a, b):
    return jnp.dot(a, b, preferred_element_type=_F32)


def _dot_nt(a, b):
    return lax.dot_general(a, b, (((1,), (1,)), ((), ())), preferred_element_type=_F32)


def _split3(x):
    hi = x.astype(_BF16)
    r1 = x - hi.astype(_F32)
    mid = r1.astype(_BF16)
    lo = (r1 - mid.astype(_F32)).astype(_BF16)
    return hi, mid, lo


def _rms(x, g):
    return x * lax.rsqrt(jnp.mean(x * x, axis=-1, keepdims=True) + EPS) * g


def _rel_bucket_np(dist):
    n = np.maximum(dist, 0)
    exact = REL_BUCKETS // 2
    logb = exact + (np.log(np.maximum(n, 1).astype(np.float32) / np.float32(exact))
                    / np.float32(math.log(REL_MAX_DIST / exact)) * (REL_BUCKETS - exact)).astype(np.int32)
    return np.where(n < exact, n, np.minimum(logb, REL_BUCKETS - 1)).astype(np.int32)


def _proj_kernel(x_ref, g_ref, *refs, n_w, epilogue):
    w_refs = refs[:n_w]
    o_refs = refs[n_w:-1]
    h_ref = refs[-1]

    @pl.when(pl.program_id(1) == 0)
    def _():
        h_ref[...] = _rms(x_ref[...], g_ref[...]).astype(_BF16)

    h = h_ref[...]
    accs = [_dot(h, w[...]) for w in w_refs]
    if epilogue == "q":
        o_refs[0][...] = (accs[0] * (HEAD_DIM ** -0.5)).astype(_BF16)
    elif epilogue == "kv":
        o_refs[0][...] = accs[0]
        o_refs[1][...] = accs[0].astype(_BF16)
    elif epilogue == "plain":
        o_refs[0][...] = accs[0]
    elif epilogue == "glu":
        o_refs[0][...] = accs[0] * jax.nn.sigmoid(accs[1])
    elif epilogue == "sigmoid":
        o_refs[0][...] = jax.nn.sigmoid(accs[0]).astype(_BF16)
    else:
        raise ValueError(epilogue)


def _proj(x, g, ws, epilogue, out_dtypes, tm, tn):
    m, d = x.shape
    n = ws[0].shape[1]
    tn = min(tn, n)
    grid = (m // tm, n // tn)
    kern = functools.partial(_proj_kernel, n_w=len(ws), epilogue=epilogue)
    outs = pl.pallas_call(
        kern,
        grid=grid,
        in_specs=[pl.BlockSpec((tm, d), lambda i, j: (i, 0)),
                  pl.BlockSpec((1, d), lambda i, j: (0, 0))]
                 + [pl.BlockSpec((d, tn), lambda i, j: (0, j)) for _ in ws],
        out_specs=[pl.BlockSpec((tm, tn), lambda i, j: (i, j)) for _ in out_dtypes],
        out_shape=[jax.ShapeDtypeStruct((m, n), dt) for dt in out_dtypes],
        scratch_shapes=[pltpu.VMEM((tm, d), _BF16)],
        compiler_params=_params("parallel", "arbitrary"),
        name="proj_" + epilogue,
    )(x, g, *ws)
    return outs


def _mix_kernel(a_ref, c_ref, ga_ref, gc_ref, wa_ref, wc_ref, o_ref):
    ya = _dot(a_ref[...], wa_ref[...])
    yc = _dot(c_ref[...], wc_ref[...])
    o_ref[...] = (ga_ref[...].astype(_F32) * ya + gc_ref[...].astype(_F32) * yc).astype(_BF16)


def _mix(attn, conv, mg, wao, wco, tm, tn):
    m, ka = attn.shape
    n = wao.shape[1]
    nb = n // tn
    return pl.pallas_call(
        _mix_kernel,
        grid=(m // tm, nb),
        in_specs=[pl.BlockSpec((tm, ka), lambda i, j: (i, 0)),
                  pl.BlockSpec((tm, conv.shape[1]), lambda i, j: (i, 0)),
                  pl.BlockSpec((tm, tn), lambda i, j: (i, j)),
                  pl.BlockSpec((tm, tn), lambda i, j: (i, j + nb)),
                  pl.BlockSpec((ka, tn), lambda i, j: (0, j)),
                  pl.BlockSpec((conv.shape[1], tn), lambda i, j: (0, j))],
        out_specs=pl.BlockSpec((tm, tn), lambda i, j: (i, j)),
        out_shape=jax.ShapeDtypeStruct((m, n), _BF16),
        compiler_params=_params("parallel", "arbitrary"),
        name="mix",
    )(attn, conv, mg, mg, wao, wco)


def _oproj_kernel(mx_ref, x_ref, w_ref, g_ref, o_ref):
    y = _dot(mx_ref[...], w_ref[...])
    o_ref[...] = x_ref[...] + _rms(y, g_ref[...])


def _oproj(mixed, x, wo, g, tm):
    m, d = x.shape
    return pl.pallas_call(
        _oproj_kernel,
        grid=(m // tm,),
        in_specs=[pl.BlockSpec((tm, d), lambda i: (i, 0)),
                  pl.BlockSpec((tm, d), lambda i: (i, 0)),
                  pl.BlockSpec((d, d), lambda i: (0, 0)),
                  pl.BlockSpec((1, d), lambda i: (0, 0))],
        out_specs=pl.BlockSpec((tm, d), lambda i: (i, 0)),
        out_shape=jax.ShapeDtypeStruct((m, d), _F32),
        compiler_params=_params("parallel"),
        name="oproj",
    )(mixed, x, wo, g)


def _ffn_kernel(x_ref, gpre_ref, gpost_ref, wu_ref, wd_ref, o_ref, h_ref, acc_ref):
    j = pl.program_id(1)

    @pl.when(j == 0)
    def _():
        h_ref[...] = _rms(x_ref[...], gpre_ref[...]).astype(_BF16)
        acc_ref[...] = jnp.zeros_like(acc_ref)

    a = jnp.maximum(_dot(h_ref[...], wu_ref[...]), 0.0)
    acc_ref[...] += _dot((a * a).astype(_BF16), wd_ref[...])

    @pl.when(j == pl.num_programs(1) - 1)
    def _():
        o_ref[...] = x_ref[...] + _rms(acc_ref[...], gpost_ref[...])


def _ffn(x, gpre, gpost, wu, wd, tm, tf):
    m, d = x.shape
    f = wu.shape[1]
    return pl.pallas_call(
        _ffn_kernel,
        grid=(m // tm, f // tf),
        in_specs=[pl.BlockSpec((tm, d), lambda i, j: (i, 0)),
                  pl.BlockSpec((1, d), lambda i, j: (0, 0)),
                  pl.BlockSpec((1, d), lambda i, j: (0, 0)),
                  pl.BlockSpec((d, tf), lambda i, j: (0, j)),
                  pl.BlockSpec((tf, d), lambda i, j: (j, 0))],
        out_specs=pl.BlockSpec((tm, d), lambda i, j: (i, 0)),
        out_shape=jax.ShapeDtypeStruct((m, d), _F32),
        scratch_shapes=[pltpu.VMEM((tm, d), _BF16), pltpu.VMEM((tm, d), _F32)],
        compiler_params=_params("parallel", "arbitrary"),
        name="ffn",
    )(x, gpre, gpost, wu, wd)


def _conv_kernel(u_ref, halo_ref, w_ref, b_ref, lg_ref, lb_ref, o_ref, win_ref, *, tt, zero_first):
    c = u_ref.shape[-1]
    halo = halo_ref[0]
    if zero_first:
        halo = jnp.where(pl.program_id(1) == 0, 0.0, halo)
    win_ref[0:HALO, :] = halo
    win_ref[HALO:HALO + tt, :] = u_ref[0]
    rc = min(ROW_CHUNK, tt)
    off = HALO - (CONV_K - 1)
    for ch in range(tt // rc):
        acc = jnp.zeros((rc, c), _F32) + b_ref[...]
        for k in range(CONV_K):
            acc = acc + w_ref[k:k + 1, :] * win_ref[ch * rc + off + k:ch * rc + off + k + rc, :]
        mu = jnp.mean(acc, axis=-1, keepdims=True)
        xc = acc - mu
        var = jnp.mean(xc * xc, axis=-1, keepdims=True)
        y = xc * lax.rsqrt(var + EPS) * lg_ref[...] + lb_ref[...]
        o_ref[0, ch * rc:(ch + 1) * rc, :] = (y * jax.nn.sigmoid(y)).astype(_BF16)


def _conv(u, halo_src, w, b, lg, lb, tt, zero_first):
    bsz, t, c = u.shape
    nhb = tt // HALO
    if zero_first:
        halo_map = lambda bi, ti: (bi, jnp.maximum(ti * nhb - 1, 0), 0)
    else:
        halo_map = lambda bi, ti: (bi, 0, 0)
    kern = functools.partial(_conv_kernel, tt=tt, zero_first=zero_first)
    return pl.pallas_call(
        kern,
        grid=(bsz, t // tt),
        in_specs=[pl.BlockSpec((1, tt, c), lambda bi, ti: (bi, ti, 0)),
                  pl.BlockSpec((1, HALO, c), halo_map),
                  pl.BlockSpec((HALO, c), lambda bi, ti: (0, 0)),
                  pl.BlockSpec((1, c), lambda bi, ti: (0, 0)),
                  pl.BlockSpec((1, c), lambda bi, ti: (0, 0)),
                  pl.BlockSpec((1, c), lambda bi, ti: (0, 0))],
        out_specs=pl.BlockSpec((1, tt, c), lambda bi, ti: (bi, ti, 0)),
        out_shape=jax.ShapeDtypeStruct((bsz, t, c), _BF16),
        scratch_shapes=[pltpu.VMEM((HALO + tt, c), _F32)],
        compiler_params=_params("parallel", "arbitrary"),
        name="conv",
    )(u, halo_src, w, b, lg, lb)


def _compress_kernel(*refs, n_src, rows_per_src, n_prefetch):
    refs = refs[n_prefetch:]
    src_refs = refs[:n_src]
    w1_ref, w2_ref, pe_ref, w1t_ref, kc_ref, vc_ref, carry_ref = refs[n_src:]
    k = pl.program_id(1)

    @pl.when(k == 0)
    def _():
        carry_ref[...] = jnp.zeros_like(carry_ref)

    nrow = n_src * rows_per_src
    outs = []
    for c in range(2):
        acc = jnp.zeros((nrow, 2 * KV_DIM), _F32)
        for s in range(CMP_STRIDE):
            lo = s * 2 * KV_DIM + c * KV_DIM
            if n_src == 1:
                xs = src_refs[0][0, :, lo:lo + KV_DIM]
            else:
                xs = jnp.concatenate([r[0, :, lo:lo + KV_DIM] for r in src_refs], axis=0)
            acc = acc + _dot(xs.astype(_BF16), w1_ref[c, s])
        pt = jnp.sum(pe_ref[c] * w1t_ref[c], axis=0, keepdims=True)
        pt = jnp.concatenate([pt] * (KV_DIM // LANES), axis=-1)
        a0 = acc[:, :KV_DIM]
        a1 = acc[:, KV_DIM:]
        first = lax.broadcasted_iota(jnp.int32, (nrow, KV_DIM), 0) == 0
        a0s = jnp.where(first, carry_ref[c], pltpu.roll(a0, 1, 0))
        carry_ref[c] = a0[nrow - 1:nrow, :]
        hid = jax.nn.gelu(a0s + a1 + pt)
        outs.append(_dot(hid.astype(_BF16), w2_ref[c]))
    kc_ref[0] = outs[0].astype(_BF16)
    vc_ref[0] = outs[1].astype(_BF16)


def _compress(src, src_specs, grid, n_src, rows_per_src, n_batch, n_out_rows, w1bd, w2bd, pe_b, w1t,
              num_prefetch=0, prefetch_args=()):
    kern = functools.partial(_compress_kernel, n_src=n_src, rows_per_src=rows_per_src, n_prefetch=num_prefetch)
    nrow = n_src * rows_per_src
    if num_prefetch:
        cmap = lambda b, k, pt: (0, 0, 0, 0)
        cmap3 = lambda b, k, pt: (0, 0, 0)
        omap = lambda b, k, pt: (b, k, 0)
    else:
        cmap = lambda b, k: (0, 0, 0, 0)
        cmap3 = lambda b, k: (0, 0, 0)
        omap = lambda b, k: (b, k, 0)
    in_specs = list(src_specs) + [
        pl.BlockSpec(w1bd.shape, cmap),
        pl.BlockSpec(w2bd.shape, cmap3),
        pl.BlockSpec(pe_b.shape, cmap3),
        pl.BlockSpec(w1t.shape, cmap3),
    ]
    grid_spec = pltpu.PrefetchScalarGridSpec(
        num_scalar_prefetch=num_prefetch,
        grid=grid,
        in_specs=in_specs,
        out_specs=[pl.BlockSpec((1, nrow, KV_DIM), omap), pl.BlockSpec((1, nrow, KV_DIM), omap)],
        scratch_shapes=[pltpu.VMEM((2, 1, KV_DIM), _F32)],
    )
    return pl.pallas_call(
        kern,
        grid_spec=grid_spec,
        out_shape=[jax.ShapeDtypeStruct((n_batch, n_out_rows, KV_DIM), _BF16)] * 2,
        compiler_params=_params("parallel", "arbitrary"),
        name="compress",
    )(*prefetch_args, *src, w1bd, w2bd, pe_b, w1t)


def _rank_select(sc_ref, n_rows, n_iter, n_sel):
    s = sc_ref[...]
    jio = lax.broadcasted_iota(jnp.int32, s.shape, 0)

    def body(jp, cnt):
        row = jnp.broadcast_to(sc_ref[pl.ds(jp, 1), :], s.shape)
        ge = jnp.where(row >= s, 1.0, 0.0)
        gt = jnp.where(row > s, 1.0, 0.0)
        return cnt + jnp.where(jio > jp, ge, gt)

    cnt = lax.fori_loop(0, n_iter, body, jnp.zeros(s.shape, _F32))
    return jnp.where(cnt < n_sel, 1.0, 0.0)


def _softmax_parts(parts):
    ms = [jnp.max(jnp.where(mk, s, NEG), axis=-1, keepdims=True) for s, mk in parts]
    m = functools.reduce(jnp.maximum, ms)
    ps = [jnp.where(mk, jnp.exp(s - m), 0.0) for s, mk in parts]
    l = functools.reduce(lambda a, b: a + b, [jnp.sum(p, axis=-1, keepdims=True) for p in ps])
    inv = 1.0 / jnp.maximum(l, 1e-30)
    return [p * inv for p in ps]


def _attn_prompt_kernel(q_ref, ng_ref, ks_ref, vs_ref, kw_ref, vw_ref, kc_ref, vc_ref,
                        tz_ref, tc_ref, c31_ref, ot_ref, et_ref, ex_ref,
                        o_ref, qz_ref, oacc_ref, acc_ref, m_ref, l_ref, sc_ref, gate_ref, *, n_cmp_pad):
    i = pl.program_id(1)
    qb = Q_BLOCK
    rows = HPG * qb
    lane_g = lax.broadcasted_iota(jnp.int32, (qb, KV_DIM), 1) // HEAD_DIM

    for g in range(N_KV):
        for r in range(HPG):
            slab = q_ref[0, :, r * KV_DIM:(r + 1) * KV_DIM].astype(_F32)
            qz_ref[g, r * qb:(r + 1) * qb, :] = jnp.where(lane_g == g, slab, 0.0).astype(_BF16)

    gs = jax.nn.sigmoid(ng_ref[0])
    gate_ref[...] = sum(_dot(p, ex_ref[...]) for p in _split3(gs))
    oacc_ref[...] = jnp.zeros_like(oacc_ref)

    def emit(branch, g, out_g):
        for r in range(HPG):
            col = r * KV_DIM
            gt = gate_ref[:, branch * ATTN_DIM + col:branch * ATTN_DIM + col + KV_DIM]
            oacc_ref[:, col:col + KV_DIM] += jnp.where(lane_g == g, gt * out_g[r * qb:(r + 1) * qb, :], 0.0)

    rq = lax.broadcasted_iota(jnp.int32, (rows, 1), 0) % qb

    jc = lax.broadcasted_iota(jnp.int32, (rows, n_cmp_pad), 1)
    cmp_valid = (jc >= 1) & (CMP_STRIDE * jc + (CMP_LEN - CMP_STRIDE - 1) - rq <= i * qb)
    mm = lax.broadcasted_iota(jnp.int32, (LANES, n_cmp_pad), 0)
    jj = lax.broadcasted_iota(jnp.int32, (LANES, n_cmp_pad), 1)
    shift = jnp.where(((mm < 16) & (jj - mm == 4 * i - 8)) | (mm == 16), 1.0, 0.0).astype(_BF16)
    imp_parts = []
    for g in range(N_KV):
        s = _dot_nt(qz_ref[g], kc_ref[0])
        s = s + sum(_dot(tc_ref[p, g], shift) for p in range(3))
        (pn,) = _softmax_parts([(s, cmp_valid)])
        emit(0, g, _dot(pn.astype(_BF16), vc_ref[0]))
        imp_parts.append(sum(pn[r * qb:(r + 1) * qb, :] for r in range(HPG)))
    imp = jnp.concatenate(imp_parts, axis=0)
    imp_t = sum(_dot_nt(ot_ref[...], p) for p in _split3(imp))

    jrow = lax.broadcasted_iota(jnp.int32, imp_t.shape, 0)
    forced = (jrow == 0) | (jrow == i) | (jrow == i - 1)
    sc_ref[...] = jnp.where(forced, jnp.inf, jnp.where(jrow <= i, imp_t, -jnp.inf))
    sel_t = _rank_select(sc_ref, imp_t.shape[0], i + 1, N_SELECT)
    far_t = jnp.where(jrow < i - (NEAR_BLOCKS - 1), sel_t, 0.0)
    pad = jnp.zeros((LANES - sel_t.shape[0], sel_t.shape[1]), _F32)
    sel = jnp.concatenate([sel_t, pad], axis=0).T.astype(_BF16)
    sel_far = jnp.concatenate([far_t, pad], axis=0).T.astype(_BF16)

    v = jnp.minimum(i, NEAR_BLOCKS - 1)
    ns = pl.multiple_of((i - v) * SEL_BLOCK, SEL_BLOCK)
    cn = lax.broadcasted_iota(jnp.int32, (rows, NEAR_KEYS), 1)
    causal = cn - rq <= v * SEL_BLOCK

    mk_near = _dot_nt(sel, et_ref[pl.ds(ns, NEAR_KEYS), :])
    for g in range(N_KV):
        s = _dot_nt(qz_ref[g], ks_ref[0, pl.ds(ns, NEAR_KEYS), :]) + tz_ref[v, g]
        mkg = mk_near[g * qb:(g + 1) * qb, :]
        mk = jnp.concatenate([mkg] * HPG, axis=0) > 0.5
        mk = mk & causal
        m = jnp.max(jnp.where(mk, s, NEG), axis=-1, keepdims=True)
        p = jnp.where(mk, jnp.exp(s - m), 0.0)
        m_ref[g] = jnp.broadcast_to(m, (rows, LANES))
        l_ref[g] = jnp.broadcast_to(jnp.sum(p, axis=-1, keepdims=True), (rows, LANES))
        acc_ref[g] = _dot(p.astype(_BF16), vs_ref[0, pl.ds(ns, NEAR_KEYS), :])

    n_far = ((i - v) * SEL_BLOCK + FAR_TILE - 1) // FAR_TILE

    def far_body(tau, carry):
        k0 = pl.multiple_of(tau * FAR_TILE, FAR_TILE)
        kt = ks_ref[0, pl.ds(k0, FAR_TILE), :]
        vt = vs_ref[0, pl.ds(k0, FAR_TILE), :]
        mk_all = _dot_nt(sel_far, et_ref[pl.ds(k0, FAR_TILE), :])
        for g in range(N_KV):
            cg = jnp.concatenate([c31_ref[g]] * (FAR_TILE // LANES), axis=-1)
            s = _dot_nt(qz_ref[g], kt) + cg
            mkg = mk_all[g * qb:(g + 1) * qb, :]
            mk = jnp.concatenate([mkg] * HPG, axis=0) > 0.5
            s = jnp.where(mk, s, NEG)
            m_old = m_ref[g][:, :1]
            m_new = jnp.maximum(m_old, jnp.max(s, axis=-1, keepdims=True))
            alpha = jnp.exp(m_old - m_new)
            p = jnp.exp(s - m_new)
            l_ref[g] = jnp.broadcast_to(alpha * l_ref[g][:, :1] + jnp.sum(p, axis=-1, keepdims=True),
                                        (rows, LANES))
            m_ref[g] = jnp.broadcast_to(m_new, (rows, LANES))
            acc_ref[g] = alpha * acc_ref[g] + _dot(p.astype(_BF16), vt)
        return carry

    lax.fori_loop(0, n_far, far_body, 0)
    for g in range(N_KV):
        emit(1, g, acc_ref[g] * (1.0 / l_ref[g][:, :1]))

    far_blocks = WIN_FAR_KEYS // SEL_BLOCK
    fb = jnp.maximum(i - (NEAR_BLOCKS - 1) - far_blocks, 0)
    fs = pl.multiple_of(fb * SEL_BLOCK, SEL_BLOCK)
    dd = i * qb - fs
    cf = lax.broadcasted_iota(jnp.int32, (rows, WIN_FAR_KEYS), 1)
    far_ok = (dd + rq - cf < WINDOW) & (cf < dd - (NEAR_BLOCKS - 1) * SEL_BLOCK)
    for g in range(N_KV):
        qg = qz_ref[g]
        sn = _dot_nt(qg, kw_ref[0, pl.ds(ns, NEAR_KEYS), :]) + tz_ref[v, g]
        cg = jnp.concatenate([c31_ref[g]] * (WIN_FAR_KEYS // LANES), axis=-1)
        sf = _dot_nt(qg, kw_ref[0, pl.ds(fs, WIN_FAR_KEYS), :]) + cg
        pn, pf = _softmax_parts([(sn, causal), (sf, far_ok)])
        ow = (_dot(pn.astype(_BF16), vw_ref[0, pl.ds(ns, NEAR_KEYS), :])
              + _dot(pf.astype(_BF16), vw_ref[0, pl.ds(fs, WIN_FAR_KEYS), :]))
        emit(2, g, ow)

    o_ref[0] = oacc_ref[...].astype(_BF16)


def _attn_prompt(q2, ng, kvs, kvw, kc, vc, tabs):
    bsz, t, _ = q2.shape
    nblk = t // SEL_BLOCK
    n_cmp_pad = kc.shape[1]
    rows = HPG * Q_BLOCK
    full = lambda a: pl.BlockSpec(a.shape, lambda b, i, _n=a.ndim: (0,) * _n)
    per_b = lambda a: pl.BlockSpec((1,) + a.shape[1:], lambda b, i: (b, 0, 0))
    half = lambda c: pl.BlockSpec((1, t, KV_DIM), lambda b, i: (b, 0, c))
    kern = functools.partial(_attn_prompt_kernel, n_cmp_pad=n_cmp_pad)
    ks, vs, kw, vw = kvs, kvs, kvw, kvw
    return pl.pallas_call(
        kern,
        grid=(bsz, nblk),
        in_specs=[pl.BlockSpec((1, Q_BLOCK, ATTN_DIM), lambda b, i: (b, i, 0)),
                  pl.BlockSpec((1, Q_BLOCK, LANES), lambda b, i: (b, i, 0)),
                  half(0), half(1), half(0), half(1), per_b(kc), per_b(vc),
                  full(tabs["tz"]), full(tabs["tc"]), full(tabs["c31"]), full(tabs["ot"]),
                  full(tabs["et"]), full(tabs["ex"])],
        out_specs=pl.BlockSpec((1, Q_BLOCK, ATTN_DIM), lambda b, i: (b, i, 0)),
        out_shape=jax.ShapeDtypeStruct((bsz, t, ATTN_DIM), _BF16),
        scratch_shapes=[pltpu.VMEM((N_KV, rows, KV_DIM), _BF16),
                        pltpu.VMEM((Q_BLOCK, ATTN_DIM), _F32),
                        pltpu.VMEM((N_KV, rows, KV_DIM), _F32),
                        pltpu.VMEM((N_KV, rows, LANES), _F32),
                        pltpu.VMEM((N_KV, rows, LANES), _F32),
                        pltpu.VMEM((SEL_BLOCK, N_KV * Q_BLOCK), _F32),
                        pltpu.VMEM((Q_BLOCK, 3 * ATTN_DIM), _F32)],
        compiler_params=_params("parallel", "arbitrary"),
        name="attn_prompt",
    )(q2, ng, ks, vs, kw, vw, kc, vc, tabs["tz"], tabs["tc"], tabs["c31"], tabs["ot"], tabs["et"], tabs["ex"])


def _prompt_tables(rel_bias, t):
    nblk = t // SEL_BLOCK
    n_cmp_pad = t // CMP_STRIDE
    assert nblk <= SEL_BLOCK and n_cmp_pad % LANES == 0
    rows = HPG * Q_BLOCK
    rel = rel_bias.astype(_F32)
    r_idx = np.arange(rows) // Q_BLOCK
    q_idx = np.arange(rows) % Q_BLOCK
    head = np.arange(N_KV)[:, None] * HPG + r_idx[None, :]
    c = np.arange(NEAR_KEYS)
    dist = (np.arange(NEAR_BLOCKS)[:, None, None] * SEL_BLOCK + q_idx[None, :, None] - c[None, None, :])
    bk = _rel_bucket_np(dist)
    tz = rel[bk[:, None, :, :], head[None, :, :, None]]
    c31 = rel[REL_BUCKETS - 1]
    mmv = np.arange(16)
    dist_c = q_idx[:, None] - CMP_STRIDE * (mmv[None, :] - 8) - (CMP_LEN - CMP_STRIDE - 1)
    bkc = _rel_bucket_np(dist_c)
    delta = rel[bkc[None, :, :], head[:, :, None]] - c31[head][:, :, None]
    tcf = jnp.zeros((N_KV, rows, LANES), _F32)
    tcf = tcf.at[:, :, :16].set(delta).at[:, :, 16].set(c31[head])
    tc = jnp.stack(_split3(tcf))
    c31b = jnp.broadcast_to(c31[head][:, :, None], (N_KV, rows, LANES))
    n = np.arange(n_cmp_pad) - 1
    cs = n * CMP_STRIDE
    bs = np.arange(SEL_BLOCK) * SEL_BLOCK
    ov = np.clip(np.minimum(cs[None, :] + CMP_LEN, bs[:, None] + SEL_BLOCK) - np.maximum(cs[None, :], bs[:, None]),
                 0, CMP_LEN).astype(np.float32) / CMP_LEN
    ov[:, 0] = 0.0
    ov[nblk:, :] = 0.0
    et = (np.arange(t)[:, None] // SEL_BLOCK == np.arange(LANES)[None, :]).astype(np.float32)
    return dict(tz=tz, tc=tc, c31=c31b, ot=jnp.asarray(ov, _BF16), et=jnp.asarray(et, _BF16), ex=_gate_expand())


def _gate_expand():
    ex = np.zeros((LANES, 3 * ATTN_DIM), np.float32)
    for j in range(3):
        for r in range(HPG):
            for g in range(N_KV):
                col = j * ATTN_DIM + r * KV_DIM + g * HEAD_DIM
                ex[j * N_HEADS + r * N_KV + g, col:col + HEAD_DIM] = 1.0
    return jnp.asarray(ex, _BF16)


PAGES_PER_STEP = 32
SUB_PAGES = 4


def _attn_sample_kernel(pt_ref, q_ref, ng_ref, kc_ref, vc_ref, knew_ref, wnew_ref, wst_ref, *refs,
                        past_len, n_blk, nb_rows):
    del pt_ref
    page_refs = refs[:PAGES_PER_STEP]
    (bc_ref, blast_ref, c31_ref, bnew_ref, bwin_ref, ot_ref, e64_ref, ex_ref,
     o_ref, qall_ref, gate_ref, oacc_ref, acc_ref, m_ref, l_ref, selt_ref, sc_ref) = refs[PAGES_PER_STEP:]
    k = pl.program_id(1)
    nk = pl.num_programs(1)
    tq = q_ref.shape[1]
    rows = N_KV * HPG * tq
    lane_g = lax.broadcasted_iota(jnp.int32, (tq, KV_DIM), 1) // HEAD_DIM
    rq = lax.broadcasted_iota(jnp.int32, (rows, 1), 0) % tq
    sub_keys = SUB_PAGES * PAGE_SIZE

    def emit(branch, out):
        for g in range(N_KV):
            for r in range(HPG):
                col = r * KV_DIM
                row0 = (g * HPG + r) * tq
                gt = gate_ref[:, branch * ATTN_DIM + col:branch * ATTN_DIM + col + KV_DIM]
                oacc_ref[:, col:col + KV_DIM] += jnp.where(lane_g == g, gt * out[row0:row0 + tq, :], 0.0)

    def pad_rows(x, n):
        return jnp.concatenate([x, jnp.zeros((n - x.shape[0], x.shape[1]), x.dtype)], axis=0)

    @pl.when(k == 0)
    def _():
        qf = q_ref[0].astype(_F32)
        pieces = []
        for g in range(N_KV):
            for r in range(HPG):
                pieces.append(jnp.where(lane_g == g, qf[:, r * KV_DIM:(r + 1) * KV_DIM], 0.0))
        qall = jnp.concatenate(pieces, axis=0).astype(_BF16)
        qall_ref[...] = qall
        gs = jax.nn.sigmoid(ng_ref[0])
        gate_ref[...] = sum(_dot(p, ex_ref[...]) for p in _split3(gs))
        oacc_ref[...] = jnp.zeros_like(oacc_ref)

        n_cmp_pad = kc_ref.shape[1]
        jc = lax.broadcasted_iota(jnp.int32, (rows, n_cmp_pad), 1)
        cmp_valid = (jc >= 1) & (CMP_STRIDE * jc + (CMP_LEN - CMP_STRIDE - 1) - rq <= past_len)
        s = _dot_nt(qall, kc_ref[0]) + bc_ref[...]
        (pn,) = _softmax_parts([(s, cmp_valid)])
        emit(0, _dot(pn.astype(_BF16), vc_ref[0]))
        imp_rows = []
        for g in range(N_KV):
            sg = sum(pn[(g * HPG + r) * tq:(g * HPG + r + 1) * tq, :] for r in range(HPG))
            imp_rows += [sg] * HPG
        imp = jnp.concatenate(imp_rows, axis=0)
        imp_t = sum(_dot_nt(ot_ref[...], p) for p in _split3(imp))

        jrow = lax.broadcasted_iota(jnp.int32, imp_t.shape, 0)
        tpos = past_len + lax.broadcasted_iota(jnp.int32, imp_t.shape, 1) % tq
        cur = tpos // SEL_BLOCK
        forced = (jrow == 0) | (jrow == cur) | (jrow == cur - 1)
        valid = jrow * SEL_BLOCK <= tpos
        sc_ref[...] = jnp.where(forced, jnp.inf, jnp.where(valid, imp_t, -jnp.inf))
        sel_t = _rank_select(sc_ref, nb_rows, n_blk, min(N_SELECT, n_blk))
        selt_ref[...] = pad_rows(sel_t, selt_ref.shape[0])

        kn = pad_rows(knew_ref[0, :, :KV_DIM], LANES).astype(_BF16)
        vn = pad_rows(knew_ref[0, :, KV_DIM:], LANES).astype(_BF16)
        cn = lax.broadcasted_iota(jnp.int32, (rows, LANES), 1)
        mk = (cn <= rq) & (cn < tq)
        s = _dot_nt(qall, kn) + bnew_ref[...]
        m = jnp.max(jnp.where(mk, s, NEG), axis=-1, keepdims=True)
        p = jnp.where(mk, jnp.exp(s - m), 0.0)
        m_ref[...] = jnp.broadcast_to(m, m_ref.shape)
        l_ref[...] = jnp.broadcast_to(jnp.sum(p, axis=-1, keepdims=True), l_ref.shape)
        acc_ref[...] = _dot(p.astype(_BF16), vn)

    qall = qall_ref[...]
    blk_per_step = PAGES_PER_STEP * PAGE_SIZE // SEL_BLOCK
    j0 = pl.multiple_of(k * blk_per_step, blk_per_step)
    sel_step = selt_ref[pl.ds(j0, LANES), :].T.astype(_BF16)
    c31 = jnp.concatenate([c31_ref[...]] * (sub_keys // LANES), axis=-1)
    n_sub = PAGES_PER_STEP // SUB_PAGES
    for st in range(n_sub):
        pages = page_refs[st * SUB_PAGES:(st + 1) * SUB_PAGES]
        kt = jnp.concatenate([r[0, :, :KV_DIM] for r in pages], axis=0).astype(_BF16)
        vt = jnp.concatenate([r[0, :, KV_DIM:] for r in pages], axis=0).astype(_BF16)
        mk = _dot(sel_step, e64_ref[:, st * sub_keys:(st + 1) * sub_keys]) > 0.5
        if st == n_sub - 1:
            bias = jnp.where(k == nk - 1, blast_ref[...], c31)
        else:
            bias = c31
        s = jnp.where(mk, _dot_nt(qall, kt) + bias, NEG)
        m_old = m_ref[:, :1]
        m_new = jnp.maximum(m_old, jnp.max(s, axis=-1, keepdims=True))
        alpha = jnp.exp(m_old - m_new)
        p = jnp.exp(s - m_new)
        l_ref[...] = jnp.broadcast_to(alpha * l_ref[:, :1] + jnp.sum(p, axis=-1, keepdims=True), l_ref.shape)
        m_ref[...] = jnp.broadcast_to(m_new, m_ref.shape)
        acc_ref[...] = alpha * acc_ref[...] + _dot(p.astype(_BF16), vt)

    @pl.when(k == nk - 1)
    def _():
        emit(1, acc_ref[...] * (1.0 / l_ref[:, :1]))
        lw = wst_ref.shape[1]
        kw = wst_ref[0, :, :KV_DIM].astype(_BF16)
        vw = wst_ref[0, :, KV_DIM:].astype(_BF16)
        kn = pad_rows(wnew_ref[0, :, :KV_DIM], LANES).astype(_BF16)
        vn = pad_rows(wnew_ref[0, :, KV_DIM:], LANES).astype(_BF16)
        cw = lax.broadcasted_iota(jnp.int32, (rows, lw), 1)
        dw = lw + rq - cw
        cn = lax.broadcasted_iota(jnp.int32, (rows, LANES), 1)
        pw, pnw = _softmax_parts([(_dot_nt(qall, kw) + bwin_ref[...], (dw >= 0) & (dw < WINDOW)),
                                  (_dot_nt(qall, kn) + bnew_ref[...], (cn <= rq) & (cn < tq))])
        emit(2, _dot(pw.astype(_BF16), vw) + _dot(pnw.astype(_BF16), vn))
        o_ref[0] = oacc_ref[...].astype(_BF16)


def _attn_sample(q2, ng, kc, vc, kvs_new, kvw_new, win_state, slc_pages, page_table, tabs, past_len):
    bsz, tq, _ = q2.shape
    n_pages = page_table.shape[1]
    assert n_pages % PAGES_PER_STEP == 0 and past_len == n_pages * PAGE_SIZE and past_len % SEL_BLOCK == 0
    n_steps = n_pages // PAGES_PER_STEP
    rows = N_KV * HPG * tq
    assert rows == LANES
    n_blk = -(-(past_len + tq) // SEL_BLOCK)
    nb_rows = tabs["ot"].shape[0]
    blk_per_step = PAGES_PER_STEP * PAGE_SIZE // SEL_BLOCK
    selt_rows = (n_steps - 1) * blk_per_step + LANES
    assert selt_rows >= nb_rows
    full = lambda a: pl.BlockSpec(a.shape, lambda b, k, pt, _n=a.ndim: (0,) * _n)
    per_b = lambda a: pl.BlockSpec((1,) + a.shape[1:], lambda b, k, pt: (b, 0, 0))
    page_specs = [pl.BlockSpec((1, PAGE_SIZE, 2 * KV_DIM),
                               lambda b, k, pt, _p=p: (pt[b, k * PAGES_PER_STEP + _p], 0, 0))
                  for p in range(PAGES_PER_STEP)]
    names = ["bc", "blast", "c31", "bnew", "bwin", "ot", "e64", "ex"]
    kern = functools.partial(_attn_sample_kernel, past_len=past_len, n_blk=n_blk, nb_rows=nb_rows)
    grid_spec = pltpu.PrefetchScalarGridSpec(
        num_scalar_prefetch=1,
        grid=(bsz, n_steps),
        in_specs=[per_b(q2), per_b(ng), per_b(kc), per_b(vc), per_b(kvs_new), per_b(kvw_new), per_b(win_state)]
                 + page_specs + [full(tabs[n]) for n in names],
        out_specs=pl.BlockSpec((1, tq, ATTN_DIM), lambda b, k, pt: (b, 0, 0)),
        scratch_shapes=[pltpu.VMEM((rows, KV_DIM), _BF16),
                        pltpu.VMEM((tq, 3 * ATTN_DIM), _F32),
                        pltpu.VMEM((tq, ATTN_DIM), _F32),
                        pltpu.VMEM((rows, KV_DIM), _F32),
                        pltpu.VMEM((rows, LANES), _F32),
                        pltpu.VMEM((rows, LANES), _F32),
                        pltpu.VMEM((selt_rows, rows), _F32),
                        pltpu.VMEM((nb_rows, rows), _F32)],
    )
    return pl.pallas_call(
        kern,
        grid_spec=grid_spec,
        out_shape=jax.ShapeDtypeStruct((bsz, tq, ATTN_DIM), _BF16),
        compiler_params=_params("parallel", "arbitrary"),
        name="attn_sample",
    )(page_table, q2, ng, kc, vc, kvs_new, kvw_new, win_state, *([slc_pages] * PAGES_PER_STEP),
      *[tabs[n] for n in names])


def _sample_tables(rel_bias, past_len, tq, lw, n_cmp_pad):
    rows = N_KV * HPG * tq
    rel = rel_bias.astype(_F32)
    ridx = np.arange(rows)
    head = ridx // tq
    qi = ridx % tq
    tpos = past_len + qi

    def bias_of(dist):
        return rel[_rel_bucket_np(dist), head[:, None]]

    jc = np.arange(n_cmp_pad)
    bc = bias_of(tpos[:, None] - (CMP_STRIDE * jc[None, :] + CMP_LEN - CMP_STRIDE - 1))
    sub_keys = SUB_PAGES * PAGE_SIZE
    blast = bias_of(tpos[:, None] - (past_len - sub_keys + np.arange(sub_keys))[None, :])
    bnew = bias_of(qi[:, None] - np.arange(LANES)[None, :])
    bwin = bias_of(lw + qi[:, None] - np.arange(lw)[None, :])
    c31 = jnp.broadcast_to(rel[REL_BUCKETS - 1][head][:, None], (rows, LANES))
    n_blk = -(-(past_len + tq) // SEL_BLOCK)
    nb_rows = -(-n_blk // 8) * 8
    n = jc - 1
    cs = n * CMP_STRIDE
    bs = np.arange(nb_rows) * SEL_BLOCK
    ov = np.clip(np.minimum(cs[None, :] + CMP_LEN, bs[:, None] + SEL_BLOCK) - np.maximum(cs[None, :], bs[:, None]),
                 0, CMP_LEN).astype(np.float32) / CMP_LEN
    ov[:, 0] = 0.0
    ov[n_blk:, :] = 0.0
    step_keys = PAGES_PER_STEP * PAGE_SIZE
    e64 = (np.arange(step_keys)[None, :] // SEL_BLOCK == np.arange(LANES)[:, None]).astype(np.float32)
    return dict(bc=bc, blast=blast, c31=c31, bnew=bnew, bwin=bwin, ot=jnp.asarray(ov, _BF16),
                e64=jnp.asarray(e64, _BF16), ex=_gate_expand())


def _layer_weights(w_in, phi_pe, phi_w1, phi_w2, w_attn_out, conv_w, w_conv_out, w_o, w_up, w_down):
    d = w_in.shape[0]
    o_q, o_kc, o_ks, o_kw = 0, ATTN_DIM, ATTN_DIM + 2 * KV_DIM, ATTN_DIM + 4 * KV_DIM
    o_ng = ATTN_DIM + 6 * KV_DIM
    o_glu = o_ng + 3 * N_HEADS
    o_mg = o_glu + 2 * CONV_DIM
    bf = lambda a: a.astype(_BF16)
    wq = w_in[:, o_q:o_kc].reshape(d, N_KV, HPG, HEAD_DIM).transpose(0, 2, 1, 3).reshape(d, ATTN_DIM)
    wng = w_in[:, o_ng:o_glu].reshape(d, N_KV, HPG, 3).transpose(0, 3, 2, 1).reshape(d, 3 * N_HEADS)
    wng = jnp.pad(wng, ((0, 0), (0, LANES - 3 * N_HEADS)))
    w5 = phi_w1.reshape(2, CMP_R, CMP_STRIDE, HEAD_DIM, PHI_HIDDEN)
    eye = jnp.eye(N_KV, dtype=_F32)
    w1bd = jnp.einsum("crsde,gh->csgdrhe", w5, eye).reshape(2, CMP_STRIDE, KV_DIM, CMP_R * KV_DIM)
    w2bd = jnp.einsum("che,gk->cghke", phi_w2, eye).reshape(2, KV_DIM, KV_DIM)
    rep = LANES // PHI_HIDDEN
    pe_b = jnp.broadcast_to(phi_pe.reshape(2, CMP_LEN * HEAD_DIM, 1), (2, CMP_LEN * HEAD_DIM, LANES))
    w1t = jnp.tile(phi_w1, (1, 1, rep))
    wao = w_attn_out.reshape(N_KV, HPG, HEAD_DIM, d).transpose(1, 0, 2, 3).reshape(ATTN_DIM, d)
    return dict(
        wq=bf(wq), wkc=bf(w_in[:, o_kc:o_ks]), wks=bf(w_in[:, o_ks:o_kw]), wkw=bf(w_in[:, o_kw:o_ng]),
        wng=bf(wng), wga=bf(w_in[:, o_glu:o_glu + CONV_DIM]), wgb=bf(w_in[:, o_glu + CONV_DIM:o_mg]),
        wmg=bf(w_in[:, o_mg:]), w1bd=bf(w1bd), w2bd=bf(w2bd), pe_b=pe_b, w1t=w1t,
        wao=bf(wao), wco=bf(w_conv_out), wo=bf(w_o), wup=bf(w_up), wdown=bf(w_down),
        conv_w=jnp.pad(conv_w, ((0, HALO - CONV_K), (0, 0))))


def _project(x2, g_pre, w, tm):
    q2, = _proj(x2, g_pre, [w["wq"]], "q", [_BF16], tm, 512)
    kvc, = _proj(x2, g_pre, [w["wkc"]], "plain", [_F32], tm, 512)
    kvs, kvs16 = _proj(x2, g_pre, [w["wks"]], "kv", [_F32, _BF16], tm, 512)
    kvw, kvw16 = _proj(x2, g_pre, [w["wkw"]], "kv", [_F32, _BF16], tm, 512)
    ng, = _proj(x2, g_pre, [w["wng"]], "plain", [_F32], tm, LANES)
    u, = _proj(x2, g_pre, [w["wga"], w["wgb"]], "glu", [_F32], tm, 512)
    mg, = _proj(x2, g_pre, [w["wmg"]], "sigmoid", [_BF16], tm, 512)
    return q2, kvc, kvs, kvs16, kvw, kvw16, ng, u, mg


def _finish(x2, attn2, conv, mg, w, g_post_mix, g_pre_ffn, g_post_ffn, tm):
    mixed = _mix(attn2, conv, mg, w["wao"], w["wco"], tm, 512)
    x1 = _oproj(mixed, x2, w["wo"], g_post_mix, tm)
    return _ffn(x1, g_pre_ffn, g_post_ffn, w["wup"], w["wdown"], tm, 512)


def kernel(x_prompt, x_sample, cache_kv_cmp, cache_kv_slc, state_kv_win, state_conv, page_table, w_in, phi_pe,
           phi_w1, phi_w2, rel_bias, w_attn_out, conv_w, conv_b, conv_ln_g, conv_ln_b, w_conv_out, w_o, w_up,
           w_down, g_pre_mix, g_post_mix, g_pre_ffn, g_post_ffn):
    depth = w_in.shape[0]
    bp, tp, d = x_prompt.shape
    bs, ts, _ = x_sample.shape
    n_phys = cache_kv_cmp.shape[1]
    n_pages = page_table.shape[1]
    past_len = n_pages * PAGE_SIZE
    lw = state_kv_win.shape[2]
    chunks_per_page = PAGE_SIZE // CMP_STRIDE
    chunk_cols = CMP_STRIDE * 2 * KV_DIM
    assert ts < CMP_STRIDE and tp % CMP_STRIDE == 0 and lw == WINDOW and tp >= WINDOW

    tabs_p = _prompt_tables(rel_bias, tp)
    tabs_s = _sample_tables(rel_bias, past_len, ts, lw, past_len // CMP_STRIDE)
    yp, ys = x_prompt.reshape(bp * tp, d), x_sample.reshape(bs * ts, d)
    outs = [[] for _ in range(8)]
    row = lambda a: a.reshape(1, -1)
    kv5 = lambda a, b, t: a.reshape(b, t, 2, N_KV, HEAD_DIM)
    for l in range(depth):
        w = _layer_weights(w_in[l], phi_pe[l], phi_w1[l], phi_w2[l], w_attn_out[l], conv_w[l], w_conv_out[l],
                           w_o[l], w_up[l], w_down[l])
        gpm, gqm, gpf, gqf = row(g_pre_mix[l]), row(g_post_mix[l]), row(g_pre_ffn[l]), row(g_post_ffn[l])
        cargs = (w["conv_w"], row(conv_b[l]), row(conv_ln_g[l]), row(conv_ln_b[l]))
        cmp_w = (w["w1bd"], w["w2bd"], w["pe_b"], w["w1t"])

        tm = 512
        q2, kvc, kvs, kvs16, kvw, kvw16, ng, u, mg = _project(yp, gpm, w, tm)
        n_chunk = tp // CMP_STRIDE
        kc, vc = _compress([kvc.reshape(bp, n_chunk, chunk_cols)],
                           [pl.BlockSpec((1, n_chunk, chunk_cols), lambda b, k: (b, 0, 0))],
                           (bp, 1), 1, n_chunk, bp, n_chunk, *cmp_w)
        attn2 = _attn_prompt(q2.reshape(bp, tp, ATTN_DIM), ng.reshape(bp, tp, LANES),
                             kvs16.reshape(bp, tp, 2 * KV_DIM), kvw16.reshape(bp, tp, 2 * KV_DIM), kc, vc, tabs_p)
        u3 = u.reshape(bp, tp, CONV_DIM)
        conv = _conv(u3, u3, *cargs, 256, True)
        yp = _finish(yp, attn2.reshape(bp * tp, ATTN_DIM), conv.reshape(bp * tp, CONV_DIM), mg, w, gqm, gpf, gqf, tm)
        outs[0].append(kv5(kvc, bp, tp))
        outs[2].append(kv5(kvs, bp, tp))
        outs[4].append(kv5(kvw, bp, tp)[:, tp - WINDOW:])
        outs[6].append(u3[:, tp - (CONV_K - 1):])

        tm = bs * ts
        q2, kvc, kvs, kvs16, kvw, kvw16, ng, u, mg = _project(ys, gpm, w, tm)
        cmp_pages = cache_kv_cmp[l].reshape(n_phys, chunks_per_page, chunk_cols)
        page_specs = [pl.BlockSpec((1, chunks_per_page, chunk_cols),
                                   lambda b, k, pt, _p=p: (pt[b, k * PAGES_PER_STEP + _p], 0, 0))
                      for p in range(PAGES_PER_STEP)]
        kc, vc = _compress([cmp_pages] * PAGES_PER_STEP, page_specs, (bs, n_pages // PAGES_PER_STEP),
                           PAGES_PER_STEP, chunks_per_page, bs, past_len // CMP_STRIDE, *cmp_w,
                           num_prefetch=1, prefetch_args=(page_table,))
        attn2 = _attn_sample(q2.reshape(bs, ts, ATTN_DIM), ng.reshape(bs, ts, LANES), kc, vc,
                             kvs.reshape(bs, ts, 2 * KV_DIM), kvw.reshape(bs, ts, 2 * KV_DIM),
                             state_kv_win[l].reshape(bs, lw, 2 * KV_DIM),
                             cache_kv_slc[l].reshape(n_phys, PAGE_SIZE, 2 * KV_DIM), page_table, tabs_s, past_len)
        u3 = u.reshape(bs, ts, CONV_DIM)
        hist = jnp.pad(state_conv[l], ((0, 0), (HALO - (CONV_K - 1), 0), (0, 0)))
        conv = _conv(u3, hist, *cargs, ts, False)
        ys = _finish(ys, attn2.reshape(bs * ts, ATTN_DIM), conv.reshape(bs * ts, CONV_DIM), mg, w, gqm, gpf, gqf, tm)
        outs[1].append(kv5(kvc, bs, ts))
        outs[3].append(kv5(kvs, bs, ts))
        win_rows = jnp.concatenate([state_kv_win[l], kv5(kvw, bs, ts)], axis=1)
        outs[5].append(win_rows[:, win_rows.shape[1] - min(WINDOW, win_rows.shape[1]):])
        up = jnp.concatenate([state_conv[l], u3], axis=1)
        outs[7].append(up[:, up.shape[1] - (CONV_K - 1):])

    stack = lambda i: jnp.stack(outs[i])
    return (yp.reshape(bp, tp, d), ys.reshape(bs, ts, d), stack(0), stack(1), stack(2), stack(3),
            stack(4), stack(5), stack(6), stack(7))
```

```python
import functools
import math

import jax
import jax.numpy as jnp
import numpy as np
from jax import lax
from jax.experimental import pallas as pl
from jax.experimental.pallas import tpu as pltpu

D_MODEL = 2048
N_HEADS = 16
HEAD_DIM = 64
N_KV = 4
HPG = N_HEADS // N_KV
ATTN_DIM = N_HEADS * HEAD_DIM
KV_DIM = N_KV * HEAD_DIM
CMP_LEN = 32
CMP_STRIDE = 16
CMP_R = CMP_LEN // CMP_STRIDE
PHI_HIDDEN = HEAD_DIM
SEL_BLOCK = 64
N_SELECT = 16
WINDOW = 512
Q_BLOCK = 64
CONV_DIM = D_MODEL // 2
CONV_K = 31
D_FF = 4 * D_MODEL
REL_BUCKETS = 32
REL_MAX_DIST = 128
EPS = 1e-6
NEG = -1e30
PAGE_SIZE = 128

LANES = 128
VMEM_LIMIT_BYTES = 56 * 1024 * 1024

NEAR_BLOCKS = 4
NEAR_KEYS = NEAR_BLOCKS * SEL_BLOCK
FAR_TILE = 512
WIN_FAR_KEYS = 384
ROW_CHUNK = 32
HALO = 32

_F32 = jnp.float32
_BF16 = jnp.bfloat16


def _params(*sem):
    return pltpu.CompilerParams(dimension_semantics=sem, vmem_limit_bytes=VMEM_LIMIT_BYTES)


def _dot(a, b):
    return jnp.dot(a, b, preferred_element_type=_F32)


def _dot_nt(a, b):
    return lax.dot_general(a, b, (((1,), (1,)), ((), ())), preferred_element_type=_F32)


def _split3(x):
    hi = x.astype(_BF16)
    r1 = x - hi.astype(_F32)
    mid = r1.astype(_BF16)
    lo = (r1 - mid.astype(_F32)).astype(_BF16)
    return hi, mid, lo


def _rms(x, g):
    return x * lax.rsqrt(jnp.mean(x * x, axis=-1, keepdims=True) + EPS) * g


def _rel_bucket_np(dist):
    n = np.maximum(dist, 0)
    exact = REL_BUCKETS // 2
    logb = exact + (np.log(np.maximum(n, 1).astype(np.float32) / np.float32(exact))
                    / np.float32(math.log(REL_MAX_DIST / exact)) * (REL_BUCKETS - exact)).astype(np.int32)
    return np.where(n < exact, n, np.minimum(logb, REL_BUCKETS - 1)).astype(np.int32)


def _proj_kernel(x_ref, g_ref, *refs, n_w, epilogue):
    w_refs = refs[:n_w]
    o_refs = refs[n_w:-1]
    h_ref = refs[-1]

    @pl.when(pl.program_id(1) == 0)
    def _():
        h_ref[...] = _rms(x_ref[...], g_ref[...]).astype(_BF16)

    h = h_ref[...]
    accs = [_dot(h, w[...]) for w in w_refs]
    if epilogue == "q":
        o_refs[0][...] = (accs[0] * (HEAD_DIM ** -0.5)).astype(_BF16)
    elif epilogue == "kv":
        o_refs[0][...] = accs[0]
        o_refs[1][...] = accs[0].astype(_BF16)
    elif epilogue == "plain":
        o_refs[0][...] = accs[0]
    elif epilogue == "glu":
        o_refs[0][...] = accs[0] * jax.nn.sigmoid(accs[1])
    elif epilogue == "sigmoid":
        o_refs[0][...] = jax.nn.sigmoid(accs[0]).astype(_BF16)
    else:
        raise ValueError(epilogue)


def _proj(x, g, ws, epilogue, out_dtypes, tm, tn):
    m, d = x.shape
    n = ws[0].shape[1]
    tn = min(tn, n)
    grid = (m // tm, n // tn)
    kern = functools.partial(_proj_kernel, n_w=len(ws), epilogue=epilogue)
    outs = pl.pallas_call(
        kern,
        grid=grid,
        in_specs=[pl.BlockSpec((tm, d), lambda i, j: (i, 0)),
                  pl.BlockSpec((1, d), lambda i, j: (0, 0))]
                 + [pl.BlockSpec((d, tn), lambda i, j: (0, j)) for _ in ws],
        out_specs=[pl.BlockSpec((tm, tn), lambda i, j: (i, j)) for _ in out_dtypes],
        out_shape=[jax.ShapeDtypeStruct((m, n), dt) for dt in out_dtypes],
        scratch_shapes=[pltpu.VMEM((tm, d), _BF16)],
        compiler_params=_params("parallel", "arbitrary"),
        name="proj_" + epilogue,
    )(x, g, *ws)
    return outs


def _mix_kernel(a_ref, c_ref, ga_ref, gc_ref, wa_ref, wc_ref, o_ref):
    ya = _dot(a_ref[...], wa_ref[...])
    yc = _dot(c_ref[...], wc_ref[...])
    o_ref[...] = (ga_ref[...].astype(_F32) * ya + gc_ref[...].astype(_F32) * yc).astype(_BF16)


def _mix(attn, conv, mg, wao, wco, tm, tn):
    m, ka = attn.shape
    n = wao.shape[1]
    nb = n // tn
    return pl.pallas_call(
        _mix_kernel,
        grid=(m // tm, nb),
        in_specs=[pl.BlockSpec((tm, ka), lambda i, j: (i, 0)),
                  pl.BlockSpec((tm, conv.shape[1]), lambda i, j: (i, 0)),
                  pl.BlockSpec((tm, tn), lambda i, j: (i, j)),
                  pl.BlockSpec((tm, tn), lambda i, j: (i, j + nb)),
                  pl.BlockSpec((ka, tn), lambda i, j: (0, j)),
                  pl.BlockSpec((conv.shape[1], tn), lambda i, j: (0, j))],
        out_specs=pl.BlockSpec((tm, tn), lambda i, j: (i, j)),
        out_shape=jax.ShapeDtypeStruct((m, n), _BF16),
        compiler_params=_params("parallel", "arbitrary"),
        name="mix",
    )(attn, conv, mg, mg, wao, wco)


def _oproj_kernel(mx_ref, x_ref, w_ref, g_ref, o_ref):
    y = _dot(mx_ref[...], w_ref[...])
    o_ref[...] = x_ref[...] + _rms(y, g_ref[...])


def _oproj(mixed, x, wo, g, tm):
    m, d = x.shape
    return pl.pallas_call(
        _oproj_kernel,
        grid=(m // tm,),
        in_specs=[pl.BlockSpec((tm, d), lambda i: (i, 0)),
                  pl.BlockSpec((tm, d), lambda i: (i, 0)),
                  pl.BlockSpec((d, d), lambda i: (0, 0)),
                  pl.BlockSpec((1, d), lambda i: (0, 0))],
        out_specs=pl.BlockSpec((tm, d), lambda i: (i, 0)),
        out_shape=jax.ShapeDtypeStruct((m, d), _F32),
        compiler_params=_params("parallel"),
        name="oproj",
    )(mixed, x, wo, g)


def _ffn_kernel(x_ref, gpre_ref, gpost_ref, wu_ref, wd_ref, o_ref, h_ref, acc_ref):
    j = pl.program_id(1)

    @pl.when(j == 0)
    def _():
        h_ref[...] = _rms(x_ref[...], gpre_ref[...]).astype(_BF16)
        acc_ref[...] = jnp.zeros_like(acc_ref)

    a = jnp.maximum(_dot(h_ref[...], wu_ref[...]), 0.0)
    acc_ref[...] += _dot((a * a).astype(_BF16), wd_ref[...])

    @pl.when(j == pl.num_programs(1) - 1)
    def _():
        o_ref[...] = x_ref[...] + _rms(acc_ref[...], gpost_ref[...])


def _ffn(x, gpre, gpost, wu, wd, tm, tf):
    m, d = x.shape
    f = wu.shape[1]
    return pl.pallas_call(
        _ffn_kernel,
        grid=(m // tm, f // tf),
        in_specs=[pl.BlockSpec((tm, d), lambda i, j: (i, 0)),
                  pl.BlockSpec((1, d), lambda i, j: (0, 0)),
                  pl.BlockSpec((1, d), lambda i, j: (0, 0)),
                  pl.BlockSpec((d, tf), lambda i, j: (0, j)),
                  pl.BlockSpec((tf, d), lambda i, j: (j, 0))],
        out_specs=pl.BlockSpec((tm, d), lambda i, j: (i, 0)),
        out_shape=jax.ShapeDtypeStruct((m, d), _F32),
        scratch_shapes=[pltpu.VMEM((tm, d), _BF16), pltpu.VMEM((tm, d), _F32)],
        compiler_params=_params("parallel", "arbitrary"),
        name="ffn",
    )(x, gpre, gpost, wu, wd)


def _conv_kernel(u_ref, halo_ref, w_ref, b_ref, lg_ref, lb_ref, o_ref, win_ref, *, tt, zero_first):
    c = u_ref.shape[-1]
    halo = halo_ref[0]
    if zero_first:
        halo = jnp.where(pl.program_id(1) == 0, 0.0, halo)
    win_ref[0:HALO, :] = halo
    win_ref[HALO:HALO + tt, :] = u_ref[0]
    rc = min(ROW_CHUNK, tt)
    off = HALO - (CONV_K - 1)
    for ch in range(tt // rc):
        acc = jnp.zeros((rc, c), _F32) + b_ref[...]
        for k in range(CONV_K):
            acc = acc + w_ref[k:k + 1, :] * win_ref[ch * rc + off + k:ch * rc + off + k + rc, :]
        mu = jnp.mean(acc, axis=-1, keepdims=True)
        xc = acc - mu
        var = jnp.mean(xc * xc, axis=-1, keepdims=True)
        y = xc * lax.rsqrt(var + EPS) * lg_ref[...] + lb_ref[...]
        o_ref[0, ch * rc:(ch + 1) * rc, :] = (y * jax.nn.sigmoid(y)).astype(_BF16)


def _conv(u, halo_src, w, b, lg, lb, tt, zero_first):
    bsz, t, c = u.shape
    nhb = tt // HALO
    if zero_first:
        halo_map = lambda bi, ti: (bi, jnp.maximum(ti * nhb - 1, 0), 0)
    else:
        halo_map = lambda bi, ti: (bi, 0, 0)
    kern = functools.partial(_conv_kernel, tt=tt, zero_first=zero_first)
    return pl.pallas_call(
        kern,
        grid=(bsz, t // tt),
        in_specs=[pl.BlockSpec((1, tt, c), lambda bi, ti: (bi, ti, 0)),
                  pl.BlockSpec((1, HALO, c), halo_map),
                  pl.BlockSpec((HALO, c), lambda bi, ti: (0, 0)),
                  pl.BlockSpec((1, c), lambda bi, ti: (0, 0)),
                  pl.BlockSpec((1, c), lambda bi, ti: (0, 0)),
                  pl.BlockSpec((1, c), lambda bi, ti: (0, 0))],
        out_specs=pl.BlockSpec((1, tt, c), lambda bi, ti: (bi, ti, 0)),
        out_shape=jax.ShapeDtypeStruct((bsz, t, c), _BF16),
        scratch_shapes=[pltpu.VMEM((HALO + tt, c), _F32)],
        compiler_params=_params("parallel", "arbitrary"),
        name="conv",
    )(u, halo_src, w, b, lg, lb)


def _compress_kernel(*refs, n_src, nrow, paged):
    if paged:
        refs = refs[1:]
        rows_ref = refs[-1]
        refs = refs[:-1]
    src_refs = refs[:n_src]
    w1_ref, w2_ref, pe_ref, w1t_ref, kc_ref, vc_ref, carry_ref = refs[n_src:]
    k = pl.program_id(1)

    @pl.when(k == 0)
    def _():
        carry_ref[...] = jnp.zeros_like(carry_ref)

    if paged:
        for p, r in enumerate(src_refs):
            for lc in range(2 * KV_DIM // LANES):
                rows_ref[lc, p * PAGE_SIZE:(p + 1) * PAGE_SIZE, :] = r[0, lc * LANES:(lc + 1) * LANES, :].T
    outs = []
    for c in range(2):
        acc = jnp.zeros((nrow, 2 * KV_DIM), _F32)
        for s in range(CMP_STRIDE):
            if paged:
                lcs = range(c * KV_DIM // LANES, (c + 1) * KV_DIM // LANES)
                xs = jnp.concatenate([rows_ref[lc, pl.ds(s, nrow, stride=CMP_STRIDE), :] for lc in lcs], axis=-1)
            else:
                lo = s * 2 * KV_DIM + c * KV_DIM
                xs = src_refs[0][0, :, lo:lo + KV_DIM]
            acc = acc + _dot(xs.astype(_BF16), w1_ref[c, s])
        pt = jnp.sum(pe_ref[c] * w1t_ref[c], axis=0, keepdims=True)
        pt = jnp.concatenate([pt] * (KV_DIM // LANES), axis=-1)
        a0 = acc[:, :KV_DIM]
        a1 = acc[:, KV_DIM:]
        first = lax.broadcasted_iota(jnp.int32, (nrow, KV_DIM), 0) == 0
        a0s = jnp.where(first, carry_ref[c], pltpu.roll(a0, 1, 0))
        carry_ref[c] = a0[nrow - 1:nrow, :]
        hid = jax.nn.gelu(a0s + a1 + pt)
        outs.append(_dot(hid.astype(_BF16), w2_ref[c]))
    kc_ref[0] = outs[0].astype(_BF16)
    vc_ref[0] = outs[1].astype(_BF16)


def _compress(src, src_specs, grid, nrow, n_batch, n_out_rows, w1bd, w2bd, pe_b, w1t, page_table=None):
    paged = page_table is not None
    n_src = len(src)
    kern = functools.partial(_compress_kernel, n_src=n_src, nrow=nrow, paged=paged)
    const = lambda a: pl.BlockSpec(a.shape, lambda *_, _n=a.ndim: (0,) * _n, pipeline_mode=pl.Buffered(1))
    omap = lambda b, k, *_: (b, k, 0)
    scratch = [pltpu.VMEM((2, 1, KV_DIM), _F32)]
    if paged:
        scratch.append(pltpu.VMEM((2 * KV_DIM // LANES, n_src * PAGE_SIZE, LANES), _F32))
    grid_spec = pltpu.PrefetchScalarGridSpec(
        num_scalar_prefetch=1 if paged else 0,
        grid=grid,
        in_specs=list(src_specs) + [const(w1bd), const(w2bd), const(pe_b), const(w1t)],
        out_specs=[pl.BlockSpec((1, nrow, KV_DIM), omap), pl.BlockSpec((1, nrow, KV_DIM), omap)],
        scratch_shapes=scratch,
    )
    return pl.pallas_call(
        kern,
        grid_spec=grid_spec,
        out_shape=[jax.ShapeDtypeStruct((n_batch, n_out_rows, KV_DIM), _BF16)] * 2,
        compiler_params=_params("parallel", "arbitrary"),
        name="compress_paged" if paged else "compress",
    )(*((page_table,) if paged else ()), *src, w1bd, w2bd, pe_b, w1t)


def _rank_select(sc_ref, n_rows, n_iter, n_sel):
    s = sc_ref[...]
    jio = lax.broadcasted_iota(jnp.int32, s.shape, 0)

    def body(jp, cnt):
        row = jnp.broadcast_to(sc_ref[pl.ds(jp, 1), :], s.shape)
        ge = jnp.where(row >= s, 1.0, 0.0)
        gt = jnp.where(row > s, 1.0, 0.0)
        return cnt + jnp.where(jio > jp, ge, gt)

    cnt = lax.fori_loop(0, n_iter, body, jnp.zeros(s.shape, _F32))
    return jnp.where(cnt < n_sel, 1.0, 0.0)


def _softmax_parts(parts):
    ms = [jnp.max(jnp.where(mk, s, NEG), axis=-1, keepdims=True) for s, mk in parts]
    m = functools.reduce(jnp.maximum, ms)
    ps = [jnp.where(mk, jnp.exp(s - m), 0.0) for s, mk in parts]
    l = functools.reduce(lambda a, b: a + b, [jnp.sum(p, axis=-1, keepdims=True) for p in ps])
    inv = 1.0 / jnp.maximum(l, 1e-30)
    return [p * inv for p in ps]


def _attn_prompt_kernel(q_ref, ng_ref, ks_ref, vs_ref, kw_ref, vw_ref, kc_ref, vc_ref,
                        tz_ref, tc_ref, c31_ref, ot_ref, et_ref, ex_ref,
                        o_ref, qz_ref, oacc_ref, acc_ref, m_ref, l_ref, sc_ref, gate_ref, *, n_cmp_pad):
    i = pl.program_id(1)
    qb = Q_BLOCK
    rows = HPG * qb
    lane_g = lax.broadcasted_iota(jnp.int32, (qb, KV_DIM), 1) // HEAD_DIM

    for g in range(N_KV):
        for r in range(HPG):
            slab = q_ref[0, :, r * KV_DIM:(r + 1) * KV_DIM].astype(_F32)
            qz_ref[g, r * qb:(r + 1) * qb, :] = jnp.where(lane_g == g, slab, 0.0).astype(_BF16)

    gs = jax.nn.sigmoid(ng_ref[0])
    gate_ref[...] = sum(_dot(p, ex_ref[...]) for p in _split3(gs))
    oacc_ref[...] = jnp.zeros_like(oacc_ref)

    def emit(branch, g, out_g):
        for r in range(HPG):
            col = r * KV_DIM
            gt = gate_ref[:, branch * ATTN_DIM + col:branch * ATTN_DIM + col + KV_DIM]
            oacc_ref[:, col:col + KV_DIM] += jnp.where(lane_g == g, gt * out_g[r * qb:(r + 1) * qb, :], 0.0)

    rq = lax.broadcasted_iota(jnp.int32, (rows, 1), 0) % qb

    jc = lax.broadcasted_iota(jnp.int32, (rows, n_cmp_pad), 1)
    cmp_valid = (jc >= 1) & (CMP_STRIDE * jc + (CMP_LEN - CMP_STRIDE - 1) - rq <= i * qb)
    mm = lax.broadcasted_iota(jnp.int32, (LANES, n_cmp_pad), 0)
    jj = lax.broadcasted_iota(jnp.int32, (LANES, n_cmp_pad), 1)
    shift = jnp.where(((mm < 16) & (jj - mm == 4 * i - 8)) | (mm == 16), 1.0, 0.0).astype(_BF16)
    imp_parts = []
    for g in range(N_KV):
        s = _dot_nt(qz_ref[g], kc_ref[0])
        s = s + sum(_dot(tc_ref[p, g], shift) for p in range(3))
        (pn,) = _softmax_parts([(s, cmp_valid)])
        emit(0, g, _dot(pn.astype(_BF16), vc_ref[0]))
        imp_parts.append(sum(pn[r * qb:(r + 1) * qb, :] for r in range(HPG)))
    imp = jnp.concatenate(imp_parts, axis=0)
    imp_t = sum(_dot_nt(ot_ref[...], p) for p in _split3(imp))

    jrow = lax.broadcasted_iota(jnp.int32, imp_t.shape, 0)
    forced = (jrow == 0) | (jrow == i) | (jrow == i - 1)
    sc_ref[...] = jnp.where(forced, jnp.inf, jnp.where(jrow <= i, imp_t, -jnp.inf))
    sel_t = _rank_select(sc_ref, imp_t.shape[0], i + 1, N_SELECT)
    far_t = jnp.where(jrow < i - (NEAR_BLOCKS - 1), sel_t, 0.0)
    pad = jnp.zeros((LANES - sel_t.shape[0], sel_t.shape[1]), _F32)
    sel = jnp.concatenate([sel_t, pad], axis=0).T.astype(_BF16)
    sel_far = jnp.concatenate([far_t, pad], axis=0).T.astype(_BF16)

    v = jnp.minimum(i, NEAR_BLOCKS - 1)
    ns = pl.multiple_of((i - v) * SEL_BLOCK, SEL_BLOCK)
    cn = lax.broadcasted_iota(jnp.int32, (rows, NEAR_KEYS), 1)
    causal = cn - rq <= v * SEL_BLOCK

    mk_near = _dot_nt(sel, et_ref[pl.ds(ns, NEAR_KEYS), :])
    for g in range(N_KV):
        s = _dot_nt(qz_ref[g], ks_ref[0, pl.ds(ns, NEAR_KEYS), :]) + tz_ref[v, g]
        mkg = mk_near[g * qb:(g + 1) * qb, :]
        mk = jnp.concatenate([mkg] * HPG, axis=0) > 0.5
        mk = mk & causal
        m = jnp.max(jnp.where(mk, s, NEG), axis=-1, keepdims=True)
        p = jnp.where(mk, jnp.exp(s - m), 0.0)
        m_ref[g] = jnp.broadcast_to(m, (rows, LANES))
        l_ref[g] = jnp.broadcast_to(jnp.sum(p, axis=-1, keepdims=True), (rows, LANES))
        acc_ref[g] = _dot(p.astype(_BF16), vs_ref[0, pl.ds(ns, NEAR_KEYS), :])

    n_far = ((i - v) * SEL_BLOCK + FAR_TILE - 1) // FAR_TILE

    def far_body(tau, carry):
        k0 = pl.multiple_of(tau * FAR_TILE, FAR_TILE)
        kt = ks_ref[0, pl.ds(k0, FAR_TILE), :]
        vt = vs_ref[0, pl.ds(k0, FAR_TILE), :]
        mk_all = _dot_nt(sel_far, et_ref[pl.ds(k0, FAR_TILE), :])
        for g in range(N_KV):
            cg = jnp.concatenate([c31_ref[g]] * (FAR_TILE // LANES), axis=-1)
            s = _dot_nt(qz_ref[g], kt) + cg
            mkg = mk_all[g * qb:(g + 1) * qb, :]
            mk = jnp.concatenate([mkg] * HPG, axis=0) > 0.5
            s = jnp.where(mk, s, NEG)
            m_old = m_ref[g][:, :1]
            m_new = jnp.maximum(m_old, jnp.max(s, axis=-1, keepdims=True))
            alpha = jnp.exp(m_old - m_new)
            p = jnp.exp(s - m_new)
            l_ref[g] = jnp.broadcast_to(alpha * l_ref[g][:, :1] + jnp.sum(p, axis=-1, keepdims=True),
                                        (rows, LANES))
            m_ref[g] = jnp.broadcast_to(m_new, (rows, LANES))
            acc_ref[g] = alpha * acc_ref[g] + _dot(p.astype(_BF16), vt)
        return carry

    lax.fori_loop(0, n_far, far_body, 0)
    for g in range(N_KV):
        emit(1, g, acc_ref[g] * (1.0 / l_ref[g][:, :1]))

    far_blocks = WIN_FAR_KEYS // SEL_BLOCK
    fb = jnp.maximum(i - (NEAR_BLOCKS - 1) - far_blocks, 0)
    fs = pl.multiple_of(fb * SEL_BLOCK, SEL_BLOCK)
    dd = i * qb - fs
    cf = lax.broadcasted_iota(jnp.int32, (rows, WIN_FAR_KEYS), 1)
    far_ok = (dd + rq - cf < WINDOW) & (cf < dd - (NEAR_BLOCKS - 1) * SEL_BLOCK)
    for g in range(N_KV):
        qg = qz_ref[g]
        sn = _dot_nt(qg, kw_ref[0, pl.ds(ns, NEAR_KEYS), :]) + tz_ref[v, g]
        cg = jnp.concatenate([c31_ref[g]] * (WIN_FAR_KEYS // LANES), axis=-1)
        sf = _dot_nt(qg, kw_ref[0, pl.ds(fs, WIN_FAR_KEYS), :]) + cg
        pn, pf = _softmax_parts([(sn, causal), (sf, far_ok)])
        ow = (_dot(pn.astype(_BF16), vw_ref[0, pl.ds(ns, NEAR_KEYS), :])
              + _dot(pf.astype(_BF16), vw_ref[0, pl.ds(fs, WIN_FAR_KEYS), :]))
        emit(2, g, ow)

    o_ref[0] = oacc_ref[...].astype(_BF16)


def _attn_prompt(q2, ng, kvs, kvw, kc, vc, tabs):
    bsz, t, _ = q2.shape
    nblk = t // SEL_BLOCK
    n_cmp_pad = kc.shape[1]
    rows = HPG * Q_BLOCK
    full = lambda a: pl.BlockSpec(a.shape, lambda b, i, _n=a.ndim: (0,) * _n)
    per_b = lambda a: pl.BlockSpec((1,) + a.shape[1:], lambda b, i: (b, 0, 0))
    half = lambda c: pl.BlockSpec((1, t, KV_DIM), lambda b, i: (b, 0, c))
    kern = functools.partial(_attn_prompt_kernel, n_cmp_pad=n_cmp_pad)
    ks, vs, kw, vw = kvs, kvs, kvw, kvw
    return pl.pallas_call(
        kern,
        grid=(bsz, nblk),
        in_specs=[pl.BlockSpec((1, Q_BLOCK, ATTN_DIM), lambda b, i: (b, i, 0)),
                  pl.BlockSpec((1, Q_BLOCK, LANES), lambda b, i: (b, i, 0)),
                  half(0), half(1), half(0), half(1), per_b(kc), per_b(vc),
                  full(tabs["tz"]), full(tabs["tc"]), full(tabs["c31"]), full(tabs["ot"]),
                  full(tabs["et"]), full(tabs["ex"])],
        out_specs=pl.BlockSpec((1, Q_BLOCK, ATTN_DIM), lambda b, i: (b, i, 0)),
        out_shape=jax.ShapeDtypeStruct((bsz, t, ATTN_DIM), _BF16),
        scratch_shapes=[pltpu.VMEM((N_KV, rows, KV_DIM), _BF16),
                        pltpu.VMEM((Q_BLOCK, ATTN_DIM), _F32),
                        pltpu.VMEM((N_KV, rows, KV_DIM), _F32),
                        pltpu.VMEM((N_KV, rows, LANES), _F32),
                        pltpu.VMEM((N_KV, rows, LANES), _F32),
                        pltpu.VMEM((SEL_BLOCK, N_KV * Q_BLOCK), _F32),
                        pltpu.VMEM((Q_BLOCK, 3 * ATTN_DIM), _F32)],
        compiler_params=_params("parallel", "arbitrary"),
        name="attn_prompt",
    )(q2, ng, ks, vs, kw, vw, kc, vc, tabs["tz"], tabs["tc"], tabs["c31"], tabs["ot"], tabs["et"], tabs["ex"])


def _bias_table(rel, dist, head):
    nmax = max(int(dist.max()), 1) + 1
    bk = _rel_bucket_np(np.arange(nmax))
    rel_h = rel[:, head]
    shape = np.broadcast_shapes(dist.shape, head.shape + (1,))
    out = jnp.broadcast_to(rel_h[0][..., None], shape)
    dist = lax.optimization_barrier(jnp.asarray(dist, jnp.int32))
    for b in range(1, int(bk.max()) + 1):
        first = int(np.argmax(bk >= b))
        out = jnp.where(dist >= first, rel_h[b][..., None], out)
    return out


def _prompt_tables(rel_bias, t):
    nblk = t // SEL_BLOCK
    n_cmp_pad = t // CMP_STRIDE
    assert nblk <= SEL_BLOCK and n_cmp_pad % LANES == 0
    rows = HPG * Q_BLOCK
    rel = rel_bias.astype(_F32)
    r_idx = np.arange(rows) // Q_BLOCK
    q_idx = np.arange(rows) % Q_BLOCK
    head = np.arange(N_KV)[:, None] * HPG + r_idx[None, :]
    c = np.arange(NEAR_KEYS)
    dist = (np.arange(NEAR_BLOCKS)[:, None, None] * SEL_BLOCK + q_idx[None, :, None] - c[None, None, :])
    tz = _bias_table(rel, dist[:, None, :, :], head[None, :, :])
    c31 = rel[REL_BUCKETS - 1]
    mmv = np.arange(16)
    dist_c = q_idx[:, None] - CMP_STRIDE * (mmv[None, :] - 8) - (CMP_LEN - CMP_STRIDE - 1)
    delta = _bias_table(rel, dist_c[None, :, :], head) - c31[head][:, :, None]
    tcf = jnp.zeros((N_KV, rows, LANES), _F32)
    tcf = tcf.at[:, :, :16].set(delta).at[:, :, 16].set(c31[head])
    tc = jnp.stack(_split3(tcf))
    c31b = jnp.broadcast_to(c31[head][:, :, None], (N_KV, rows, LANES))
    n = np.arange(n_cmp_pad) - 1
    cs = n * CMP_STRIDE
    bs = np.arange(SEL_BLOCK) * SEL_BLOCK
    ov = np.clip(np.minimum(cs[None, :] + CMP_LEN, bs[:, None] + SEL_BLOCK) - np.maximum(cs[None, :], bs[:, None]),
                 0, CMP_LEN).astype(np.float32) / CMP_LEN
    ov[:, 0] = 0.0
    ov[nblk:, :] = 0.0
    et = (np.arange(t)[:, None] // SEL_BLOCK == np.arange(LANES)[None, :]).astype(np.float32)
    return dict(tz=tz, tc=tc, c31=c31b, ot=jnp.asarray(ov, _BF16), et=jnp.asarray(et, _BF16), ex=_gate_expand())


def _gate_expand():
    ex = np.zeros((LANES, 3 * ATTN_DIM), np.float32)
    for j in range(3):
        for r in range(HPG):
            for g in range(N_KV):
                col = j * ATTN_DIM + r * KV_DIM + g * HEAD_DIM
                ex[j * N_HEADS + r * N_KV + g, col:col + HEAD_DIM] = 1.0
    return jnp.asarray(ex, _BF16)


PAGES_PER_STEP = 32
SUB_PAGES = 4


def _attn_sample_kernel(pt_ref, q_ref, ng_ref, kc_ref, vc_ref, knew_ref, wnew_ref, wst_ref, *refs,
                        past_len, n_blk, nb_rows):
    del pt_ref
    page_refs = refs[:PAGES_PER_STEP]
    (bc_ref, blast_ref, c31_ref, bnew_ref, bwin_ref, ot_ref, e64_ref, ex_ref,
     o_ref, qall_ref, gate_ref, oacc_ref, acc_ref, m_ref, l_ref, selt_ref, sc_ref) = refs[PAGES_PER_STEP:]
    k = pl.program_id(1)
    nk = pl.num_programs(1)
    tq = q_ref.shape[1]
    rows = N_KV * HPG * tq
    lane_g = lax.broadcasted_iota(jnp.int32, (tq, KV_DIM), 1) // HEAD_DIM
    rq = lax.broadcasted_iota(jnp.int32, (rows, 1), 0) % tq
    sub_keys = SUB_PAGES * PAGE_SIZE

    def emit(branch, out):
        for g in range(N_KV):
            for r in range(HPG):
                col = r * KV_DIM
                row0 = (g * HPG + r) * tq
                gt = gate_ref[:, branch * ATTN_DIM + col:branch * ATTN_DIM + col + KV_DIM]
                oacc_ref[:, col:col + KV_DIM] += jnp.where(lane_g == g, gt * out[row0:row0 + tq, :], 0.0)

    def pad_rows(x, n):
        return jnp.concatenate([x, jnp.zeros((n - x.shape[0], x.shape[1]), x.dtype)], axis=0)

    @pl.when(k == 0)
    def _():
        qf = q_ref[0].astype(_F32)
        pieces = []
        for g in range(N_KV):
            for r in range(HPG):
                pieces.append(jnp.where(lane_g == g, qf[:, r * KV_DIM:(r + 1) * KV_DIM], 0.0))
        qall = jnp.concatenate(pieces, axis=0).astype(_BF16)
        qall_ref[...] = qall
        gs = jax.nn.sigmoid(ng_ref[0])
        gate_ref[...] = sum(_dot(p, ex_ref[...]) for p in _split3(gs))
        oacc_ref[...] = jnp.zeros_like(oacc_ref)

        n_cmp_pad = kc_ref.shape[1]
        jc = lax.broadcasted_iota(jnp.int32, (rows, n_cmp_pad), 1)
        cmp_valid = (jc >= 1) & (CMP_STRIDE * jc + (CMP_LEN - CMP_STRIDE - 1) - rq <= past_len)
        s = _dot_nt(qall, kc_ref[0]) + bc_ref[...]
        (pn,) = _softmax_parts([(s, cmp_valid)])
        emit(0, _dot(pn.astype(_BF16), vc_ref[0]))
        imp_rows = []
        for g in range(N_KV):
            sg = sum(pn[(g * HPG + r) * tq:(g * HPG + r + 1) * tq, :] for r in range(HPG))
            imp_rows += [sg] * HPG
        imp = jnp.concatenate(imp_rows, axis=0)
        imp_t = sum(_dot_nt(ot_ref[...], p) for p in _split3(imp))

        jrow = lax.broadcasted_iota(jnp.int32, imp_t.shape, 0)
        tpos = past_len + lax.broadcasted_iota(jnp.int32, imp_t.shape, 1) % tq
        cur = tpos // SEL_BLOCK
        forced = (jrow == 0) | (jrow == cur) | (jrow == cur - 1)
        valid = jrow * SEL_BLOCK <= tpos
        sc_ref[...] = jnp.where(forced, jnp.inf, jnp.where(valid, imp_t, -jnp.inf))
        sel_t = _rank_select(sc_ref, nb_rows, n_blk, min(N_SELECT, n_blk))
        selt_ref[...] = pad_rows(sel_t, selt_ref.shape[0])

        kn = pad_rows(knew_ref[0, :, :KV_DIM], LANES).astype(_BF16)
        vn = pad_rows(knew_ref[0, :, KV_DIM:], LANES).astype(_BF16)
        cn = lax.broadcasted_iota(jnp.int32, (rows, LANES), 1)
        mk = (cn <= rq) & (cn < tq)
        s = _dot_nt(qall, kn) + bnew_ref[...]
        m = jnp.max(jnp.where(mk, s, NEG), axis=-1, keepdims=True)
        p = jnp.where(mk, jnp.exp(s - m), 0.0)
        m_ref[...] = jnp.broadcast_to(m, m_ref.shape)
        l_ref[...] = jnp.broadcast_to(jnp.sum(p, axis=-1, keepdims=True), l_ref.shape)
        acc_ref[...] = _dot(p.astype(_BF16), vn)

    qall = qall_ref[...]
    blk_per_step = PAGES_PER_STEP * PAGE_SIZE // SEL_BLOCK
    j0 = pl.multiple_of(k * blk_per_step, blk_per_step)
    sel_step = selt_ref[pl.ds(j0, LANES), :].T.astype(_BF16)
    c31 = jnp.concatenate([c31_ref[...]] * (sub_keys // LANES), axis=-1)
    n_sub = PAGES_PER_STEP // SUB_PAGES
    for st in range(n_sub):
        pages = page_refs[st * SUB_PAGES:(st + 1) * SUB_PAGES]
        kt = jnp.concatenate([r[0, :KV_DIM, :] for r in pages], axis=1).astype(_BF16)
        vt = jnp.concatenate([r[0, KV_DIM:, :] for r in pages], axis=1).astype(_BF16)
        mk = _dot(sel_step, e64_ref[:, st * sub_keys:(st + 1) * sub_keys]) > 0.5
        if st == n_sub - 1:
            bias = jnp.where(k == nk - 1, blast_ref[...], c31)
        else:
            bias = c31
        s = jnp.where(mk, _dot(qall, kt) + bias, NEG)
        m_old = m_ref[:, :1]
        m_new = jnp.maximum(m_old, jnp.max(s, axis=-1, keepdims=True))
        alpha = jnp.exp(m_old - m_new)
        p = jnp.exp(s - m_new)
        l_ref[...] = jnp.broadcast_to(alpha * l_ref[:, :1] + jnp.sum(p, axis=-1, keepdims=True), l_ref.shape)
        m_ref[...] = jnp.broadcast_to(m_new, m_ref.shape)
        acc_ref[...] = alpha * acc_ref[...] + _dot_nt(p.astype(_BF16), vt)

    @pl.when(k == nk - 1)
    def _():
        emit(1, acc_ref[...] * (1.0 / l_ref[:, :1]))
        lw = wst_ref.shape[2]
        kw = wst_ref[0, :KV_DIM, :].astype(_BF16)
        vw = wst_ref[0, KV_DIM:, :].astype(_BF16)
        kn = pad_rows(wnew_ref[0, :, :KV_DIM], LANES).astype(_BF16)
        vn = pad_rows(wnew_ref[0, :, KV_DIM:], LANES).astype(_BF16)
        cw = lax.broadcasted_iota(jnp.int32, (rows, lw), 1)
        dw = lw + rq - cw
        cn = lax.broadcasted_iota(jnp.int32, (rows, LANES), 1)
        pw, pnw = _softmax_parts([(_dot(qall, kw) + bwin_ref[...], (dw >= 0) & (dw < WINDOW)),
                                  (_dot_nt(qall, kn) + bnew_ref[...], (cn <= rq) & (cn < tq))])
        emit(2, _dot_nt(pw.astype(_BF16), vw) + _dot(pnw.astype(_BF16), vn))
        o_ref[0] = oacc_ref[...].astype(_BF16)


def _page_specs():
    return [pl.BlockSpec((1, 2 * KV_DIM, PAGE_SIZE),
                         lambda b, k, pt, _p=p: (pt[b, k * PAGES_PER_STEP + _p], 0, 0))
            for p in range(PAGES_PER_STEP)]


def _transposed_rows(a):
    n, rows = a.shape[:2]
    return jnp.transpose(a, (0, 2, 3, 4, 1)).reshape(n, 2 * KV_DIM, rows)


def _attn_sample(q2, ng, kc, vc, kvs_new, kvw_new, win_state, slc_pages, page_table, tabs, past_len):
    bsz, tq, _ = q2.shape
    n_pages = page_table.shape[1]
    assert n_pages % PAGES_PER_STEP == 0 and past_len == n_pages * PAGE_SIZE and past_len % SEL_BLOCK == 0
    n_steps = n_pages // PAGES_PER_STEP
    rows = N_KV * HPG * tq
    assert rows == LANES
    n_blk = -(-(past_len + tq) // SEL_BLOCK)
    nb_rows = tabs["ot"].shape[0]
    blk_per_step = PAGES_PER_STEP * PAGE_SIZE // SEL_BLOCK
    selt_rows = (n_steps - 1) * blk_per_step + LANES
    assert selt_rows >= nb_rows
    full = lambda a: pl.BlockSpec(a.shape, lambda b, k, pt, _n=a.ndim: (0,) * _n)
    per_b = lambda a: pl.BlockSpec((1,) + a.shape[1:], lambda b, k, pt: (b, 0, 0))
    page_specs = _page_specs()
    names =["bc", "blast", "c31", "bnew", "bwin", "ot", "e64", "ex"]
    kern = functools.partial(_attn_sample_kernel, past_len=past_len, n_blk=n_blk, nb_rows=nb_rows)
    grid_spec = pltpu.PrefetchScalarGridSpec(
        num_scalar_prefetch=1,
        grid=(bsz, n_steps),
        in_specs=[per_b(q2), per_b(ng), per_b(kc), per_b(vc), per_b(kvs_new), per_b(kvw_new), per_b(win_state)]
                 + page_specs + [full(tabs[n]) for n in names],
        out_specs=pl.BlockSpec((1, tq, ATTN_DIM), lambda b, k, pt: (b, 0, 0)),
        scratch_shapes=[pltpu.VMEM((rows, KV_DIM), _BF16),
                        pltpu.VMEM((tq, 3 * ATTN_DIM), _F32),
                        pltpu.VMEM((tq, ATTN_DIM), _F32),
                        pltpu.VMEM((rows, KV_DIM), _F32),
                        pltpu.VMEM((rows, LANES), _F32),
                        pltpu.VMEM((rows, LANES), _F32),
                        pltpu.VMEM((selt_rows, rows), _F32),
                        pltpu.VMEM((nb_rows, rows), _F32)],
    )
    return pl.pallas_call(
        kern,
        grid_spec=grid_spec,
        out_shape=jax.ShapeDtypeStruct((bsz, tq, ATTN_DIM), _BF16),
        compiler_params=_params("parallel", "arbitrary"),
        name="attn_sample",
    )(page_table, q2, ng, kc, vc, kvs_new, kvw_new, win_state, *([slc_pages] * PAGES_PER_STEP),
      *[tabs[n] for n in names])


def _sample_tables(rel_bias, past_len, tq, lw, n_cmp_pad):
    rows = N_KV * HPG * tq
    rel = rel_bias.astype(_F32)
    ridx = np.arange(rows)
    head = ridx // tq
    qi = ridx % tq
    tpos = past_len + qi

    bias_of = lambda dist: _bias_table(rel, dist, head)

    jc = np.arange(n_cmp_pad)
    bc = bias_of(tpos[:, None] - (CMP_STRIDE * jc[None, :] + CMP_LEN - CMP_STRIDE - 1))
    sub_keys = SUB_PAGES * PAGE_SIZE
    blast = bias_of(tpos[:, None] - (past_len - sub_keys + np.arange(sub_keys))[None, :])
    bnew = bias_of(qi[:, None] - np.arange(LANES)[None, :])
    bwin = bias_of(lw + qi[:, None] - np.arange(lw)[None, :])
    c31 = jnp.broadcast_to(rel[REL_BUCKETS - 1][head][:, None], (rows, LANES))
    n_blk = -(-(past_len + tq) // SEL_BLOCK)
    nb_rows = -(-n_blk // 8) * 8
    n = jc - 1
    cs = n * CMP_STRIDE
    bs = np.arange(nb_rows) * SEL_BLOCK
    ov = np.clip(np.minimum(cs[None, :] + CMP_LEN, bs[:, None] + SEL_BLOCK) - np.maximum(cs[None, :], bs[:, None]),
                 0, CMP_LEN).astype(np.float32) / CMP_LEN
    ov[:, 0] = 0.0
    ov[n_blk:, :] = 0.0
    step_keys = PAGES_PER_STEP * PAGE_SIZE
    e64 = (np.arange(step_keys)[None, :] // SEL_BLOCK == np.arange(LANES)[:, None]).astype(np.float32)
    return dict(bc=bc, blast=blast, c31=c31, bnew=bnew, bwin=bwin, ot=jnp.asarray(ov, _BF16),
                e64=jnp.asarray(e64, _BF16), ex=_gate_expand())


def _layer_weights(w_in, phi_pe, phi_w1, phi_w2, w_attn_out, conv_w, w_conv_out, w_o, w_up, w_down):
    d = w_in.shape[0]
    o_q, o_kc, o_ks, o_kw = 0, ATTN_DIM, ATTN_DIM + 2 * KV_DIM, ATTN_DIM + 4 * KV_DIM
    o_ng = ATTN_DIM + 6 * KV_DIM
    o_glu = o_ng + 3 * N_HEADS
    o_mg = o_glu + 2 * CONV_DIM
    bf = lambda a: a.astype(_BF16)
    wq = w_in[:, o_q:o_kc].reshape(d, N_KV, HPG, HEAD_DIM).transpose(0, 2, 1, 3).reshape(d, ATTN_DIM)
    wng = w_in[:, o_ng:o_glu].reshape(d, N_KV, HPG, 3).transpose(0, 3, 2, 1).reshape(d, 3 * N_HEADS)
    wng = jnp.pad(wng, ((0, 0), (0, LANES - 3 * N_HEADS)))
    w5 = phi_w1.reshape(2, CMP_R, CMP_STRIDE, HEAD_DIM, PHI_HIDDEN)
    eye = jnp.eye(N_KV, dtype=_F32)
    w1bd = jnp.einsum("crsde,gh->csgdrhe", w5, eye).reshape(2, CMP_STRIDE, KV_DIM, CMP_R * KV_DIM)
    w2bd = jnp.einsum("che,gk->cghke", phi_w2, eye).reshape(2, KV_DIM, KV_DIM)
    rep = LANES // PHI_HIDDEN
    pe_b = jnp.broadcast_to(phi_pe.reshape(2, CMP_LEN * HEAD_DIM, 1), (2, CMP_LEN * HEAD_DIM, LANES))
    w1t = jnp.tile(phi_w1, (1, 1, rep))
    wao = w_attn_out.reshape(N_KV, HPG, HEAD_DIM, d).transpose(1, 0, 2, 3).reshape(ATTN_DIM, d)
    return dict(
        wq=bf(wq), wkc=bf(w_in[:, o_kc:o_ks]), wks=bf(w_in[:, o_ks:o_kw]), wkw=bf(w_in[:, o_kw:o_ng]),
        wng=bf(wng), wga=bf(w_in[:, o_glu:o_glu + CONV_DIM]), wgb=bf(w_in[:, o_glu + CONV_DIM:o_mg]),
        wmg=bf(w_in[:, o_mg:]), w1bd=bf(w1bd), w2bd=bf(w2bd), pe_b=pe_b, w1t=w1t,
        wao=bf(wao), wco=bf(w_conv_out), wo=bf(w_o), wup=bf(w_up), wdown=bf(w_down),
        conv_w=jnp.pad(conv_w, ((0, HALO - CONV_K), (0, 0))))


def _project(x2, g_pre, w, tm):
    q2, = _proj(x2, g_pre, [w["wq"]], "q", [_BF16], tm, 512)
    kvc, = _proj(x2, g_pre, [w["wkc"]], "plain", [_F32], tm, 512)
    kvs, kvs16 = _proj(x2, g_pre, [w["wks"]], "kv", [_F32, _BF16], tm, 512)
    kvw, kvw16 = _proj(x2, g_pre, [w["wkw"]], "kv", [_F32, _BF16], tm, 512)
    ng, = _proj(x2, g_pre, [w["wng"]], "plain", [_F32], tm, LANES)
    u, = _proj(x2, g_pre, [w["wga"], w["wgb"]], "glu", [_F32], tm, 512)
    mg, = _proj(x2, g_pre, [w["wmg"]], "sigmoid", [_BF16], tm, 512)
    return q2, kvc, kvs, kvs16, kvw, kvw16, ng, u, mg


def _finish(x2, attn2, conv, mg, w, g_post_mix, g_pre_ffn, g_post_ffn, tm):
    mixed = _mix(attn2, conv, mg, w["wao"], w["wco"], tm, 512)
    x1 = _oproj(mixed, x2, w["wo"], g_post_mix, tm)
    return _ffn(x1, g_pre_ffn, g_post_ffn, w["wup"], w["wdown"], tm, 512)


def kernel(x_prompt, x_sample, cache_kv_cmp, cache_kv_slc, state_kv_win, state_conv, page_table, w_in, phi_pe,
           phi_w1, phi_w2, rel_bias, w_attn_out, conv_w, conv_b, conv_ln_g, conv_ln_b, w_conv_out, w_o, w_up,
           w_down, g_pre_mix, g_post_mix, g_pre_ffn, g_post_ffn):
    depth = w_in.shape[0]
    bp, tp, d = x_prompt.shape
    bs, ts, _ = x_sample.shape
    n_phys = cache_kv_cmp.shape[1]
    n_pages = page_table.shape[1]
    past_len = n_pages * PAGE_SIZE
    lw = state_kv_win.shape[2]
    chunks_per_page = PAGE_SIZE // CMP_STRIDE
    chunk_cols = CMP_STRIDE * 2 * KV_DIM
    assert ts < CMP_STRIDE and tp % CMP_STRIDE == 0 and lw == WINDOW and tp >= WINDOW

    tabs_p = _prompt_tables(rel_bias, tp)
    tabs_s = _sample_tables(rel_bias, past_len, ts, lw, past_len // CMP_STRIDE)
    yp, ys = x_prompt.reshape(bp * tp, d), x_sample.reshape(bs * ts, d)
    outs = [[] for _ in range(8)]
    row = lambda a: a.reshape(1, -1)
    kv5 = lambda a, b, t: a.reshape(b, t, 2, N_KV, HEAD_DIM)
    for l in range(depth):
        w = _layer_weights(w_in[l], phi_pe[l], phi_w1[l], phi_w2[l], w_attn_out[l], conv_w[l], w_conv_out[l],
                           w_o[l], w_up[l], w_down[l])
        gpm, gqm, gpf, gqf = row(g_pre_mix[l]), row(g_post_mix[l]), row(g_pre_ffn[l]), row(g_post_ffn[l])
        cargs = (w["conv_w"], row(conv_b[l]), row(conv_ln_g[l]), row(conv_ln_b[l]))
        cmp_w = (w["w1bd"], w["w2bd"], w["pe_b"], w["w1t"])

        tm = 512
        q2, kvc, kvs, kvs16, kvw, kvw16, ng, u, mg = _project(yp, gpm, w, tm)
        n_chunk = tp // CMP_STRIDE
        kc, vc = _compress([kvc.reshape(bp, n_chunk, chunk_cols)],
                           [pl.BlockSpec((1, n_chunk, chunk_cols), lambda b, k: (b, 0, 0))],
                           (bp, 1), n_chunk, bp, n_chunk, *cmp_w)
        attn2 = _attn_prompt(q2.reshape(bp, tp, ATTN_DIM), ng.reshape(bp, tp, LANES),
                             kvs16.reshape(bp, tp, 2 * KV_DIM), kvw16.reshape(bp, tp, 2 * KV_DIM), kc, vc, tabs_p)
        u3 = u.reshape(bp, tp, CONV_DIM)
        conv = _conv(u3, u3, *cargs, 256, True)
        yp = _finish(yp, attn2.reshape(bp * tp, ATTN_DIM), conv.reshape(bp * tp, CONV_DIM), mg, w, gqm, gpf, gqf, tm)
        outs[0].append(kv5(kvc, bp, tp))
        outs[2].append(kv5(kvs, bp, tp))
        outs[4].append(kv5(kvw, bp, tp)[:, tp - WINDOW:])
        outs[6].append(u3[:, tp - (CONV_K - 1):])

        tm = bs * ts
        q2, kvc, kvs, kvs16, kvw, kvw16, ng, u, mg = _project(ys, gpm, w, tm)
        kc, vc = _compress([_transposed_rows(cache_kv_cmp[l])] * PAGES_PER_STEP, _page_specs(),
                           (bs, n_pages // PAGES_PER_STEP), PAGES_PER_STEP * chunks_per_page, bs,
                           past_len // CMP_STRIDE, *cmp_w, page_table=page_table)
        attn2 = _attn_sample(q2.reshape(bs, ts, ATTN_DIM), ng.reshape(bs, ts, LANES), kc, vc,
                             kvs.reshape(bs, ts, 2 * KV_DIM), kvw.reshape(bs, ts, 2 * KV_DIM),
                             _transposed_rows(state_kv_win[l]), _transposed_rows(cache_kv_slc[l]),
                             page_table, tabs_s, past_len)
        u3 = u.reshape(bs, ts, CONV_DIM)
        hist = jnp.pad(state_conv[l], ((0, 0), (HALO - (CONV_K - 1), 0), (0, 0)))
        conv = _conv(u3, hist, *cargs, ts, False)
        ys = _finish(ys, attn2.reshape(bs * ts, ATTN_DIM), conv.reshape(bs * ts, CONV_DIM), mg, w, gqm, gpf, gqf, tm)
        outs[1].append(kv5(kvc, bs, ts))
        outs[3].append(kv5(kvs, bs, ts))
        win_rows = jnp.concatenate([state_kv_win[l], kv5(kvw, bs, ts)], axis=1)
        outs[5].append(win_rows[:, win_rows.shape[1] - min(WINDOW, win_rows.shape[1]):])
        up = jnp.concatenate([state_conv[l], u3], axis=1)
        outs[7].append(up[:, up.shape[1] - (CONV_K - 1):])

    stack = lambda i: jnp.stack(outs[i])
    return (yp.reshape(bp, tp, d), ys.reshape(bs, ts, d), stack(0), stack(1), stack(2), stack(3),
            stack(4), stack(5), stack(6), stack(7))
```

```python
import functools
import math

import jax
import jax.numpy as jnp
import numpy as np
from jax import lax
from jax.experimental import pallas as pl
from jax.experimental.pallas import tpu as pltpu

D_MODEL = 2048
N_HEADS = 16
HEAD_DIM = 64
N_KV = 4
HPG = N_HEADS // N_KV
ATTN_DIM = N_HEADS * HEAD_DIM
KV_DIM = N_KV * HEAD_DIM
CMP_LEN = 32
CMP_STRIDE = 16
CMP_R = CMP_LEN // CMP_STRIDE
PHI_HIDDEN = HEAD_DIM
SEL_BLOCK = 64
N_SELECT = 16
WINDOW = 512
Q_BLOCK = 64
CONV_DIM = D_MODEL // 2
CONV_K = 31
D_FF = 4 * D_MODEL
REL_BUCKETS = 32
REL_MAX_DIST = 128
EPS = 1e-6
NEG = -1e30
PAGE_SIZE = 128

LANES = 128
VMEM_LIMIT_BYTES = 56 * 1024 * 1024

NEAR_KEYS = 384
NEAR_VARIANTS = 6
FAR_TILE = 512
WIN_FAR_KEYS = 384
MASK_BIG = 1e30
ROW_CHUNK = 32
HALO = 32

_F32 = jnp.float32
_BF16 = jnp.bfloat16


def _params(*sem):
    return pltpu.CompilerParams(dimension_semantics=sem, vmem_limit_bytes=VMEM_LIMIT_BYTES)


def _dot(a, b):
    return jnp.dot(a, b, preferred_element_type=_F32)


def _dot_nt(a, b):
    return lax.dot_general(a, b, (((1,), (1,)), ((), ())), preferred_element_type=_F32)


def _split3(x):
    hi = x.astype(_BF16)
    r1 = x - hi.astype(_F32)
    mid = r1.astype(_BF16)
    lo = (r1 - mid.astype(_F32)).astype(_BF16)
    return hi, mid, lo


def _rms(x, g):
    return x * lax.rsqrt(jnp.mean(x * x, axis=-1, keepdims=True) + EPS) * g


def _rel_bucket_np(dist):
    n = np.maximum(dist, 0)
    exact = REL_BUCKETS // 2
    logb = exact + (np.log(np.maximum(n, 1).astype(np.float32) / np.float32(exact))
                    / np.float32(math.log(REL_MAX_DIST / exact)) * (REL_BUCKETS - exact)).astype(np.int32)
    return np.where(n < exact, n, np.minimum(logb, REL_BUCKETS - 1)).astype(np.int32)


PROJ_TILE = 512
PROJ_SEGMENTS = (("q", 0, 2), ("kvc", 2, 1), ("kvs", 3, 1), ("kvw", 4, 1), ("glu", 5, 4), ("mg", 9, 8))
PROJ_TILES = 17


def _proj_kernel(x_ref, g_ref, w_ref, wng_ref, *refs, transposed_v):
    h_ref = refs[-1]
    if transposed_v:
        q_ref, kvc_ref, kvs_ref, ks16_ref, vst_ref, kvw_ref, kw16_ref, vwt_ref, ng_ref, u_ref, mg_ref = refs[:-1]
    else:
        q_ref, kvc_ref, kvs_ref, kvw_ref, ng_ref, u_ref, mg_ref = refs[:-1]
    j = pl.program_id(1)
    seg = {name: (lo, lo + n) for name, lo, n in PROJ_SEGMENTS}
    inside = lambda name: (j >= seg[name][0]) & (j < seg[name][1])

    @pl.when(j == 0)
    def _():
        h_ref[...] = _rms(x_ref[...], g_ref[...]).astype(_BF16)
        ng_ref[...] = _dot(h_ref[...], wng_ref[...])

    acc = _dot(h_ref[...], w_ref[...])

    @pl.when(inside("q"))
    def _():
        q_ref[...] = (acc * (HEAD_DIM ** -0.5)).astype(_BF16)

    @pl.when(inside("kvc"))
    def _():
        kvc_ref[...] = acc

    def kv_out(f32_ref, k16_ref, vt_ref):
        f32_ref[...] = acc
        if transposed_v:
            k16_ref[...] = acc[:, :KV_DIM].astype(_BF16)
            vt_ref[0] = acc[:, KV_DIM:].T.astype(_BF16)

    @pl.when(inside("kvs"))
    def _():
        kv_out(kvs_ref, ks16_ref if transposed_v else None, vst_ref if transposed_v else None)

    @pl.when(inside("kvw"))
    def _():
        kv_out(kvw_ref, kw16_ref if transposed_v else None, vwt_ref if transposed_v else None)

    @pl.when(inside("glu"))
    def _():
        half = PROJ_TILE // 2
        u_ref[...] = acc[:, :half] * jax.nn.sigmoid(acc[:, half:])

    @pl.when(inside("mg"))
    def _():
        mg_ref[...] = jax.nn.sigmoid(acc).astype(_BF16)


def _proj(x, g, w_all, wng, tm, seq_len=None):
    m, d = x.shape
    tn = PROJ_TILE
    assert w_all.shape == (d, PROJ_TILES * tn)
    transposed_v = seq_len is not None
    seg = {name: (lo, n) for name, lo, n in PROJ_SEGMENTS}

    def spec(name, width=tn):
        lo, n = seg[name]
        return pl.BlockSpec((tm, width), lambda i, j: (i, jnp.clip(j - lo, 0, n - 1)))

    f32 = lambda n: jax.ShapeDtypeStruct((m, n), _F32)
    b16 = lambda n: jax.ShapeDtypeStruct((m, n), _BF16)
    kv_specs, kv_shapes = [spec("kvs")], [f32(tn)]
    kw_specs, kw_shapes = [spec("kvw")], [f32(tn)]
    if transposed_v:
        assert seq_len % tm == 0
        spb = seq_len // tm
        extra_specs = [pl.BlockSpec((tm, KV_DIM), lambda i, j: (i, 0)),
                       pl.BlockSpec((1, KV_DIM, tm), lambda i, j: (i // spb, 0, i % spb))]
        extra_shapes = [b16(KV_DIM), jax.ShapeDtypeStruct((m // seq_len, KV_DIM, seq_len), _BF16)]
        kv_specs, kv_shapes = kv_specs + extra_specs, kv_shapes + extra_shapes
        kw_specs, kw_shapes = kw_specs + extra_specs, kw_shapes + extra_shapes
    out_specs = ([spec("q"), spec("kvc")] + kv_specs + kw_specs
                 + [pl.BlockSpec((tm, LANES), lambda i, j: (i, 0)), spec("glu", tn // 2), spec("mg")])
    out_shape = ([b16(seg["q"][1] * tn), f32(tn)] + kv_shapes + kw_shapes
                 + [f32(LANES), f32(seg["glu"][1] * tn // 2), b16(seg["mg"][1] * tn)])
    outs = pl.pallas_call(
        functools.partial(_proj_kernel, transposed_v=transposed_v),
        grid=(m // tm, PROJ_TILES),
        in_specs=[pl.BlockSpec((tm, d), lambda i, j: (i, 0)),
                  pl.BlockSpec((1, d), lambda i, j: (0, 0)),
                  pl.BlockSpec((d, tn), lambda i, j: (0, j)),
                  pl.BlockSpec((d, LANES), lambda i, j: (0, 0))],
        out_specs=out_specs,
        out_shape=out_shape,
        scratch_shapes=[pltpu.VMEM((tm, d), _BF16)],
        compiler_params=_params("parallel", "arbitrary"),
        name="proj",
    )(x, g, w_all, wng)
    nkv = len(kv_specs)
    q2, kvc = outs[0], outs[1]
    kvs, kvw = tuple(outs[2:2 + nkv]), tuple(outs[2 + nkv:2 + 2 * nkv])
    ng, u, mg = outs[2 + 2 * nkv:]
    return q2, kvc, kvs, kvw, ng, u, mg


def _mix_kernel(a_ref, c_ref, ga_ref, gc_ref, wa_ref, wc_ref, o_ref):
    ya = _dot(a_ref[...], wa_ref[...])
    yc = _dot(c_ref[...], wc_ref[...])
    o_ref[...] = (ga_ref[...].astype(_F32) * ya + gc_ref[...].astype(_F32) * yc).astype(_BF16)


def _mix(attn, conv, mg, wao, wco, tm, tn):
    m, ka = attn.shape
    n = wao.shape[1]
    nb = n // tn
    return pl.pallas_call(
        _mix_kernel,
        grid=(m // tm, nb),
        in_specs=[pl.BlockSpec((tm, ka), lambda i, j: (i, 0)),
                  pl.BlockSpec((tm, conv.shape[1]), lambda i, j: (i, 0)),
                  pl.BlockSpec((tm, tn), lambda i, j: (i, j)),
                  pl.BlockSpec((tm, tn), lambda i, j: (i, j + nb)),
                  pl.BlockSpec((ka, tn), lambda i, j: (0, j)),
                  pl.BlockSpec((conv.shape[1], tn), lambda i, j: (0, j))],
        out_specs=pl.BlockSpec((tm, tn), lambda i, j: (i, j)),
        out_shape=jax.ShapeDtypeStruct((m, n), _BF16),
        compiler_params=_params("parallel", "arbitrary"),
        name="mix",
    )(attn, conv, mg, mg, wao, wco)


def _oproj_kernel(mx_ref, x_ref, w_ref, g_ref, o_ref):
    y = _dot(mx_ref[...], w_ref[...])
    o_ref[...] = x_ref[...] + _rms(y, g_ref[...])


def _oproj(mixed, x, wo, g, tm):
    m, d = x.shape
    return pl.pallas_call(
        _oproj_kernel,
        grid=(m // tm,),
        in_specs=[pl.BlockSpec((tm, d), lambda i: (i, 0)),
                  pl.BlockSpec((tm, d), lambda i: (i, 0)),
                  pl.BlockSpec((d, d), lambda i: (0, 0)),
                  pl.BlockSpec((1, d), lambda i: (0, 0))],
        out_specs=pl.BlockSpec((tm, d), lambda i: (i, 0)),
        out_shape=jax.ShapeDtypeStruct((m, d), _F32),
        compiler_params=_params("parallel"),
        name="oproj",
    )(mixed, x, wo, g)


def _ffn_kernel(x_ref, gpre_ref, gpost_ref, wu_ref, wd_ref, o_ref, h_ref, acc_ref):
    j = pl.program_id(1)

    @pl.when(j == 0)
    def _():
        h_ref[...] = _rms(x_ref[...], gpre_ref[...]).astype(_BF16)
        acc_ref[...] = jnp.zeros_like(acc_ref)

    a = jnp.maximum(_dot(h_ref[...], wu_ref[...]), 0.0)
    acc_ref[...] += _dot((a * a).astype(_BF16), wd_ref[...])

    @pl.when(j == pl.num_programs(1) - 1)
    def _():
        o_ref[...] = x_ref[...] + _rms(acc_ref[...], gpost_ref[...])


def _ffn(x, gpre, gpost, wu, wd, tm, tf):
    m, d = x.shape
    f = wu.shape[1]
    return pl.pallas_call(
        _ffn_kernel,
        grid=(m // tm, f // tf),
        in_specs=[pl.BlockSpec((tm, d), lambda i, j: (i, 0)),
                  pl.BlockSpec((1, d), lambda i, j: (0, 0)),
                  pl.BlockSpec((1, d), lambda i, j: (0, 0)),
                  pl.BlockSpec((d, tf), lambda i, j: (0, j)),
                  pl.BlockSpec((tf, d), lambda i, j: (j, 0))],
        out_specs=pl.BlockSpec((tm, d), lambda i, j: (i, 0)),
        out_shape=jax.ShapeDtypeStruct((m, d), _F32),
        scratch_shapes=[pltpu.VMEM((tm, d), _BF16), pltpu.VMEM((tm, d), _F32)],
        compiler_params=_params("parallel", "arbitrary"),
        name="ffn",
    )(x, gpre, gpost, wu, wd)


def _conv_kernel(u_ref, halo_ref, w_ref, b_ref, lg_ref, lb_ref, o_ref, win_ref, *, tt, zero_first):
    c = u_ref.shape[-1]
    halo = halo_ref[0]
    if zero_first:
        halo = jnp.where(pl.program_id(1) == 0, 0.0, halo)
    win_ref[0:HALO, :] = halo
    win_ref[HALO:HALO + tt, :] = u_ref[0]
    rc = min(ROW_CHUNK, tt)
    off = HALO - (CONV_K - 1)
    for ch in range(tt // rc):
        acc = jnp.zeros((rc, c), _F32) + b_ref[...]
        for k in range(CONV_K):
            acc = acc + w_ref[k:k + 1, :] * win_ref[ch * rc + off + k:ch * rc + off + k + rc, :]
        mu = jnp.mean(acc, axis=-1, keepdims=True)
        xc = acc - mu
        var = jnp.mean(xc * xc, axis=-1, keepdims=True)
        y = xc * lax.rsqrt(var + EPS) * lg_ref[...] + lb_ref[...]
        o_ref[0, ch * rc:(ch + 1) * rc, :] = (y * jax.nn.sigmoid(y)).astype(_BF16)


def _conv(u, halo_src, w, b, lg, lb, tt, zero_first):
    bsz, t, c = u.shape
    nhb = tt // HALO
    if zero_first:
        halo_map = lambda bi, ti: (bi, jnp.maximum(ti * nhb - 1, 0), 0)
    else:
        halo_map = lambda bi, ti: (bi, 0, 0)
    kern = functools.partial(_conv_kernel, tt=tt, zero_first=zero_first)
    return pl.pallas_call(
        kern,
        grid=(bsz, t // tt),
        in_specs=[pl.BlockSpec((1, tt, c), lambda bi, ti: (bi, ti, 0)),
                  pl.BlockSpec((1, HALO, c), halo_map),
                  pl.BlockSpec((HALO, c), lambda bi, ti: (0, 0)),
                  pl.BlockSpec((1, c), lambda bi, ti: (0, 0)),
                  pl.BlockSpec((1, c), lambda bi, ti: (0, 0)),
                  pl.BlockSpec((1, c), lambda bi, ti: (0, 0))],
        out_specs=pl.BlockSpec((1, tt, c), lambda bi, ti: (bi, ti, 0)),
        out_shape=jax.ShapeDtypeStruct((bsz, t, c), _BF16),
        scratch_shapes=[pltpu.VMEM((HALO + tt, c), _F32)],
        compiler_params=_params("parallel", "arbitrary"),
        name="conv",
    )(u, halo_src, w, b, lg, lb)


def _compress_kernel(*refs, n_src, nrow, paged):
    if paged:
        refs = refs[1:]
        rows_ref = refs[-1]
        refs = refs[:-1]
    src_refs = refs[:n_src]
    w1_ref, w2_ref, pe_ref, w1t_ref, kc_ref, vc_ref, vct_ref, carry_ref = refs[n_src:]
    k = pl.program_id(1)

    @pl.when(k == 0)
    def _():
        carry_ref[...] = jnp.zeros_like(carry_ref)

    if paged:
        for p, r in enumerate(src_refs):
            for lc in range(2 * KV_DIM // LANES):
                rows_ref[lc, p * PAGE_SIZE:(p + 1) * PAGE_SIZE, :] = r[0, lc * LANES:(lc + 1) * LANES, :].T
    outs = []
    for c in range(2):
        acc = jnp.zeros((nrow, 2 * KV_DIM), _F32)
        for s in range(CMP_STRIDE):
            if paged:
                lcs = range(c * KV_DIM // LANES, (c + 1) * KV_DIM // LANES)
                xs = jnp.concatenate([rows_ref[lc, pl.ds(s, nrow, stride=CMP_STRIDE), :] for lc in lcs], axis=-1)
            else:
                lo = s * 2 * KV_DIM + c * KV_DIM
                xs = src_refs[0][0, :, lo:lo + KV_DIM]
            acc = acc + _dot(xs.astype(_BF16), w1_ref[c, s])
        pt = jnp.sum(pe_ref[c] * w1t_ref[c], axis=0, keepdims=True)
        pt = jnp.concatenate([pt] * (KV_DIM // LANES), axis=-1)
        a0 = acc[:, :KV_DIM]
        a1 = acc[:, KV_DIM:]
        first = lax.broadcasted_iota(jnp.int32, (nrow, KV_DIM), 0) == 0
        a0s = jnp.where(first, carry_ref[c], pltpu.roll(a0, 1, 0))
        carry_ref[c] = a0[nrow - 1:nrow, :]
        hid = jax.nn.gelu(a0s + a1 + pt)
        outs.append(_dot(hid.astype(_BF16), w2_ref[c]))
    kc_ref[0] = outs[0].astype(_BF16)
    vc_ref[0] = outs[1].astype(_BF16)
    vct_ref[0] = outs[1].T.astype(_BF16)


def _compress(src, src_specs, grid, nrow, n_batch, n_out_rows, w1bd, w2bd, pe_b, w1t, page_table=None):
    paged = page_table is not None
    n_src = len(src)
    kern = functools.partial(_compress_kernel, n_src=n_src, nrow=nrow, paged=paged)
    const = lambda a: pl.BlockSpec(a.shape, lambda *_, _n=a.ndim: (0,) * _n, pipeline_mode=pl.Buffered(1))
    omap = lambda b, k, *_: (b, k, 0)
    scratch = [pltpu.VMEM((2, 1, KV_DIM), _F32)]
    if paged:
        scratch.append(pltpu.VMEM((2 * KV_DIM // LANES, n_src * PAGE_SIZE, LANES), _F32))
    grid_spec = pltpu.PrefetchScalarGridSpec(
        num_scalar_prefetch=1 if paged else 0,
        grid=grid,
        in_specs=list(src_specs) + [const(w1bd), const(w2bd), const(pe_b), const(w1t)],
        out_specs=[pl.BlockSpec((1, nrow, KV_DIM), omap), pl.BlockSpec((1, nrow, KV_DIM), omap),
                   pl.BlockSpec((1, KV_DIM, nrow), lambda b, k, *_: (b, 0, k))],
        scratch_shapes=scratch,
    )
    return pl.pallas_call(
        kern,
        grid_spec=grid_spec,
        out_shape=[jax.ShapeDtypeStruct((n_batch, n_out_rows, KV_DIM), _BF16)] * 2
                  + [jax.ShapeDtypeStruct((n_batch, KV_DIM, n_out_rows), _BF16)],
        compiler_params=_params("parallel", "arbitrary"),
        name="compress_paged" if paged else "compress",
    )(*((page_table,) if paged else ()), *src, w1bd, w2bd, pe_b, w1t)


def _rank_select(sc_ref, n_rows, n_iter, n_sel):
    s = sc_ref[...]
    jio = lax.broadcasted_iota(jnp.int32, s.shape, 0)

    def body(jp, cnt):
        row = jnp.broadcast_to(sc_ref[pl.ds(jp, 1), :], s.shape)
        ge = jnp.where(row >= s, 1.0, 0.0)
        gt = jnp.where(row > s, 1.0, 0.0)
        return cnt + jnp.where(jio > jp, ge, gt)

    cnt = lax.fori_loop(0, n_iter, body, jnp.zeros(s.shape, _F32))
    return jnp.where(cnt < n_sel, 1.0, 0.0)


def _softmax_parts(parts):
    ms = [jnp.max(jnp.where(mk, s, NEG), axis=-1, keepdims=True) for s, mk in parts]
    m = functools.reduce(jnp.maximum, ms)
    ps = [jnp.where(mk, jnp.exp(s - m), 0.0) for s, mk in parts]
    l = functools.reduce(lambda a, b: a + b, [jnp.sum(p, axis=-1, keepdims=True) for p in ps])
    inv = 1.0 / jnp.maximum(l, 1e-30)
    return [p * inv for p in ps]


def _softmax_cols(parts):
    ms = [jnp.max(jnp.where(mk, s, NEG), axis=0, keepdims=True) for s, mk in parts]
    m = functools.reduce(jnp.maximum, ms)
    ps = [jnp.where(mk, jnp.exp(s - m), 0.0) for s, mk in parts]
    l = functools.reduce(lambda a, b: a + b, [jnp.sum(p, axis=0, keepdims=True) for p in ps])
    inv = 1.0 / jnp.maximum(l, 1e-30)
    return [p * inv for p in ps]


def _attn_prompt_kernel(q_ref, ng_ref, ks_ref, vst_ref, kw_ref, vwt_ref, kc_ref, vct_ref,
                        tzt_ref, tct_ref, ot_ref, gsel_ref, rsum_ref, rep_ref,
                        o_ref, qzt_ref, otacc_ref, gt_ref, sc_ref, nsel_ref, *, n_cmp_pad, seq_len):
    i = pl.program_id(1)
    qb = Q_BLOCK
    rows = HPG * qb
    lane_g = lax.broadcasted_iota(jnp.int32, (qb, KV_DIM), 1) // HEAD_DIM
    vrows = lambda g: pl.ds(g * HEAD_DIM, HEAD_DIM)

    for g in range(N_KV):
        qz = jnp.concatenate([jnp.where(lane_g == g, q_ref[0, :, r * KV_DIM:(r + 1) * KV_DIM].astype(_F32), 0.0)
                              for r in range(HPG)], axis=0)
        qzt_ref[g] = qz.T.astype(_BF16)

    gparts = _split3(jax.nn.sigmoid(ng_ref[0]))
    gcols = jnp.concatenate([sum(_dot(p, gsel_ref[r]) for p in gparts) for r in range(HPG)], axis=0)
    gt_ref[...] = gcols.T
    otacc_ref[...] = jnp.zeros_like(otacc_ref)

    def emit(branch, g, out_t):
        otacc_ref[g] += gt_ref[pl.ds(branch * N_KV + g, 1), :] * out_t

    jc = lax.broadcasted_iota(jnp.int32, (n_cmp_pad, rows), 0)
    lq = lax.broadcasted_iota(jnp.int32, (n_cmp_pad, rows), 1) % qb
    cmp_valid = (jc >= 1) & (CMP_STRIDE * jc + (CMP_LEN - CMP_STRIDE - 1) - lq <= i * qb)
    mm = lax.broadcasted_iota(jnp.int32, (n_cmp_pad, LANES), 1)
    jj = lax.broadcasted_iota(jnp.int32, (n_cmp_pad, LANES), 0)
    shift_t = jnp.where((mm < 16) & (jj - mm == 4 * i - 8), 1.0, 0.0).astype(_BF16)
    kc_aug = jnp.concatenate([kc_ref[0], shift_t, shift_t, shift_t], axis=1)
    groups = range(N_KV)
    s_c = [_dot(kc_aug, jnp.concatenate([qzt_ref[g], tct_ref[0, g], tct_ref[1, g], tct_ref[2, g]], axis=0))
           for g in groups]
    pn_c = [_softmax_cols([(s, cmp_valid)])[0] for s in s_c]
    for g in groups:
        emit(0, g, _dot(vct_ref[0, vrows(g), :], pn_c[g].astype(_BF16)))
    y_c = [sum(_dot(ot_ref[...], p) for p in _split3(pn_c[g])) for g in groups]
    imp_t = sum(sum(_dot(p, rsum_ref[g]) for p in _split3(y_c[g])) for g in groups)

    jrow = lax.broadcasted_iota(jnp.int32, imp_t.shape, 0)
    forced = (jrow == 0) | (jrow == i) | (jrow == i - 1)
    sc_ref[...] = jnp.where(forced, jnp.inf, jnp.where(jrow <= i, imp_t, -jnp.inf))
    sel_t = _rank_select(sc_ref, imp_t.shape[0], i + 1, N_SELECT)

    h = jnp.minimum(jnp.maximum(i - 3, 0) // 2, (seq_len - NEAR_KEYS) // LANES)
    ns = pl.multiple_of(h * LANES, LANES)
    v = i - 2 * h
    sel16 = sel_t.astype(_BF16)
    for g in groups:
        neg = (_dot(sel16, rep_ref[g]) - 1.0) * MASK_BIG
        nsel_ref[0, g] = neg
        nsel_ref[1, g] = jnp.where(jrow < 2 * h, neg, -MASK_BIG)

    def add_block_mask(s, far, g, j0, n_blocks):
        return jnp.concatenate([s[jj * SEL_BLOCK:(jj + 1) * SEL_BLOCK, :] + nsel_ref[far, g, pl.ds(j0 + jj, 1), :]
                                for jj in range(n_blocks)], axis=0)

    cn = lax.broadcasted_iota(jnp.int32, (NEAR_KEYS, rows), 0)
    lqn = lax.broadcasted_iota(jnp.int32, (NEAR_KEYS, rows), 1) % qb
    causal = cn - lqn <= v * SEL_BLOCK
    near = pl.ds(ns, NEAR_KEYS)
    n_far = (ns + FAR_TILE - 1) // FAR_TILE

    k_near = ks_ref[0, near, :]
    s_n = [add_block_mask(_dot(k_near, qzt_ref[g]) + tzt_ref[v, g], 0, g, 2 * h, NEAR_KEYS // SEL_BLOCK)
           for g in groups]
    s_n = [jnp.where(causal, s, NEG) for s in s_n]
    m0 = [jnp.max(s, axis=0, keepdims=True) for s in s_n]
    p_n = [jnp.exp(s_n[g] - m0[g]) for g in groups]
    l0 = [jnp.sum(p, axis=0, keepdims=True) for p in p_n]
    acc0 = [_dot(vst_ref[0, vrows(g), near], p_n[g].astype(_BF16)) for g in groups]

    def far_body(tau, carry):
        m_old, l_old, acc_old = carry
        k0 = pl.multiple_of(tau * FAR_TILE, FAR_TILE)
        tile = pl.ds(k0, FAR_TILE)
        kt = ks_ref[0, tile, :]
        s_f = [add_block_mask(_dot(kt, qzt_ref[g]), 1, g, tau * (FAR_TILE // SEL_BLOCK), FAR_TILE // SEL_BLOCK)
               for g in groups]
        m_new = [jnp.maximum(m_old[g], jnp.max(s_f[g], axis=0, keepdims=True)) for g in groups]
        alpha = [jnp.exp(m_old[g] - m_new[g]) for g in groups]
        p_f = [jnp.exp(s_f[g] - m_new[g]) for g in groups]
        l_new = [alpha[g] * l_old[g] + jnp.sum(p_f[g], axis=0, keepdims=True) for g in groups]
        acc = [alpha[g] * acc_old[g] + _dot(vst_ref[0, vrows(g), tile], p_f[g].astype(_BF16)) for g in groups]
        return tuple(m_new), tuple(l_new), tuple(acc)

    _, l1, acc1 = lax.fori_loop(0, n_far, far_body, (tuple(m0), tuple(l0), tuple(acc0)))
    for g in groups:
        emit(1, g, acc1[g] * (1.0 / l1[g]))

    fs = pl.multiple_of((jnp.maximum(i - WINDOW // SEL_BLOCK, 0) // 2) * LANES, LANES)
    wfar = pl.ds(fs, WIN_FAR_KEYS)
    cf = lax.broadcasted_iota(jnp.int32, (WIN_FAR_KEYS, rows), 0)
    lqf = lax.broadcasted_iota(jnp.int32, (WIN_FAR_KEYS, rows), 1) % qb
    far_ok = (i * qb + lqf - fs - cf < WINDOW) & (fs + cf < ns)
    kw_near, kw_far = kw_ref[0, near, :], kw_ref[0, wfar, :]
    s_wn = [_dot(kw_near, qzt_ref[g]) + tzt_ref[v, g] for g in groups]
    s_wf = [_dot(kw_far, qzt_ref[g]) for g in groups]
    p_w = [_softmax_cols([(s_wn[g], causal), (s_wf[g], far_ok)]) for g in groups]
    for g in groups:
        emit(2, g, _dot(vwt_ref[0, vrows(g), near], p_w[g][0].astype(_BF16))
             + _dot(vwt_ref[0, vrows(g), wfar], p_w[g][1].astype(_BF16)))

    res = otacc_ref[...].reshape(N_KV * HEAD_DIM, rows).T
    for r in range(HPG):
        o_ref[0, :, r * KV_DIM:(r + 1) * KV_DIM] = res[r * qb:(r + 1) * qb, :].astype(_BF16)


def _attn_prompt(q2, ng, ks, vst, kw, vwt, kc, vct, tabs):
    bsz, t, _ = q2.shape
    nblk = t // SEL_BLOCK
    n_cmp_pad = kc.shape[1]
    rows = HPG * Q_BLOCK
    assert t % FAR_TILE == 0 and t >= NEAR_KEYS
    names = ["tzt", "tct", "ot", "gsel", "rsum", "rep"]
    full = lambda a: pl.BlockSpec(a.shape, lambda b, i, _n=a.ndim: (0,) * _n, pipeline_mode=pl.Buffered(1))
    per_b = lambda a: pl.BlockSpec((1,) + a.shape[1:], lambda b, i: (b, 0, 0))
    kern = functools.partial(_attn_prompt_kernel, n_cmp_pad=n_cmp_pad, seq_len=t)
    return pl.pallas_call(
        kern,
        grid=(bsz, nblk),
        in_specs=[pl.BlockSpec((1, Q_BLOCK, ATTN_DIM), lambda b, i: (b, i, 0)),
                  pl.BlockSpec((1, Q_BLOCK, LANES), lambda b, i: (b, i, 0)),
                  per_b(ks), per_b(vst), per_b(kw), per_b(vwt), per_b(kc), per_b(vct)]
                 + [full(tabs[n]) for n in names],
        out_specs=pl.BlockSpec((1, Q_BLOCK, ATTN_DIM), lambda b, i: (b, i, 0)),
        out_shape=jax.ShapeDtypeStruct((bsz, t, ATTN_DIM), _BF16),
        scratch_shapes=[pltpu.VMEM((N_KV, KV_DIM, rows), _BF16),
                        pltpu.VMEM((N_KV, HEAD_DIM, rows), _F32),
                        pltpu.VMEM((LANES, rows), _F32),
                        pltpu.VMEM((SEL_BLOCK, rows), _F32),
                        pltpu.VMEM((2, N_KV, SEL_BLOCK, rows), _F32)],
        compiler_params=_params("parallel", "arbitrary"),
        name="attn_prompt",
    )(q2, ng, ks, vst, kw, vwt, kc, vct, *[tabs[n] for n in names])


def _bias_table(rel, dist, head):
    nmax = max(int(dist.max()), 1) + 1
    bk = _rel_bucket_np(np.arange(nmax))
    rel_h = rel[:, head]
    out = jnp.broadcast_to(rel_h[0], np.broadcast_shapes(dist.shape, head.shape))
    dist = lax.optimization_barrier(jnp.asarray(dist, jnp.int32))
    for b in range(1, int(bk.max()) + 1):
        first = int(np.argmax(bk >= b))
        out = jnp.where(dist >= first, rel_h[b], out)
    return out


def _prompt_tables(rel_bias, t):
    nblk = t // SEL_BLOCK
    n_cmp_pad = t // CMP_STRIDE
    assert nblk <= SEL_BLOCK and n_cmp_pad % LANES == 0
    rows = HPG * Q_BLOCK
    rel = rel_bias.astype(_F32)
    r_idx = np.arange(rows) // Q_BLOCK
    q_idx = np.arange(rows) % Q_BLOCK
    head = np.arange(N_KV)[:, None] * HPG + r_idx[None, :]
    c31 = rel[REL_BUCKETS - 1][head]
    c = np.arange(NEAR_KEYS)
    dist = np.arange(NEAR_VARIANTS)[:, None, None] * SEL_BLOCK + q_idx[None, None, :] - c[None, :, None]
    tzt = _bias_table(rel, dist[:, None], head[None, :, None, :]) - c31[None, :, None, :]
    mmv = np.arange(16)
    dist_c = q_idx[None, :] - CMP_STRIDE * (mmv[:, None] - 8) - (CMP_LEN - CMP_STRIDE - 1)
    delta = _bias_table(rel, dist_c[None], head[:, None, :]) - c31[:, None, :]
    tct = jnp.stack(_split3(jnp.pad(delta, ((0, 0), (0, LANES - 16), (0, 0)))))
    n = np.arange(n_cmp_pad) - 1
    cs = n * CMP_STRIDE
    bs = np.arange(SEL_BLOCK) * SEL_BLOCK
    ov = np.clip(np.minimum(cs[None, :] + CMP_LEN, bs[:, None] + SEL_BLOCK) - np.maximum(cs[None, :], bs[:, None]),
                 0, CMP_LEN).astype(np.float32) / CMP_LEN
    ov[:, 0] = 0.0
    ov[nblk:, :] = 0.0
    gsel = np.zeros((HPG, LANES, LANES), np.float32)
    for j in range(3):
        for r in range(HPG):
            for g in range(N_KV):
                gsel[r, j * N_HEADS + r * N_KV + g, j * N_KV + g] = 1.0
    rsum = np.zeros((N_KV, rows, N_KV * Q_BLOCK), np.float32)
    for g in range(N_KV):
        rsum[g, np.arange(rows), g * Q_BLOCK + q_idx] = 1.0
    bf = lambda a: jnp.asarray(a, _BF16)
    return dict(tzt=tzt, tct=tct, ot=bf(ov), gsel=bf(gsel), rsum=bf(rsum),
                rep=bf(rsum.transpose(0, 2, 1)))


def _gate_expand():
    ex = np.zeros((LANES, 3 * ATTN_DIM), np.float32)
    for j in range(3):
        for r in range(HPG):
            for g in range(N_KV):
                col = j * ATTN_DIM + r * KV_DIM + g * HEAD_DIM
                ex[j * N_HEADS + r * N_KV + g, col:col + HEAD_DIM] = 1.0
    return jnp.asarray(ex, _BF16)


PAGES_PER_STEP = 32
SUB_PAGES = 4


def _attn_sample_kernel(pt_ref, q_ref, ng_ref, kc_ref, vc_ref, knew_ref, wnew_ref, wst_ref, *refs,
                        past_len, n_blk, nb_rows):
    del pt_ref
    page_refs = refs[:PAGES_PER_STEP]
    (bc_ref, blast_ref, c31_ref, bnew_ref, bwin_ref, ot_ref, e64_ref, ex_ref,
     o_ref, qall_ref, gate_ref, oacc_ref, acc_ref, m_ref, l_ref, selt_ref, sc_ref) = refs[PAGES_PER_STEP:]
    k = pl.program_id(1)
    nk = pl.num_programs(1)
    tq = q_ref.shape[1]
    rows = N_KV * HPG * tq
    lane_g = lax.broadcasted_iota(jnp.int32, (tq, KV_DIM), 1) // HEAD_DIM
    rq = lax.broadcasted_iota(jnp.int32, (rows, 1), 0) % tq
    sub_keys = SUB_PAGES * PAGE_SIZE

    def emit(branch, out):
        for g in range(N_KV):
            for r in range(HPG):
                col = r * KV_DIM
                row0 = (g * HPG + r) * tq
                gt = gate_ref[:, branch * ATTN_DIM + col:branch * ATTN_DIM + col + KV_DIM]
                oacc_ref[:, col:col + KV_DIM] += jnp.where(lane_g == g, gt * out[row0:row0 + tq, :], 0.0)

    def pad_rows(x, n):
        return jnp.concatenate([x, jnp.zeros((n - x.shape[0], x.shape[1]), x.dtype)], axis=0)

    @pl.when(k == 0)
    def _():
        qf = q_ref[0].astype(_F32)
        pieces = []
        for g in range(N_KV):
            for r in range(HPG):
                pieces.append(jnp.where(lane_g == g, qf[:, r * KV_DIM:(r + 1) * KV_DIM], 0.0))
        qall = jnp.concatenate(pieces, axis=0).astype(_BF16)
        qall_ref[...] = qall
        gs = jax.nn.sigmoid(ng_ref[0])
        gate_ref[...] = sum(_dot(p, ex_ref[...]) for p in _split3(gs))
        oacc_ref[...] = jnp.zeros_like(oacc_ref)

        n_cmp_pad = kc_ref.shape[1]
        jc = lax.broadcasted_iota(jnp.int32, (rows, n_cmp_pad), 1)
        cmp_valid = (jc >= 1) & (CMP_STRIDE * jc + (CMP_LEN - CMP_STRIDE - 1) - rq <= past_len)
        s = _dot_nt(qall, kc_ref[0]) + bc_ref[...]
        (pn,) = _softmax_parts([(s, cmp_valid)])
        emit(0, _dot(pn.astype(_BF16), vc_ref[0]))
        imp_rows = []
        for g in range(N_KV):
            sg = sum(pn[(g * HPG + r) * tq:(g * HPG + r + 1) * tq, :] for r in range(HPG))
            imp_rows += [sg] * HPG
        imp = jnp.concatenate(imp_rows, axis=0)
        imp_t = sum(_dot_nt(ot_ref[...], p) for p in _split3(imp))

        jrow = lax.broadcasted_iota(jnp.int32, imp_t.shape, 0)
        tpos = past_len + lax.broadcasted_iota(jnp.int32, imp_t.shape, 1) % tq
        cur = tpos // SEL_BLOCK
        forced = (jrow == 0) | (jrow == cur) | (jrow == cur - 1)
        valid = jrow * SEL_BLOCK <= tpos
        sc_ref[...] = jnp.where(forced, jnp.inf, jnp.where(valid, imp_t, -jnp.inf))
        sel_t = _rank_select(sc_ref, nb_rows, n_blk, min(N_SELECT, n_blk))
        selt_ref[...] = pad_rows(sel_t, selt_ref.shape[0])

        kn = pad_rows(knew_ref[0, :, :KV_DIM], LANES).astype(_BF16)
        vn = pad_rows(knew_ref[0, :, KV_DIM:], LANES).astype(_BF16)
        cn = lax.broadcasted_iota(jnp.int32, (rows, LANES), 1)
        mk = (cn <= rq) & (cn < tq)
        s = _dot_nt(qall, kn) + bnew_ref[...]
        m = jnp.max(jnp.where(mk, s, NEG), axis=-1, keepdims=True)
        p = jnp.where(mk, jnp.exp(s - m), 0.0)
        m_ref[...] = jnp.broadcast_to(m, m_ref.shape)
        l_ref[...] = jnp.broadcast_to(jnp.sum(p, axis=-1, keepdims=True), l_ref.shape)
        acc_ref[...] = _dot(p.astype(_BF16), vn)

    qall = qall_ref[...]
    blk_per_step = PAGES_PER_STEP * PAGE_SIZE // SEL_BLOCK
    j0 = pl.multiple_of(k * blk_per_step, blk_per_step)
    sel_step = selt_ref[pl.ds(j0, LANES), :].T.astype(_BF16)
    c31 = jnp.concatenate([c31_ref[...]] * (sub_keys // LANES), axis=-1)
    n_sub = PAGES_PER_STEP // SUB_PAGES
    for st in range(n_sub):
        pages = page_refs[st * SUB_PAGES:(st + 1) * SUB_PAGES]
        kt = jnp.concatenate([r[0, :KV_DIM, :] for r in pages], axis=1).astype(_BF16)
        vt = jnp.concatenate([r[0, KV_DIM:, :] for r in pages], axis=1).astype(_BF16)
        mk = _dot(sel_step, e64_ref[:, st * sub_keys:(st + 1) * sub_keys]) > 0.5
        if st == n_sub - 1:
            bias = jnp.where(k == nk - 1, blast_ref[...], c31)
        else:
            bias = c31
        s = jnp.where(mk, _dot(qall, kt) + bias, NEG)
        m_old = m_ref[:, :1]
        m_new = jnp.maximum(m_old, jnp.max(s, axis=-1, keepdims=True))
        alpha = jnp.exp(m_old - m_new)
        p = jnp.exp(s - m_new)
        l_ref[...] = jnp.broadcast_to(alpha * l_ref[:, :1] + jnp.sum(p, axis=-1, keepdims=True), l_ref.shape)
        m_ref[...] = jnp.broadcast_to(m_new, m_ref.shape)
        acc_ref[...] = alpha * acc_ref[...] + _dot_nt(p.astype(_BF16), vt)

    @pl.when(k == nk - 1)
    def _():
        emit(1, acc_ref[...] * (1.0 / l_ref[:, :1]))
        lw = wst_ref.shape[2]
        kw = wst_ref[0, :KV_DIM, :].astype(_BF16)
        vw = wst_ref[0, KV_DIM:, :].astype(_BF16)
        kn = pad_rows(wnew_ref[0, :, :KV_DIM], LANES).astype(_BF16)
        vn = pad_rows(wnew_ref[0, :, KV_DIM:], LANES).astype(_BF16)
        cw = lax.broadcasted_iota(jnp.int32, (rows, lw), 1)
        dw = lw + rq - cw
        cn = lax.broadcasted_iota(jnp.int32, (rows, LANES), 1)
        pw, pnw = _softmax_parts([(_dot(qall, kw) + bwin_ref[...], (dw >= 0) & (dw < WINDOW)),
                                  (_dot_nt(qall, kn) + bnew_ref[...], (cn <= rq) & (cn < tq))])
        emit(2, _dot_nt(pw.astype(_BF16), vw) + _dot(pnw.astype(_BF16), vn))
        o_ref[0] = oacc_ref[...].astype(_BF16)


def _page_specs():
    return [pl.BlockSpec((1, 2 * KV_DIM, PAGE_SIZE),
                         lambda b, k, pt, _p=p: (pt[b, k * PAGES_PER_STEP + _p], 0, 0))
            for p in range(PAGES_PER_STEP)]


def _transposed_rows(a):
    n, rows = a.shape[:2]
    return jnp.transpose(a, (0, 2, 3, 4, 1)).reshape(n, 2 * KV_DIM, rows)


def _attn_sample(q2, ng, kc, vc, kvs_new, kvw_new, win_state, slc_pages, page_table, tabs, past_len):
    bsz, tq, _ = q2.shape
    n_pages = page_table.shape[1]
    assert n_pages % PAGES_PER_STEP == 0 and past_len == n_pages * PAGE_SIZE and past_len % SEL_BLOCK == 0
    n_steps = n_pages // PAGES_PER_STEP
    rows = N_KV * HPG * tq
    assert rows == LANES
    n_blk = -(-(past_len + tq) // SEL_BLOCK)
    nb_rows = tabs["ot"].shape[0]
    blk_per_step = PAGES_PER_STEP * PAGE_SIZE // SEL_BLOCK
    selt_rows = (n_steps - 1) * blk_per_step + LANES
    assert selt_rows >= nb_rows
    full = lambda a: pl.BlockSpec(a.shape, lambda b, k, pt, _n=a.ndim: (0,) * _n)
    per_b = lambda a: pl.BlockSpec((1,) + a.shape[1:], lambda b, k, pt: (b, 0, 0))
    page_specs = _page_specs()
    names =["bc", "blast", "c31", "bnew", "bwin", "ot", "e64", "ex"]
    kern = functools.partial(_attn_sample_kernel, past_len=past_len, n_blk=n_blk, nb_rows=nb_rows)
    grid_spec = pltpu.PrefetchScalarGridSpec(
        num_scalar_prefetch=1,
        grid=(bsz, n_steps),
        in_specs=[per_b(q2), per_b(ng), per_b(kc), per_b(vc), per_b(kvs_new), per_b(kvw_new), per_b(win_state)]
                 + page_specs + [full(tabs[n]) for n in names],
        out_specs=pl.BlockSpec((1, tq, ATTN_DIM), lambda b, k, pt: (b, 0, 0)),
        scratch_shapes=[pltpu.VMEM((rows, KV_DIM), _BF16),
                        pltpu.VMEM((tq, 3 * ATTN_DIM), _F32),
                        pltpu.VMEM((tq, ATTN_DIM), _F32),
                        pltpu.VMEM((rows, KV_DIM), _F32),
                        pltpu.VMEM((rows, LANES), _F32),
                        pltpu.VMEM((rows, LANES), _F32),
                        pltpu.VMEM((selt_rows, rows), _F32),
                        pltpu.VMEM((nb_rows, rows), _F32)],
    )
    return pl.pallas_call(
        kern,
        grid_spec=grid_spec,
        out_shape=jax.ShapeDtypeStruct((bsz, tq, ATTN_DIM), _BF16),
        compiler_params=_params("parallel", "arbitrary"),
        name="attn_sample",
    )(page_table, q2, ng, kc, vc, kvs_new, kvw_new, win_state, *([slc_pages] * PAGES_PER_STEP),
      *[tabs[n] for n in names])


def _sample_tables(rel_bias, past_len, tq, lw, n_cmp_pad):
    rows = N_KV * HPG * tq
    rel = rel_bias.astype(_F32)
    ridx = np.arange(rows)
    head = ridx // tq
    qi = ridx % tq
    tpos = past_len + qi

    bias_of = lambda dist: _bias_table(rel, dist, head[:, None])

    jc = np.arange(n_cmp_pad)
    bc = bias_of(tpos[:, None] - (CMP_STRIDE * jc[None, :] + CMP_LEN - CMP_STRIDE - 1))
    sub_keys = SUB_PAGES * PAGE_SIZE
    blast = bias_of(tpos[:, None] - (past_len - sub_keys + np.arange(sub_keys))[None, :])
    bnew = bias_of(qi[:, None] - np.arange(LANES)[None, :])
    bwin = bias_of(lw + qi[:, None] - np.arange(lw)[None, :])
    c31 = jnp.broadcast_to(rel[REL_BUCKETS - 1][head][:, None], (rows, LANES))
    n_blk = -(-(past_len + tq) // SEL_BLOCK)
    nb_rows = -(-n_blk // 8) * 8
    n = jc - 1
    cs = n * CMP_STRIDE
    bs = np.arange(nb_rows) * SEL_BLOCK
    ov = np.clip(np.minimum(cs[None, :] + CMP_LEN, bs[:, None] + SEL_BLOCK) - np.maximum(cs[None, :], bs[:, None]),
                 0, CMP_LEN).astype(np.float32) / CMP_LEN
    ov[:, 0] = 0.0
    ov[n_blk:, :] = 0.0
    step_keys = PAGES_PER_STEP * PAGE_SIZE
    e64 = (np.arange(step_keys)[None, :] // SEL_BLOCK == np.arange(LANES)[:, None]).astype(np.float32)
    return dict(bc=bc, blast=blast, c31=c31, bnew=bnew, bwin=bwin, ot=jnp.asarray(ov, _BF16),
                e64=jnp.asarray(e64, _BF16), ex=_gate_expand())


def _layer_weights(w_in, phi_pe, phi_w1, phi_w2, w_attn_out, conv_w, w_conv_out, w_o, w_up, w_down):
    d = w_in.shape[0]
    o_q, o_kc, o_ks, o_kw = 0, ATTN_DIM, ATTN_DIM + 2 * KV_DIM, ATTN_DIM + 4 * KV_DIM
    o_ng = ATTN_DIM + 6 * KV_DIM
    o_glu = o_ng + 3 * N_HEADS
    o_mg = o_glu + 2 * CONV_DIM
    bf = lambda a: a.astype(_BF16)
    wq = w_in[:, o_q:o_kc].reshape(d, N_KV, HPG, HEAD_DIM).transpose(0, 2, 1, 3).reshape(d, ATTN_DIM)
    wng = w_in[:, o_ng:o_glu].reshape(d, N_KV, HPG, 3).transpose(0, 3, 2, 1).reshape(d, 3 * N_HEADS)
    wng = jnp.pad(wng, ((0, 0), (0, LANES - 3 * N_HEADS)))
    w5 = phi_w1.reshape(2, CMP_R, CMP_STRIDE, HEAD_DIM, PHI_HIDDEN)
    eye = jnp.eye(N_KV, dtype=_F32)
    w1bd = jnp.einsum("crsde,gh->csgdrhe", w5, eye).reshape(2, CMP_STRIDE, KV_DIM, CMP_R * KV_DIM)
    w2bd = jnp.einsum("che,gk->cghke", phi_w2, eye).reshape(2, KV_DIM, KV_DIM)
    rep = LANES // PHI_HIDDEN
    pe_b = jnp.broadcast_to(phi_pe.reshape(2, CMP_LEN * HEAD_DIM, 1), (2, CMP_LEN * HEAD_DIM, LANES))
    w1t = jnp.tile(phi_w1, (1, 1, rep))
    wao = w_attn_out.reshape(N_KV, HPG, HEAD_DIM, d).transpose(1, 0, 2, 3).reshape(ATTN_DIM, d)
    half = PROJ_TILE // 2
    wga = w_in[:, o_glu:o_glu + CONV_DIM].reshape(d, CONV_DIM // half, half)
    wgb = w_in[:, o_glu + CONV_DIM:o_mg].reshape(d, CONV_DIM // half, half)
    wglu = jnp.concatenate([wga, wgb], axis=2).reshape(d, 2 * CONV_DIM)
    w_all = jnp.concatenate([wq, w_in[:, o_kc:o_ng], wglu, w_in[:, o_mg:]], axis=1)
    return dict(
        w_all=bf(w_all), wng=bf(wng), w1bd=bf(w1bd), w2bd=bf(w2bd), pe_b=pe_b, w1t=w1t,
        wao=bf(wao), wco=bf(w_conv_out), wo=bf(w_o), wup=bf(w_up), wdown=bf(w_down),
        conv_w=jnp.pad(conv_w, ((0, HALO - CONV_K), (0, 0))))


def _finish(x2, attn2, conv, mg, w, g_post_mix, g_pre_ffn, g_post_ffn, tm):
    mixed = _mix(attn2, conv, mg, w["wao"], w["wco"], tm, 512)
    x1 = _oproj(mixed, x2, w["wo"], g_post_mix, tm)
    return _ffn(x1, g_pre_ffn, g_post_ffn, w["wup"], w["wdown"], tm, 512)


def kernel(x_prompt, x_sample, cache_kv_cmp, cache_kv_slc, state_kv_win, state_conv, page_table, w_in, phi_pe,
           phi_w1, phi_w2, rel_bias, w_attn_out, conv_w, conv_b, conv_ln_g, conv_ln_b, w_conv_out, w_o, w_up,
           w_down, g_pre_mix, g_post_mix, g_pre_ffn, g_post_ffn):
    depth = w_in.shape[0]
    bp, tp, d = x_prompt.shape
    bs, ts, _ = x_sample.shape
    n_pages = page_table.shape[1]
    past_len = n_pages * PAGE_SIZE
    lw = state_kv_win.shape[2]
    chunks_per_page = PAGE_SIZE // CMP_STRIDE
    chunk_cols = CMP_STRIDE * 2 * KV_DIM
    assert ts < CMP_STRIDE and tp % CMP_STRIDE == 0 and lw == WINDOW and tp >= WINDOW

    tabs_p = _prompt_tables(rel_bias, tp)
    tabs_s = _sample_tables(rel_bias, past_len, ts, lw, past_len // CMP_STRIDE)
    yp, ys = x_prompt.reshape(bp * tp, d), x_sample.reshape(bs * ts, d)
    outs = [[] for _ in range(8)]
    row = lambda a: a.reshape(1, -1)
    kv5 = lambda a, b, t: a.reshape(b, t, 2, N_KV, HEAD_DIM)
    for l in range(depth):
        w = _layer_weights(w_in[l], phi_pe[l], phi_w1[l], phi_w2[l], w_attn_out[l], conv_w[l], w_conv_out[l],
                           w_o[l], w_up[l], w_down[l])
        gpm, gqm, gpf, gqf = row(g_pre_mix[l]), row(g_post_mix[l]), row(g_pre_ffn[l]), row(g_post_ffn[l])
        cargs = (w["conv_w"], row(conv_b[l]), row(conv_ln_g[l]), row(conv_ln_b[l]))
        cmp_w = (w["w1bd"], w["w2bd"], w["pe_b"], w["w1t"])

        tm = 512
        q2, kvc, (kvs, ks16, vst16), (kvw, kw16, vwt16), ng, u, mg = _proj(yp, gpm, w["w_all"], w["wng"], tm, tp)
        n_chunk = tp // CMP_STRIDE
        kc, _, vct = _compress([kvc.reshape(bp, n_chunk, chunk_cols)],
                               [pl.BlockSpec((1, n_chunk, chunk_cols), lambda b, k: (b, 0, 0))],
                               (bp, 1), n_chunk, bp, n_chunk, *cmp_w)
        attn2 = _attn_prompt(q2.reshape(bp, tp, ATTN_DIM), ng.reshape(bp, tp, LANES),
                             ks16.reshape(bp, tp, KV_DIM), vst16, kw16.reshape(bp, tp, KV_DIM), vwt16,
                             kc, vct, tabs_p)
        u3 = u.reshape(bp, tp, CONV_DIM)
        conv = _conv(u3, u3, *cargs, 256, True)
        yp = _finish(yp, attn2.reshape(bp * tp, ATTN_DIM), conv.reshape(bp * tp, CONV_DIM), mg, w, gqm, gpf, gqf, tm)
        outs[0].append(kv5(kvc, bp, tp))
        outs[2].append(kv5(kvs, bp, tp))
        outs[4].append(kv5(kvw, bp, tp)[:, tp - WINDOW:])
        outs[6].append(u3[:, tp - (CONV_K - 1):])

        tm = bs * ts
        q2, kvc, (kvs,), (kvw,), ng, u, mg = _proj(ys, gpm, w["w_all"], w["wng"], tm)
        kc, vc, _ = _compress([_transposed_rows(cache_kv_cmp[l])] * PAGES_PER_STEP, _page_specs(),
                           (bs, n_pages // PAGES_PER_STEP), PAGES_PER_STEP * chunks_per_page, bs,
                           past_len // CMP_STRIDE, *cmp_w, page_table=page_table)
        attn2 = _attn_sample(q2.reshape(bs, ts, ATTN_DIM), ng.reshape(bs, ts, LANES), kc, vc,
                             kvs.reshape(bs, ts, 2 * KV_DIM), kvw.reshape(bs, ts, 2 * KV_DIM),
                             _transposed_rows(state_kv_win[l]), _transposed_rows(cache_kv_slc[l]),
                             page_table, tabs_s, past_len)
        u3 = u.reshape(bs, ts, CONV_DIM)
        hist = jnp.pad(state_conv[l], ((0, 0), (HALO - (CONV_K - 1), 0), (0, 0)))
        conv = _conv(u3, hist, *cargs, ts, False)
        ys = _finish(ys, attn2.reshape(bs * ts, ATTN_DIM), conv.reshape(bs * ts, CONV_DIM), mg, w, gqm, gpf, gqf, tm)
        outs[1].append(kv5(kvc, bs, ts))
        outs[3].append(kv5(kvs, bs, ts))
        win_rows = jnp.concatenate([state_kv_win[l], kv5(kvw, bs, ts)], axis=1)
        outs[5].append(win_rows[:, win_rows.shape[1] - min(WINDOW, win_rows.shape[1]):])
        up = jnp.concatenate([state_conv[l], u3], axis=1)
        outs[7].append(up[:, up.shape[1] - (CONV_K - 1):])

    stack = lambda i: jnp.stack(outs[i])
    return (yp.reshape(bp, tp, d), ys.reshape(bs, ts, d), stack(0), stack(1), stack(2), stack(3),
            stack(4), stack(5), stack(6), stack(7))
```

```python
import functools
import math

import jax
import jax.numpy as jnp
import numpy as np
from jax import lax
from jax.experimental import pallas as pl
from jax.experimental.pallas import tpu as pltpu

D_MODEL = 2048
N_HEADS = 16
HEAD_DIM = 64
N_KV = 4
HPG = N_HEADS // N_KV
ATTN_DIM = N_HEADS * HEAD_DIM
KV_DIM = N_KV * HEAD_DIM
CMP_LEN = 32
CMP_STRIDE = 16
CMP_R = CMP_LEN // CMP_STRIDE
PHI_HIDDEN = HEAD_DIM
SEL_BLOCK = 64
N_SELECT = 16
WINDOW = 512
Q_BLOCK = 64
CONV_DIM = D_MODEL // 2
CONV_K = 31
D_FF = 4 * D_MODEL
REL_BUCKETS = 32
REL_MAX_DIST = 128
EPS = 1e-6
NEG = -1e30
PAGE_SIZE = 128

LANES = 128
VMEM_LIMIT_BYTES = 56 * 1024 * 1024

NEAR_KEYS = 384
NEAR_VARIANTS = 6
FAR_TILE = 512
WIN_FAR_KEYS = 384
MASK_BIG = 1e30
ROW_CHUNK = 32
HALO = 32

_F32 = jnp.float32
_BF16 = jnp.bfloat16


def _params(*sem):
    return pltpu.CompilerParams(dimension_semantics=sem, vmem_limit_bytes=VMEM_LIMIT_BYTES)


def _dot(a, b):
    return jnp.dot(a, b, preferred_element_type=_F32)


def _dot_nt(a, b):
    return lax.dot_general(a, b, (((1,), (1,)), ((), ())), preferred_element_type=_F32)


def _split3(x):
    hi = x.astype(_BF16)
    r1 = x - hi.astype(_F32)
    mid = r1.astype(_BF16)
    lo = (r1 - mid.astype(_F32)).astype(_BF16)
    return hi, mid, lo


def _rms(x, g):
    return x * lax.rsqrt(jnp.mean(x * x, axis=-1, keepdims=True) + EPS) * g


def _rel_bucket_np(dist):
    n = np.maximum(dist, 0)
    exact = REL_BUCKETS // 2
    logb = exact + (np.log(np.maximum(n, 1).astype(np.float32) / np.float32(exact))
                    / np.float32(math.log(REL_MAX_DIST / exact)) * (REL_BUCKETS - exact)).astype(np.int32)
    return np.where(n < exact, n, np.minimum(logb, REL_BUCKETS - 1)).astype(np.int32)


PROJ_TILE = 512
PROJ_SEGMENTS = (("q", 0, 2), ("kvc", 2, 1), ("kvs", 3, 1), ("kvw", 4, 1), ("glu", 5, 4), ("mg", 9, 8))
PROJ_TILES = 17


def _proj_kernel(x_ref, g_ref, w_ref, wng_ref, *refs, transposed_v):
    h_ref = refs[-1]
    if transposed_v:
        (q_ref, kvc_ref, kvc16_ref, kvs_ref, ks16_ref, vst_ref, kvw_ref, kw16_ref, vwt_ref,
         ng_ref, u_ref, mg_ref) = refs[:-1]
    else:
        q_ref, kvc_ref, kvs_ref, kvw_ref, ng_ref, u_ref, mg_ref = refs[:-1]
    j = pl.program_id(1)
    seg = {name: (lo, lo + n) for name, lo, n in PROJ_SEGMENTS}
    inside = lambda name: (j >= seg[name][0]) & (j < seg[name][1])

    @pl.when(j == 0)
    def _():
        h_ref[...] = _rms(x_ref[...], g_ref[...]).astype(_BF16)
        ng_ref[...] = _dot(h_ref[...], wng_ref[...])

    acc = _dot(h_ref[...], w_ref[...])

    @pl.when(inside("q"))
    def _():
        q_ref[...] = (acc * (HEAD_DIM ** -0.5)).astype(_BF16)

    @pl.when(inside("kvc"))
    def _():
        if transposed_v:
            kvc_ref[0] = acc.T
            kvc16_ref[...] = acc.astype(_BF16)
        else:
            kvc_ref[...] = acc

    def kv_out(f32_ref, k16_ref, vt_ref):
        if transposed_v:
            acc_t = acc.T
            f32_ref[0] = acc_t
            k16_ref[...] = acc[:, :KV_DIM].astype(_BF16)
            vt_ref[0] = acc_t[KV_DIM:, :].astype(_BF16)
        else:
            f32_ref[...] = acc

    @pl.when(inside("kvs"))
    def _():
        kv_out(kvs_ref, ks16_ref if transposed_v else None, vst_ref if transposed_v else None)

    @pl.when(inside("kvw"))
    def _():
        kv_out(kvw_ref, kw16_ref if transposed_v else None, vwt_ref if transposed_v else None)

    @pl.when(inside("glu"))
    def _():
        half = PROJ_TILE // 2
        u_ref[...] = acc[:, :half] * jax.nn.sigmoid(acc[:, half:])

    @pl.when(inside("mg"))
    def _():
        mg_ref[...] = jax.nn.sigmoid(acc).astype(_BF16)


def _proj(x, g, w_all, wng, tm, seq_len=None):
    m, d = x.shape
    tn = PROJ_TILE
    assert w_all.shape == (d, PROJ_TILES * tn)
    transposed_v = seq_len is not None
    seg = {name: (lo, n) for name, lo, n in PROJ_SEGMENTS}

    def spec(name, width=tn):
        lo, n = seg[name]
        return pl.BlockSpec((tm, width), lambda i, j: (i, jnp.clip(j - lo, 0, n - 1)))

    f32 = lambda n: jax.ShapeDtypeStruct((m, n), _F32)
    b16 = lambda n: jax.ShapeDtypeStruct((m, n), _BF16)
    if transposed_v:
        assert seq_len % tm == 0
        spb = seq_len // tm
        nb = m // seq_len
        t_spec = lambda rows: pl.BlockSpec((1, rows, tm), lambda i, j: (i // spb, 0, i % spb))
        row_spec = lambda width: pl.BlockSpec((tm, width), lambda i, j: (i, 0))
        f32_t = jax.ShapeDtypeStruct((nb, tn, seq_len), _F32)
        kc_specs, kc_shapes = [t_spec(tn), row_spec(tn)], [f32_t, b16(tn)]
        kv_specs = [t_spec(tn), row_spec(KV_DIM), t_spec(KV_DIM)]
        kv_shapes = [f32_t, b16(KV_DIM), jax.ShapeDtypeStruct((nb, KV_DIM, seq_len), _BF16)]
        kw_specs, kw_shapes = kv_specs, kv_shapes
    else:
        kc_specs, kc_shapes = [spec("kvc")], [f32(tn)]
        kv_specs, kv_shapes = [spec("kvs")], [f32(tn)]
        kw_specs, kw_shapes = [spec("kvw")], [f32(tn)]
    out_specs = ([spec("q")] + kc_specs + kv_specs + kw_specs
                 + [pl.BlockSpec((tm, LANES), lambda i, j: (i, 0)), spec("glu", tn // 2), spec("mg")])
    out_shape = ([b16(seg["q"][1] * tn)] + kc_shapes + kv_shapes + kw_shapes
                 + [f32(LANES), f32(seg["glu"][1] * tn // 2), b16(seg["mg"][1] * tn)])
    outs = pl.pallas_call(
        functools.partial(_proj_kernel, transposed_v=transposed_v),
        grid=(m // tm, PROJ_TILES),
        in_specs=[pl.BlockSpec((tm, d), lambda i, j: (i, 0)),
                  pl.BlockSpec((1, d), lambda i, j: (0, 0)),
                  pl.BlockSpec((d, tn), lambda i, j: (0, j)),
                  pl.BlockSpec((d, LANES), lambda i, j: (0, 0))],
        out_specs=out_specs,
        out_shape=out_shape,
        scratch_shapes=[pltpu.VMEM((tm, d), _BF16)],
        compiler_params=_params("parallel", "arbitrary"),
        name="proj",
    )(x, g, w_all, wng)
    nkc, nkv = len(kc_specs), len(kv_specs)
    q2, kvc = outs[0], tuple(outs[1:1 + nkc])
    kvs, kvw = tuple(outs[1 + nkc:1 + nkc + nkv]), tuple(outs[1 + nkc + nkv:1 + nkc + 2 * nkv])
    ng, u, mg = outs[1 + nkc + 2 * nkv:]
    return q2, kvc, kvs, kvw, ng, u, mg


def _mix_kernel(a_ref, c_ref, ga_ref, gc_ref, wa_ref, wc_ref, o_ref):
    ya = _dot(a_ref[...], wa_ref[...])
    yc = _dot(c_ref[...], wc_ref[...])
    o_ref[...] = (ga_ref[...].astype(_F32) * ya + gc_ref[...].astype(_F32) * yc).astype(_BF16)


def _mix(attn, conv, mg, wao, wco, tm, tn):
    m, ka = attn.shape
    n = wao.shape[1]
    nb = n // tn
    return pl.pallas_call(
        _mix_kernel,
        grid=(m // tm, nb),
        in_specs=[pl.BlockSpec((tm, ka), lambda i, j: (i, 0)),
                  pl.BlockSpec((tm, conv.shape[1]), lambda i, j: (i, 0)),
                  pl.BlockSpec((tm, tn), lambda i, j: (i, j)),
                  pl.BlockSpec((tm, tn), lambda i, j: (i, j + nb)),
                  pl.BlockSpec((ka, tn), lambda i, j: (0, j)),
                  pl.BlockSpec((conv.shape[1], tn), lambda i, j: (0, j))],
        out_specs=pl.BlockSpec((tm, tn), lambda i, j: (i, j)),
        out_shape=jax.ShapeDtypeStruct((m, n), _BF16),
        compiler_params=_params("parallel", "arbitrary"),
        name="mix",
    )(attn, conv, mg, mg, wao, wco)


def _oproj_kernel(mx_ref, x_ref, w_ref, g_ref, o_ref):
    y = _dot(mx_ref[...], w_ref[...])
    o_ref[...] = x_ref[...] + _rms(y, g_ref[...])


def _oproj(mixed, x, wo, g, tm):
    m, d = x.shape
    return pl.pallas_call(
        _oproj_kernel,
        grid=(m // tm,),
        in_specs=[pl.BlockSpec((tm, d), lambda i: (i, 0)),
                  pl.BlockSpec((tm, d), lambda i: (i, 0)),
                  pl.BlockSpec((d, d), lambda i: (0, 0)),
                  pl.BlockSpec((1, d), lambda i: (0, 0))],
        out_specs=pl.BlockSpec((tm, d), lambda i: (i, 0)),
        out_shape=jax.ShapeDtypeStruct((m, d), _F32),
        compiler_params=_params("parallel"),
        name="oproj",
    )(mixed, x, wo, g)


def _ffn_kernel(x_ref, gpre_ref, gpost_ref, wu_ref, wd_ref, o_ref, h_ref, acc_ref):
    j = pl.program_id(1)

    @pl.when(j == 0)
    def _():
        h_ref[...] = _rms(x_ref[...], gpre_ref[...]).astype(_BF16)
        acc_ref[...] = jnp.zeros_like(acc_ref)

    a = jnp.maximum(_dot(h_ref[...], wu_ref[...]), 0.0)
    acc_ref[...] += _dot((a * a).astype(_BF16), wd_ref[...])

    @pl.when(j == pl.num_programs(1) - 1)
    def _():
        o_ref[...] = x_ref[...] + _rms(acc_ref[...], gpost_ref[...])


def _ffn(x, gpre, gpost, wu, wd, tm, tf):
    m, d = x.shape
    f = wu.shape[1]
    return pl.pallas_call(
        _ffn_kernel,
        grid=(m // tm, f // tf),
        in_specs=[pl.BlockSpec((tm, d), lambda i, j: (i, 0)),
                  pl.BlockSpec((1, d), lambda i, j: (0, 0)),
                  pl.BlockSpec((1, d), lambda i, j: (0, 0)),
                  pl.BlockSpec((d, tf), lambda i, j: (0, j)),
                  pl.BlockSpec((tf, d), lambda i, j: (j, 0))],
        out_specs=pl.BlockSpec((tm, d), lambda i, j: (i, 0)),
        out_shape=jax.ShapeDtypeStruct((m, d), _F32),
        scratch_shapes=[pltpu.VMEM((tm, d), _BF16), pltpu.VMEM((tm, d), _F32)],
        compiler_params=_params("parallel", "arbitrary"),
        name="ffn",
    )(x, gpre, gpost, wu, wd)


def _conv_kernel(u_ref, halo_ref, w_ref, b_ref, lg_ref, lb_ref, o_ref, win_ref, *, tt, zero_first):
    c = u_ref.shape[-1]
    halo = halo_ref[0]
    if zero_first:
        halo = jnp.where(pl.program_id(1) == 0, 0.0, halo)
    win_ref[0:HALO, :] = halo
    win_ref[HALO:HALO + tt, :] = u_ref[0]
    rc = min(ROW_CHUNK, tt)
    off = HALO - (CONV_K - 1)
    for ch in range(tt // rc):
        acc = jnp.zeros((rc, c), _F32) + b_ref[...]
        for k in range(CONV_K):
            acc = acc + w_ref[k:k + 1, :] * win_ref[ch * rc + off + k:ch * rc + off + k + rc, :]
        mu = jnp.mean(acc, axis=-1, keepdims=True)
        xc = acc - mu
        var = jnp.mean(xc * xc, axis=-1, keepdims=True)
        y = xc * lax.rsqrt(var + EPS) * lg_ref[...] + lb_ref[...]
        o_ref[0, ch * rc:(ch + 1) * rc, :] = (y * jax.nn.sigmoid(y)).astype(_BF16)


def _conv(u, halo_src, w, b, lg, lb, tt, zero_first):
    bsz, t, c = u.shape
    nhb = tt // HALO
    if zero_first:
        halo_map = lambda bi, ti: (bi, jnp.maximum(ti * nhb - 1, 0), 0)
    else:
        halo_map = lambda bi, ti: (bi, 0, 0)
    kern = functools.partial(_conv_kernel, tt=tt, zero_first=zero_first)
    return pl.pallas_call(
        kern,
        grid=(bsz, t // tt),
        in_specs=[pl.BlockSpec((1, tt, c), lambda bi, ti: (bi, ti, 0)),
                  pl.BlockSpec((1, HALO, c), halo_map),
                  pl.BlockSpec((HALO, c), lambda bi, ti: (0, 0)),
                  pl.BlockSpec((1, c), lambda bi, ti: (0, 0)),
                  pl.BlockSpec((1, c), lambda bi, ti: (0, 0)),
                  pl.BlockSpec((1, c), lambda bi, ti: (0, 0))],
        out_specs=pl.BlockSpec((1, tt, c), lambda bi, ti: (bi, ti, 0)),
        out_shape=jax.ShapeDtypeStruct((bsz, t, c), _BF16),
        scratch_shapes=[pltpu.VMEM((HALO + tt, c), _F32)],
        compiler_params=_params("parallel", "arbitrary"),
        name="conv",
    )(u, halo_src, w, b, lg, lb)


def _compress_kernel(*refs, n_src, nrow, paged):
    if paged:
        refs = refs[1:]
        rows_ref = refs[-1]
        refs = refs[:-1]
    src_refs = refs[:n_src]
    w1_ref, w2_ref, pe_ref, w1t_ref, kc_ref, vc_ref, vct_ref, carry_ref = refs[n_src:]
    k = pl.program_id(1)

    @pl.when(k == 0)
    def _():
        carry_ref[...] = jnp.zeros_like(carry_ref)

    if paged:
        for p, r in enumerate(src_refs):
            for lc in range(2 * KV_DIM // LANES):
                rows_ref[lc, p * PAGE_SIZE:(p + 1) * PAGE_SIZE, :] = r[0, lc * LANES:(lc + 1) * LANES, :].T
    outs = []
    for c in range(2):
        acc = jnp.zeros((nrow, 2 * KV_DIM), _F32)
        for s in range(CMP_STRIDE):
            if paged:
                lcs = range(c * KV_DIM // LANES, (c + 1) * KV_DIM // LANES)
                xs = jnp.concatenate([rows_ref[lc, pl.ds(s, nrow, stride=CMP_STRIDE), :] for lc in lcs], axis=-1)
            else:
                lo = s * 2 * KV_DIM + c * KV_DIM
                xs = src_refs[0][0, :, lo:lo + KV_DIM]
            acc = acc + _dot(xs.astype(_BF16), w1_ref[c, s])
        pt = jnp.sum(pe_ref[c] * w1t_ref[c], axis=0, keepdims=True)
        pt = jnp.concatenate([pt] * (KV_DIM // LANES), axis=-1)
        a0 = acc[:, :KV_DIM]
        a1 = acc[:, KV_DIM:]
        first = lax.broadcasted_iota(jnp.int32, (nrow, KV_DIM), 0) == 0
        a0s = jnp.where(first, carry_ref[c], pltpu.roll(a0, 1, 0))
        carry_ref[c] = a0[nrow - 1:nrow, :]
        hid = jax.nn.gelu(a0s + a1 + pt)
        outs.append(_dot(hid.astype(_BF16), w2_ref[c]))
    kc_ref[0] = outs[0].astype(_BF16)
    vc_ref[0] = outs[1].astype(_BF16)
    vct_ref[0] = outs[1].T.astype(_BF16)


def _compress(src, src_specs, grid, nrow, n_batch, n_out_rows, w1bd, w2bd, pe_b, w1t, page_table=None):
    paged = page_table is not None
    n_src = len(src)
    kern = functools.partial(_compress_kernel, n_src=n_src, nrow=nrow, paged=paged)
    const = lambda a: pl.BlockSpec(a.shape, lambda *_, _n=a.ndim: (0,) * _n, pipeline_mode=pl.Buffered(1))
    omap = lambda b, k, *_: (b, k, 0)
    scratch = [pltpu.VMEM((2, 1, KV_DIM), _F32)]
    if paged:
        scratch.append(pltpu.VMEM((2 * KV_DIM // LANES, n_src * PAGE_SIZE, LANES), _F32))
    grid_spec = pltpu.PrefetchScalarGridSpec(
        num_scalar_prefetch=1 if paged else 0,
        grid=grid,
        in_specs=list(src_specs) + [const(w1bd), const(w2bd), const(pe_b), const(w1t)],
        out_specs=[pl.BlockSpec((1, nrow, KV_DIM), omap), pl.BlockSpec((1, nrow, KV_DIM), omap),
                   pl.BlockSpec((1, KV_DIM, nrow), lambda b, k, *_: (b, 0, k))],
        scratch_shapes=scratch,
    )
    return pl.pallas_call(
        kern,
        grid_spec=grid_spec,
        out_shape=[jax.ShapeDtypeStruct((n_batch, n_out_rows, KV_DIM), _BF16)] * 2
                  + [jax.ShapeDtypeStruct((n_batch, KV_DIM, n_out_rows), _BF16)],
        compiler_params=_params("parallel", "arbitrary"),
        name="compress_paged" if paged else "compress",
    )(*((page_table,) if paged else ()), *src, w1bd, w2bd, pe_b, w1t)


def _rank_select(sc_ref, n_rows, n_iter, n_sel):
    s = sc_ref[...]
    jio = lax.broadcasted_iota(jnp.int32, s.shape, 0)

    def body(jp, cnt):
        row = jnp.broadcast_to(sc_ref[pl.ds(jp, 1), :], s.shape)
        ge = jnp.where(row >= s, 1.0, 0.0)
        gt = jnp.where(row > s, 1.0, 0.0)
        return cnt + jnp.where(jio > jp, ge, gt)

    cnt = lax.fori_loop(0, n_iter, body, jnp.zeros(s.shape, _F32))
    return jnp.where(cnt < n_sel, 1.0, 0.0)


def _softmax_parts(parts):
    ms = [jnp.max(jnp.where(mk, s, NEG), axis=-1, keepdims=True) for s, mk in parts]
    m = functools.reduce(jnp.maximum, ms)
    ps = [jnp.where(mk, jnp.exp(s - m), 0.0) for s, mk in parts]
    l = functools.reduce(lambda a, b: a + b, [jnp.sum(p, axis=-1, keepdims=True) for p in ps])
    inv = 1.0 / jnp.maximum(l, 1e-30)
    return [p * inv for p in ps]


def _softmax_cols(parts):
    ms = [jnp.max(jnp.where(mk, s, NEG), axis=0, keepdims=True) for s, mk in parts]
    m = functools.reduce(jnp.maximum, ms)
    ps = [jnp.where(mk, jnp.exp(s - m), 0.0) for s, mk in parts]
    l = functools.reduce(lambda a, b: a + b, [jnp.sum(p, axis=0, keepdims=True) for p in ps])
    inv = 1.0 / jnp.maximum(l, 1e-30)
    return [p * inv for p in ps]


def _attn_prompt_kernel(q_ref, ng_ref, ks_ref, vst_ref, kw_ref, vwt_ref, kc_ref, vct_ref,
                        tzt_ref, tct_ref, ot_ref, gsel_ref, rsum_ref, rep_ref,
                        o_ref, qzt_ref, otacc_ref, gt_ref, sc_ref, nsel_ref, *, n_cmp_pad, seq_len):
    i = pl.program_id(1)
    qb = Q_BLOCK
    rows = HPG * qb
    lane_g = lax.broadcasted_iota(jnp.int32, (qb, KV_DIM), 1) // HEAD_DIM
    vrows = lambda g: pl.ds(g * HEAD_DIM, HEAD_DIM)

    for g in range(N_KV):
        qz = jnp.concatenate([jnp.where(lane_g == g, q_ref[0, :, r * KV_DIM:(r + 1) * KV_DIM].astype(_F32), 0.0)
                              for r in range(HPG)], axis=0)
        qzt_ref[g] = qz.T.astype(_BF16)

    gparts = _split3(jax.nn.sigmoid(ng_ref[0]))
    gcols = jnp.concatenate([sum(_dot(p, gsel_ref[r]) for p in gparts) for r in range(HPG)], axis=0)
    gt_ref[...] = gcols.T
    otacc_ref[...] = jnp.zeros_like(otacc_ref)

    def emit(branch, g, out_t):
        otacc_ref[g] += gt_ref[pl.ds(branch * N_KV + g, 1), :] * out_t

    jc = lax.broadcasted_iota(jnp.int32, (n_cmp_pad, rows), 0)
    lq = lax.broadcasted_iota(jnp.int32, (n_cmp_pad, rows), 1) % qb
    cmp_valid = (jc >= 1) & (CMP_STRIDE * jc + (CMP_LEN - CMP_STRIDE - 1) - lq <= i * qb)
    mm = lax.broadcasted_iota(jnp.int32, (n_cmp_pad, LANES), 1)
    jj = lax.broadcasted_iota(jnp.int32, (n_cmp_pad, LANES), 0)
    shift_t = jnp.where((mm < 16) & (jj - mm == 4 * i - 8), 1.0, 0.0).astype(_BF16)
    kc_aug = jnp.concatenate([kc_ref[0], shift_t, shift_t, shift_t], axis=1)
    groups = range(N_KV)
    s_c = [_dot(kc_aug, jnp.concatenate([qzt_ref[g], tct_ref[0, g], tct_ref[1, g], tct_ref[2, g]], axis=0))
           for g in groups]
    pn_c = [_softmax_cols([(s, cmp_valid)])[0] for s in s_c]
    for g in groups:
        emit(0, g, _dot(vct_ref[0, vrows(g), :], pn_c[g].astype(_BF16)))
    y_c = [sum(_dot(ot_ref[...], p) for p in _split3(pn_c[g])) for g in groups]
    imp_t = sum(sum(_dot(p, rsum_ref[g]) for p in _split3(y_c[g])) for g in groups)

    jrow = lax.broadcasted_iota(jnp.int32, imp_t.shape, 0)
    forced = (jrow == 0) | (jrow == i) | (jrow == i - 1)
    sc_ref[...] = jnp.where(forced, jnp.inf, jnp.where(jrow <= i, imp_t, -jnp.inf))
    sel_t = _rank_select(sc_ref, imp_t.shape[0], i + 1, N_SELECT)

    h = jnp.minimum(jnp.maximum(i - 3, 0) // 2, (seq_len - NEAR_KEYS) // LANES)
    ns = pl.multiple_of(h * LANES, LANES)
    v = i - 2 * h
    sel16 = sel_t.astype(_BF16)
    for g in groups:
        neg = (_dot(sel16, rep_ref[g]) - 1.0) * MASK_BIG
        nsel_ref[0, g] = neg
        nsel_ref[1, g] = jnp.where(jrow < 2 * h, neg, -MASK_BIG)

    def add_block_mask(s, far, g, j0, n_blocks):
        return jnp.concatenate([s[jj * SEL_BLOCK:(jj + 1) * SEL_BLOCK, :] + nsel_ref[far, g, pl.ds(j0 + jj, 1), :]
                                for jj in range(n_blocks)], axis=0)

    cn = lax.broadcasted_iota(jnp.int32, (NEAR_KEYS, rows), 0)
    lqn = lax.broadcasted_iota(jnp.int32, (NEAR_KEYS, rows), 1) % qb
    causal = cn - lqn <= v * SEL_BLOCK
    near = pl.ds(ns, NEAR_KEYS)
    n_far = (ns + FAR_TILE - 1) // FAR_TILE

    k_near = ks_ref[0, near, :]
    s_n = [add_block_mask(_dot(k_near, qzt_ref[g]) + tzt_ref[v, g], 0, g, 2 * h, NEAR_KEYS // SEL_BLOCK)
           for g in groups]
    s_n = [jnp.where(causal, s, NEG) for s in s_n]
    m0 = [jnp.max(s, axis=0, keepdims=True) for s in s_n]
    p_n = [jnp.exp(s_n[g] - m0[g]) for g in groups]
    l0 = [jnp.sum(p, axis=0, keepdims=True) for p in p_n]
    acc0 = [_dot(vst_ref[0, vrows(g), near], p_n[g].astype(_BF16)) for g in groups]

    def far_body(tau, carry):
        m_old, l_old, acc_old = carry
        k0 = pl.multiple_of(tau * FAR_TILE, FAR_TILE)
        tile = pl.ds(k0, FAR_TILE)
        kt = ks_ref[0, tile, :]
        s_f = [add_block_mask(_dot(kt, qzt_ref[g]), 1, g, tau * (FAR_TILE // SEL_BLOCK), FAR_TILE // SEL_BLOCK)
               for g in groups]
        m_new = [jnp.maximum(m_old[g], jnp.max(s_f[g], axis=0, keepdims=True)) for g in groups]
        alpha = [jnp.exp(m_old[g] - m_new[g]) for g in groups]
        p_f = [jnp.exp(s_f[g] - m_new[g]) for g in groups]
        l_new = [alpha[g] * l_old[g] + jnp.sum(p_f[g], axis=0, keepdims=True) for g in groups]
        acc = [alpha[g] * acc_old[g] + _dot(vst_ref[0, vrows(g), tile], p_f[g].astype(_BF16)) for g in groups]
        return tuple(m_new), tuple(l_new), tuple(acc)

    _, l1, acc1 = lax.fori_loop(0, n_far, far_body, (tuple(m0), tuple(l0), tuple(acc0)))
    for g in groups:
        emit(1, g, acc1[g] * (1.0 / l1[g]))

    fs = pl.multiple_of((jnp.maximum(i - WINDOW // SEL_BLOCK, 0) // 2) * LANES, LANES)
    wfar = pl.ds(fs, WIN_FAR_KEYS)
    cf = lax.broadcasted_iota(jnp.int32, (WIN_FAR_KEYS, rows), 0)
    lqf = lax.broadcasted_iota(jnp.int32, (WIN_FAR_KEYS, rows), 1) % qb
    far_ok = (i * qb + lqf - fs - cf < WINDOW) & (fs + cf < ns)
    kw_near, kw_far = kw_ref[0, near, :], kw_ref[0, wfar, :]
    s_wn = [_dot(kw_near, qzt_ref[g]) + tzt_ref[v, g] for g in groups]
    s_wf = [_dot(kw_far, qzt_ref[g]) for g in groups]
    p_w = [_softmax_cols([(s_wn[g], causal), (s_wf[g], far_ok)]) for g in groups]
    for g in groups:
        emit(2, g, _dot(vwt_ref[0, vrows(g), near], p_w[g][0].astype(_BF16))
             + _dot(vwt_ref[0, vrows(g), wfar], p_w[g][1].astype(_BF16)))

    res = otacc_ref[...].reshape(N_KV * HEAD_DIM, rows).T
    for r in range(HPG):
        o_ref[0, :, r * KV_DIM:(r + 1) * KV_DIM] = res[r * qb:(r + 1) * qb, :].astype(_BF16)


def _attn_prompt(q2, ng, ks, vst, kw, vwt, kc, vct, tabs):
    bsz, t, _ = q2.shape
    nblk = t // SEL_BLOCK
    n_cmp_pad = kc.shape[1]
    rows = HPG * Q_BLOCK
    assert t % FAR_TILE == 0 and t >= NEAR_KEYS
    names = ["tzt", "tct", "ot", "gsel", "rsum", "rep"]
    full = lambda a: pl.BlockSpec(a.shape, lambda b, i, _n=a.ndim: (0,) * _n, pipeline_mode=pl.Buffered(1))
    per_b = lambda a: pl.BlockSpec((1,) + a.shape[1:], lambda b, i: (b, 0, 0))
    kern = functools.partial(_attn_prompt_kernel, n_cmp_pad=n_cmp_pad, seq_len=t)
    return pl.pallas_call(
        kern,
        grid=(bsz, nblk),
        in_specs=[pl.BlockSpec((1, Q_BLOCK, ATTN_DIM), lambda b, i: (b, i, 0)),
                  pl.BlockSpec((1, Q_BLOCK, LANES), lambda b, i: (b, i, 0)),
                  per_b(ks), per_b(vst), per_b(kw), per_b(vwt), per_b(kc), per_b(vct)]
                 + [full(tabs[n]) for n in names],
        out_specs=pl.BlockSpec((1, Q_BLOCK, ATTN_DIM), lambda b, i: (b, i, 0)),
        out_shape=jax.ShapeDtypeStruct((bsz, t, ATTN_DIM), _BF16),
        scratch_shapes=[pltpu.VMEM((N_KV, KV_DIM, rows), _BF16),
                        pltpu.VMEM((N_KV, HEAD_DIM, rows), _F32),
                        pltpu.VMEM((LANES, rows), _F32),
                        pltpu.VMEM((SEL_BLOCK, rows), _F32),
                        pltpu.VMEM((2, N_KV, SEL_BLOCK, rows), _F32)],
        compiler_params=_params("parallel", "arbitrary"),
        name="attn_prompt",
    )(q2, ng, ks, vst, kw, vwt, kc, vct, *[tabs[n] for n in names])


def _bias_table(rel, dist, head):
    nmax = max(int(dist.max()), 1) + 1
    bk = _rel_bucket_np(np.arange(nmax))
    rel_h = rel[:, head]
    out = jnp.broadcast_to(rel_h[0], np.broadcast_shapes(dist.shape, head.shape))
    dist = lax.optimization_barrier(jnp.asarray(dist, jnp.int32))
    for b in range(1, int(bk.max()) + 1):
        first = int(np.argmax(bk >= b))
        out = jnp.where(dist >= first, rel_h[b], out)
    return out


def _prompt_tables(rel_bias, t):
    nblk = t // SEL_BLOCK
    n_cmp_pad = t // CMP_STRIDE
    assert nblk <= SEL_BLOCK and n_cmp_pad % LANES == 0
    rows = HPG * Q_BLOCK
    rel = rel_bias.astype(_F32)
    r_idx = np.arange(rows) // Q_BLOCK
    q_idx = np.arange(rows) % Q_BLOCK
    head = np.arange(N_KV)[:, None] * HPG + r_idx[None, :]
    c31 = rel[REL_BUCKETS - 1][head]
    c = np.arange(NEAR_KEYS)
    dist = np.arange(NEAR_VARIANTS)[:, None, None] * SEL_BLOCK + q_idx[None, None, :] - c[None, :, None]
    tzt = _bias_table(rel, dist[:, None], head[None, :, None, :]) - c31[None, :, None, :]
    mmv = np.arange(16)
    dist_c = q_idx[None, :] - CMP_STRIDE * (mmv[:, None] - 8) - (CMP_LEN - CMP_STRIDE - 1)
    delta = _bias_table(rel, dist_c[None], head[:, None, :]) - c31[:, None, :]
    tct = jnp.stack(_split3(jnp.pad(delta, ((0, 0), (0, LANES - 16), (0, 0)))))
    n = np.arange(n_cmp_pad) - 1
    cs = n * CMP_STRIDE
    bs = np.arange(SEL_BLOCK) * SEL_BLOCK
    ov = np.clip(np.minimum(cs[None, :] + CMP_LEN, bs[:, None] + SEL_BLOCK) - np.maximum(cs[None, :], bs[:, None]),
                 0, CMP_LEN).astype(np.float32) / CMP_LEN
    ov[:, 0] = 0.0
    ov[nblk:, :] = 0.0
    gsel = np.zeros((HPG, LANES, LANES), np.float32)
    for j in range(3):
        for r in range(HPG):
            for g in range(N_KV):
                gsel[r, j * N_HEADS + r * N_KV + g, j * N_KV + g] = 1.0
    rsum = np.zeros((N_KV, rows, N_KV * Q_BLOCK), np.float32)
    for g in range(N_KV):
        rsum[g, np.arange(rows), g * Q_BLOCK + q_idx] = 1.0
    bf = lambda a: jnp.asarray(a, _BF16)
    return dict(tzt=tzt, tct=tct, ot=bf(ov), gsel=bf(gsel), rsum=bf(rsum),
                rep=bf(rsum.transpose(0, 2, 1)))


def _gate_expand():
    ex = np.zeros((LANES, 3 * ATTN_DIM), np.float32)
    for j in range(3):
        for r in range(HPG):
            for g in range(N_KV):
                col = j * ATTN_DIM + r * KV_DIM + g * HEAD_DIM
                ex[j * N_HEADS + r * N_KV + g, col:col + HEAD_DIM] = 1.0
    return jnp.asarray(ex, _BF16)


PAGES_PER_STEP = 32
SUB_PAGES = 32


def _attn_sample_kernel(pt_ref, q_ref, ng_ref, kc_ref, vc_ref, knew_ref, wnew_ref, wst_ref, *refs,
                        past_len, n_blk, nb_rows):
    del pt_ref
    page_refs = refs[:PAGES_PER_STEP]
    (bc_ref, blast_ref, c31_ref, bnew_ref, bwin_ref, ot_ref, e64_ref, ex_ref,
     o_ref, qall_ref, gate_ref, oacc_ref, acc_ref, m_ref, l_ref, selt_ref, sc_ref) = refs[PAGES_PER_STEP:]
    k = pl.program_id(1)
    nk = pl.num_programs(1)
    tq = q_ref.shape[1]
    rows = N_KV * HPG * tq
    lane_g = lax.broadcasted_iota(jnp.int32, (tq, KV_DIM), 1) // HEAD_DIM
    rq = lax.broadcasted_iota(jnp.int32, (rows, 1), 0) % tq
    sub_keys = SUB_PAGES * PAGE_SIZE

    def emit(branch, out):
        for g in range(N_KV):
            for r in range(HPG):
                col = r * KV_DIM
                row0 = (g * HPG + r) * tq
                gt = gate_ref[:, branch * ATTN_DIM + col:branch * ATTN_DIM + col + KV_DIM]
                oacc_ref[:, col:col + KV_DIM] += jnp.where(lane_g == g, gt * out[row0:row0 + tq, :], 0.0)

    def pad_rows(x, n):
        return jnp.concatenate([x, jnp.zeros((n - x.shape[0], x.shape[1]), x.dtype)], axis=0)

    @pl.when(k == 0)
    def _():
        qf = q_ref[0].astype(_F32)
        pieces = []
        for g in range(N_KV):
            for r in range(HPG):
                pieces.append(jnp.where(lane_g == g, qf[:, r * KV_DIM:(r + 1) * KV_DIM], 0.0))
        qall = jnp.concatenate(pieces, axis=0).astype(_BF16)
        qall_ref[...] = qall
        gs = jax.nn.sigmoid(ng_ref[0])
        gate_ref[...] = sum(_dot(p, ex_ref[...]) for p in _split3(gs))
        oacc_ref[...] = jnp.zeros_like(oacc_ref)

        n_cmp_pad = kc_ref.shape[1]
        jc = lax.broadcasted_iota(jnp.int32, (rows, n_cmp_pad), 1)
        cmp_valid = (jc >= 1) & (CMP_STRIDE * jc + (CMP_LEN - CMP_STRIDE - 1) - rq <= past_len)
        s = _dot_nt(qall, kc_ref[0]) + bc_ref[...]
        (pn,) = _softmax_parts([(s, cmp_valid)])
        emit(0, _dot(pn.astype(_BF16), vc_ref[0]))
        imp_rows = []
        for g in range(N_KV):
            sg = sum(pn[(g * HPG + r) * tq:(g * HPG + r + 1) * tq, :] for r in range(HPG))
            imp_rows += [sg] * HPG
        imp = jnp.concatenate(imp_rows, axis=0)
        imp_t = sum(_dot_nt(ot_ref[...], p) for p in _split3(imp))

        jrow = lax.broadcasted_iota(jnp.int32, imp_t.shape, 0)
        tpos = past_len + lax.broadcasted_iota(jnp.int32, imp_t.shape, 1) % tq
        cur = tpos // SEL_BLOCK
        forced = (jrow == 0) | (jrow == cur) | (jrow == cur - 1)
        valid = jrow * SEL_BLOCK <= tpos
        sc_ref[...] = jnp.where(forced, jnp.inf, jnp.where(valid, imp_t, -jnp.inf))
        sel_t = _rank_select(sc_ref, nb_rows, n_blk, min(N_SELECT, n_blk))
        selt_ref[...] = pad_rows(sel_t, selt_ref.shape[0])

        kn = pad_rows(knew_ref[0, :, :KV_DIM], LANES).astype(_BF16)
        vn = pad_rows(knew_ref[0, :, KV_DIM:], LANES).astype(_BF16)
        cn = lax.broadcasted_iota(jnp.int32, (rows, LANES), 1)
        mk = (cn <= rq) & (cn < tq)
        s = _dot_nt(qall, kn) + bnew_ref[...]
        m = jnp.max(jnp.where(mk, s, NEG), axis=-1, keepdims=True)
        p = jnp.where(mk, jnp.exp(s - m), 0.0)
        m_ref[...] = jnp.broadcast_to(m, m_ref.shape)
        l_ref[...] = jnp.broadcast_to(jnp.sum(p, axis=-1, keepdims=True), l_ref.shape)
        acc_ref[...] = _dot(p.astype(_BF16), vn)

    qall = qall_ref[...]
    blk_per_step = PAGES_PER_STEP * PAGE_SIZE // SEL_BLOCK
    j0 = pl.multiple_of(k * blk_per_step, blk_per_step)
    sel_step = selt_ref[pl.ds(j0, LANES), :].T.astype(_BF16)
    c31 = jnp.concatenate([c31_ref[...]] * (sub_keys // LANES), axis=-1)
    n_sub = PAGES_PER_STEP // SUB_PAGES
    for st in range(n_sub):
        pages = page_refs[st * SUB_PAGES:(st + 1) * SUB_PAGES]
        kt = jnp.concatenate([r[0, :KV_DIM, :] for r in pages], axis=1).astype(_BF16)
        vt = jnp.concatenate([r[0, KV_DIM:, :] for r in pages], axis=1).astype(_BF16)
        mk = _dot(sel_step, e64_ref[:, st * sub_keys:(st + 1) * sub_keys]) > 0.5
        if st == n_sub - 1:
            bias = jnp.where(k == nk - 1, blast_ref[...], c31)
        else:
            bias = c31
        s = jnp.where(mk, _dot(qall, kt) + bias, NEG)
        m_old = m_ref[:, :1]
        m_new = jnp.maximum(m_old, jnp.max(s, axis=-1, keepdims=True))
        alpha = jnp.exp(m_old - m_new)
        p = jnp.exp(s - m_new)
        l_ref[...] = jnp.broadcast_to(alpha * l_ref[:, :1] + jnp.sum(p, axis=-1, keepdims=True), l_ref.shape)
        m_ref[...] = jnp.broadcast_to(m_new, m_ref.shape)
        acc_ref[...] = alpha * acc_ref[...] + _dot_nt(p.astype(_BF16), vt)

    @pl.when(k == nk - 1)
    def _():
        emit(1, acc_ref[...] * (1.0 / l_ref[:, :1]))
        lw = wst_ref.shape[2]
        kw = wst_ref[0, :KV_DIM, :].astype(_BF16)
        vw = wst_ref[0, KV_DIM:, :].astype(_BF16)
        kn = pad_rows(wnew_ref[0, :, :KV_DIM], LANES).astype(_BF16)
        vn = pad_rows(wnew_ref[0, :, KV_DIM:], LANES).astype(_BF16)
        cw = lax.broadcasted_iota(jnp.int32, (rows, lw), 1)
        dw = lw + rq - cw
        cn = lax.broadcasted_iota(jnp.int32, (rows, LANES), 1)
        pw, pnw = _softmax_parts([(_dot(qall, kw) + bwin_ref[...], (dw >= 0) & (dw < WINDOW)),
                                  (_dot_nt(qall, kn) + bnew_ref[...], (cn <= rq) & (cn < tq))])
        emit(2, _dot_nt(pw.astype(_BF16), vw) + _dot(pnw.astype(_BF16), vn))
        o_ref[0] = oacc_ref[...].astype(_BF16)


def _page_specs():
    return [pl.BlockSpec((1, 2 * KV_DIM, PAGE_SIZE),
                         lambda b, k, pt, _p=p: (pt[b, k * PAGES_PER_STEP + _p], 0, 0))
            for p in range(PAGES_PER_STEP)]


def _transposed_rows(a):
    n, rows = a.shape[:2]
    return jnp.transpose(a, (0, 2, 3, 4, 1)).reshape(n, 2 * KV_DIM, rows)


def _attn_sample(q2, ng, kc, vc, kvs_new, kvw_new, win_state, slc_pages, page_table, tabs, past_len):
    bsz, tq, _ = q2.shape
    n_pages = page_table.shape[1]
    assert n_pages % PAGES_PER_STEP == 0 and past_len == n_pages * PAGE_SIZE and past_len % SEL_BLOCK == 0
    n_steps = n_pages // PAGES_PER_STEP
    rows = N_KV * HPG * tq
    assert rows == LANES
    n_blk = -(-(past_len + tq) // SEL_BLOCK)
    nb_rows = tabs["ot"].shape[0]
    blk_per_step = PAGES_PER_STEP * PAGE_SIZE // SEL_BLOCK
    selt_rows = (n_steps - 1) * blk_per_step + LANES
    assert selt_rows >= nb_rows
    full = lambda a: pl.BlockSpec(a.shape, lambda b, k, pt, _n=a.ndim: (0,) * _n)
    per_b = lambda a: pl.BlockSpec((1,) + a.shape[1:], lambda b, k, pt: (b, 0, 0))
    page_specs = _page_specs()
    names =["bc", "blast", "c31", "bnew", "bwin", "ot", "e64", "ex"]
    kern = functools.partial(_attn_sample_kernel, past_len=past_len, n_blk=n_blk, nb_rows=nb_rows)
    grid_spec = pltpu.PrefetchScalarGridSpec(
        num_scalar_prefetch=1,
        grid=(bsz, n_steps),
        in_specs=[per_b(q2), per_b(ng), per_b(kc), per_b(vc), per_b(kvs_new), per_b(kvw_new), per_b(win_state)]
                 + page_specs + [full(tabs[n]) for n in names],
        out_specs=pl.BlockSpec((1, tq, ATTN_DIM), lambda b, k, pt: (b, 0, 0)),
        scratch_shapes=[pltpu.VMEM((rows, KV_DIM), _BF16),
                        pltpu.VMEM((tq, 3 * ATTN_DIM), _F32),
                        pltpu.VMEM((tq, ATTN_DIM), _F32),
                        pltpu.VMEM((rows, KV_DIM), _F32),
                        pltpu.VMEM((rows, LANES), _F32),
                        pltpu.VMEM((rows, LANES), _F32),
                        pltpu.VMEM((selt_rows, rows), _F32),
                        pltpu.VMEM((nb_rows, rows), _F32)],
    )
    return pl.pallas_call(
        kern,
        grid_spec=grid_spec,
        out_shape=jax.ShapeDtypeStruct((bsz, tq, ATTN_DIM), _BF16),
        compiler_params=_params("parallel", "arbitrary"),
        name="attn_sample",
    )(page_table, q2, ng, kc, vc, kvs_new, kvw_new, win_state, *([slc_pages] * PAGES_PER_STEP),
      *[tabs[n] for n in names])


def _sample_tables(rel_bias, past_len, tq, lw, n_cmp_pad):
    rows = N_KV * HPG * tq
    rel = rel_bias.astype(_F32)
    ridx = np.arange(rows)
    head = ridx // tq
    qi = ridx % tq
    tpos = past_len + qi

    bias_of = lambda dist: _bias_table(rel, dist, head[:, None])

    jc = np.arange(n_cmp_pad)
    bc = bias_of(tpos[:, None] - (CMP_STRIDE * jc[None, :] + CMP_LEN - CMP_STRIDE - 1))
    sub_keys = SUB_PAGES * PAGE_SIZE
    blast = bias_of(tpos[:, None] - (past_len - sub_keys + np.arange(sub_keys))[None, :])
    bnew = bias_of(qi[:, None] - np.arange(LANES)[None, :])
    bwin = bias_of(lw + qi[:, None] - np.arange(lw)[None, :])
    c31 = jnp.broadcast_to(rel[REL_BUCKETS - 1][head][:, None], (rows, LANES))
    n_blk = -(-(past_len + tq) // SEL_BLOCK)
    nb_rows = -(-n_blk // 8) * 8
    n = jc - 1
    cs = n * CMP_STRIDE
    bs = np.arange(nb_rows) * SEL_BLOCK
    ov = np.clip(np.minimum(cs[None, :] + CMP_LEN, bs[:, None] + SEL_BLOCK) - np.maximum(cs[None, :], bs[:, None]),
                 0, CMP_LEN).astype(np.float32) / CMP_LEN
    ov[:, 0] = 0.0
    ov[n_blk:, :] = 0.0
    step_keys = PAGES_PER_STEP * PAGE_SIZE
    e64 = (np.arange(step_keys)[None, :] // SEL_BLOCK == np.arange(LANES)[:, None]).astype(np.float32)
    return dict(bc=bc, blast=blast, c31=c31, bnew=bnew, bwin=bwin, ot=jnp.asarray(ov, _BF16),
                e64=jnp.asarray(e64, _BF16), ex=_gate_expand())


def _layer_weights(w_in, phi_pe, phi_w1, phi_w2, w_attn_out, conv_w, w_conv_out, w_o, w_up, w_down):
    d = w_in.shape[0]
    o_q, o_kc, o_ks, o_kw = 0, ATTN_DIM, ATTN_DIM + 2 * KV_DIM, ATTN_DIM + 4 * KV_DIM
    o_ng = ATTN_DIM + 6 * KV_DIM
    o_glu = o_ng + 3 * N_HEADS
    o_mg = o_glu + 2 * CONV_DIM
    bf = lambda a: a.astype(_BF16)
    wq = w_in[:, o_q:o_kc].reshape(d, N_KV, HPG, HEAD_DIM).transpose(0, 2, 1, 3).reshape(d, ATTN_DIM)
    wng = w_in[:, o_ng:o_glu].reshape(d, N_KV, HPG, 3).transpose(0, 3, 2, 1).reshape(d, 3 * N_HEADS)
    wng = jnp.pad(wng, ((0, 0), (0, LANES - 3 * N_HEADS)))
    w5 = phi_w1.reshape(2, CMP_R, CMP_STRIDE, HEAD_DIM, PHI_HIDDEN)
    eye = jnp.eye(N_KV, dtype=_F32)
    w1bd = jnp.einsum("crsde,gh->csgdrhe", w5, eye).reshape(2, CMP_STRIDE, KV_DIM, CMP_R * KV_DIM)
    w2bd = jnp.einsum("che,gk->cghke", phi_w2, eye).reshape(2, KV_DIM, KV_DIM)
    rep = LANES // PHI_HIDDEN
    pe_b = jnp.broadcast_to(phi_pe.reshape(2, CMP_LEN * HEAD_DIM, 1), (2, CMP_LEN * HEAD_DIM, LANES))
    w1t = jnp.tile(phi_w1, (1, 1, rep))
    wao = w_attn_out.reshape(N_KV, HPG, HEAD_DIM, d).transpose(1, 0, 2, 3).reshape(ATTN_DIM, d)
    half = PROJ_TILE // 2
    wga = w_in[:, o_glu:o_glu + CONV_DIM].reshape(d, CONV_DIM // half, half)
    wgb = w_in[:, o_glu + CONV_DIM:o_mg].reshape(d, CONV_DIM // half, half)
    wglu = jnp.concatenate([wga, wgb], axis=2).reshape(d, 2 * CONV_DIM)
    w_all = jnp.concatenate([wq, w_in[:, o_kc:o_ng], wglu, w_in[:, o_mg:]], axis=1)
    return dict(
        w_all=bf(w_all), wng=bf(wng), w1bd=bf(w1bd), w2bd=bf(w2bd), pe_b=pe_b, w1t=w1t,
        wao=bf(wao), wco=bf(w_conv_out), wo=bf(w_o), wup=bf(w_up), wdown=bf(w_down),
        conv_w=jnp.pad(conv_w, ((0, HALO - CONV_K), (0, 0))))


def _finish(x2, attn2, conv, mg, w, g_post_mix, g_pre_ffn, g_post_ffn, tm):
    mixed = _mix(attn2, conv, mg, w["wao"], w["wco"], tm, 512)
    x1 = _oproj(mixed, x2, w["wo"], g_post_mix, tm)
    return _ffn(x1, g_pre_ffn, g_post_ffn, w["wup"], w["wdown"], tm, 512)


def kernel(x_prompt, x_sample, cache_kv_cmp, cache_kv_slc, state_kv_win, state_conv, page_table, w_in, phi_pe,
           phi_w1, phi_w2, rel_bias, w_attn_out, conv_w, conv_b, conv_ln_g, conv_ln_b, w_conv_out, w_o, w_up,
           w_down, g_pre_mix, g_post_mix, g_pre_ffn, g_post_ffn):
    depth = w_in.shape[0]
    bp, tp, d = x_prompt.shape
    bs, ts, _ = x_sample.shape
    n_pages = page_table.shape[1]
    past_len = n_pages * PAGE_SIZE
    lw = state_kv_win.shape[2]
    chunks_per_page = PAGE_SIZE // CMP_STRIDE
    chunk_cols = CMP_STRIDE * 2 * KV_DIM
    assert ts < CMP_STRIDE and tp % CMP_STRIDE == 0 and lw == WINDOW and tp >= WINDOW

    tabs_p = _prompt_tables(rel_bias, tp)
    tabs_s = _sample_tables(rel_bias, past_len, ts, lw, past_len // CMP_STRIDE)
    yp, ys = x_prompt.reshape(bp * tp, d), x_sample.reshape(bs * ts, d)
    outs = [[] for _ in range(8)]
    row = lambda a: a.reshape(1, -1)
    kv5 = lambda a, b, t: a.reshape(b, t, 2, N_KV, HEAD_DIM)
    kv5_t = lambda a: jnp.transpose(a.reshape(a.shape[0], 2, N_KV, HEAD_DIM, a.shape[2]), (0, 4, 1, 2, 3))
    for l in range(depth):
        w = _layer_weights(w_in[l], phi_pe[l], phi_w1[l], phi_w2[l], w_attn_out[l], conv_w[l], w_conv_out[l],
                           w_o[l], w_up[l], w_down[l])
        gpm, gqm, gpf, gqf = row(g_pre_mix[l]), row(g_post_mix[l]), row(g_pre_ffn[l]), row(g_post_ffn[l])
        cargs = (w["conv_w"], row(conv_b[l]), row(conv_ln_g[l]), row(conv_ln_b[l]))
        cmp_w = (w["w1bd"], w["w2bd"], w["pe_b"], w["w1t"])

        tm = 512
        q2, (kvc_t, kvc16), (kvs_t, ks16, vst16), (kvw_t, kw16, vwt16), ng, u, mg = _proj(
            yp, gpm, w["w_all"], w["wng"], tm, tp)
        n_chunk = tp // CMP_STRIDE
        kc, _, vct = _compress([kvc16.reshape(bp, n_chunk, chunk_cols)],
                               [pl.BlockSpec((1, n_chunk, chunk_cols), lambda b, k: (b, 0, 0))],
                               (bp, 1), n_chunk, bp, n_chunk, *cmp_w)
        attn2 = _attn_prompt(q2.reshape(bp, tp, ATTN_DIM), ng.reshape(bp, tp, LANES),
                             ks16.reshape(bp, tp, KV_DIM), vst16, kw16.reshape(bp, tp, KV_DIM), vwt16,
                             kc, vct, tabs_p)
        u3 = u.reshape(bp, tp, CONV_DIM)
        conv = _conv(u3, u3, *cargs, 256, True)
        yp = _finish(yp, attn2.reshape(bp * tp, ATTN_DIM), conv.reshape(bp * tp, CONV_DIM), mg, w, gqm, gpf, gqf, tm)
        outs[0].append(kv5_t(kvc_t))
        outs[2].append(kv5_t(kvs_t))
        outs[4].append(kv5_t(kvw_t[:, :, tp - WINDOW:]))
        outs[6].append(u3[:, tp - (CONV_K - 1):])

        tm = bs * ts
        q2, (kvc,), (kvs,), (kvw,), ng, u, mg = _proj(ys, gpm, w["w_all"], w["wng"], tm)
        kc, vc, _ = _compress([_transposed_rows(cache_kv_cmp[l])] * PAGES_PER_STEP, _page_specs(),
                           (bs, n_pages // PAGES_PER_STEP), PAGES_PER_STEP * chunks_per_page, bs,
                           past_len // CMP_STRIDE, *cmp_w, page_table=page_table)
        attn2 = _attn_sample(q2.reshape(bs, ts, ATTN_DIM), ng.reshape(bs, ts, LANES), kc, vc,
                             kvs.reshape(bs, ts, 2 * KV_DIM), kvw.reshape(bs, ts, 2 * KV_DIM),
                             _transposed_rows(state_kv_win[l]), _transposed_rows(cache_kv_slc[l]),
                             page_table, tabs_s, past_len)
        u3 = u.reshape(bs, ts, CONV_DIM)
        hist = jnp.pad(state_conv[l], ((0, 0), (HALO - (CONV_K - 1), 0), (0, 0)))
        conv = _conv(u3, hist, *cargs, ts, False)
        ys = _finish(ys, attn2.reshape(bs * ts, ATTN_DIM), conv.reshape(bs * ts, CONV_DIM), mg, w, gqm, gpf, gqf, tm)
        outs[1].append(kv5(kvc, bs, ts))
        outs[3].append(kv5(kvs, bs, ts))
        win_rows = jnp.concatenate([state_kv_win[l], kv5(kvw, bs, ts)], axis=1)
        outs[5].append(win_rows[:, win_rows.shape[1] - min(WINDOW, win_rows.shape[1]):])
        up = jnp.concatenate([state_conv[l], u3], axis=1)
        outs[7].append(up[:, up.shape[1] - (CONV_K - 1):])

    stack = lambda i: jnp.stack(outs[i])
    return (yp.reshape(bp, tp, d), ys.reshape(bs, ts, d), stack(0), stack(1), stack(2), stack(3),
            stack(4), stack(5), stack(6), stack(7))
```

```python
import functools
import math

import jax
import jax.numpy as jnp
import numpy as np
from jax import lax
from jax.experimental import pallas as pl
from jax.experimental.pallas import tpu as pltpu

D_MODEL = 2048
N_HEADS = 16
HEAD_DIM = 64
N_KV = 4
HPG = N_HEADS // N_KV
ATTN_DIM = N_HEADS * HEAD_DIM
KV_DIM = N_KV * HEAD_DIM
CMP_LEN = 32
CMP_STRIDE = 16
CMP_R = CMP_LEN // CMP_STRIDE
PHI_HIDDEN = HEAD_DIM
SEL_BLOCK = 64
N_SELECT = 16
WINDOW = 512
Q_BLOCK = 64
CONV_DIM = D_MODEL // 2
CONV_K = 31
D_FF = 4 * D_MODEL
REL_BUCKETS = 32
REL_MAX_DIST = 128
EPS = 1e-6
NEG = -1e30
PAGE_SIZE = 128

LANES = 128
SUBLANES = 8
VMEM_LIMIT_BYTES = 56 * 1024 * 1024

NEAR_KEYS = 384
NEAR_VARIANTS = 6
FAR_TILE = 512
WIN_FAR_KEYS = 384
MASK_BIG = 1e30
ROW_CHUNK = 32
HALO = 32

_F32 = jnp.float32
_BF16 = jnp.bfloat16


def _params(*sem):
    return pltpu.CompilerParams(dimension_semantics=sem, vmem_limit_bytes=VMEM_LIMIT_BYTES)


def _dot(a, b):
    return jnp.dot(a, b, preferred_element_type=_F32)


def _dot_nt(a, b):
    return lax.dot_general(a, b, (((1,), (1,)), ((), ())), preferred_element_type=_F32)


def _split3(x):
    hi = x.astype(_BF16)
    r1 = x - hi.astype(_F32)
    mid = r1.astype(_BF16)
    lo = (r1 - mid.astype(_F32)).astype(_BF16)
    return hi, mid, lo


def _rms(x, g):
    return x * lax.rsqrt(jnp.mean(x * x, axis=-1, keepdims=True) + EPS) * g


def _rel_bucket_np(dist):
    n = np.maximum(dist, 0)
    exact = REL_BUCKETS // 2
    logb = exact + (np.log(np.maximum(n, 1).astype(np.float32) / np.float32(exact))
                    / np.float32(math.log(REL_MAX_DIST / exact)) * (REL_BUCKETS - exact)).astype(np.int32)
    return np.where(n < exact, n, np.minimum(logb, REL_BUCKETS - 1)).astype(np.int32)


PROJ_TILE = 512
PROJ_SEGMENTS = (("q", 0, 2), ("kvc", 2, 1), ("kvs", 3, 1), ("kvw", 4, 1), ("glu", 5, 4), ("mg", 9, 8))
PROJ_TILES = 17


def _proj_kernel(x_ref, g_ref, w_ref, wng_ref, *refs, transposed_v):
    h_ref = refs[-1]
    if transposed_v:
        (q_ref, kvc_ref, kvc16_ref, kvs_ref, ks16_ref, vst_ref, kvw_ref, kw16_ref, vwt_ref,
         ng_ref, u_ref, mg_ref) = refs[:-1]
    else:
        q_ref, kvc_ref, kvs_ref, kvw_ref, ng_ref, u_ref, mg_ref = refs[:-1]
    j = pl.program_id(1)
    seg = {name: (lo, lo + n) for name, lo, n in PROJ_SEGMENTS}
    inside = lambda name: (j >= seg[name][0]) & (j < seg[name][1])

    @pl.when(j == 0)
    def _():
        h_ref[...] = _rms(x_ref[...], g_ref[...]).astype(_BF16)
        ng_ref[...] = _dot(h_ref[...], wng_ref[...])

    acc = _dot(h_ref[...], w_ref[...])

    @pl.when(inside("q"))
    def _():
        q_ref[...] = (acc * (HEAD_DIM ** -0.5)).astype(_BF16)

    @pl.when(inside("kvc"))
    def _():
        if transposed_v:
            kvc_ref[0] = acc.T
            kvc16_ref[...] = acc.astype(_BF16)
        else:
            kvc_ref[...] = acc

    def kv_out(f32_ref, k16_ref, vt_ref):
        if transposed_v:
            acc_t = acc.T
            f32_ref[0] = acc_t
            k16_ref[...] = acc[:, :KV_DIM].astype(_BF16)
            vt_ref[0] = acc_t[KV_DIM:, :].astype(_BF16)
        else:
            f32_ref[...] = acc

    @pl.when(inside("kvs"))
    def _():
        kv_out(kvs_ref, ks16_ref if transposed_v else None, vst_ref if transposed_v else None)

    @pl.when(inside("kvw"))
    def _():
        kv_out(kvw_ref, kw16_ref if transposed_v else None, vwt_ref if transposed_v else None)

    @pl.when(inside("glu"))
    def _():
        half = PROJ_TILE // 2
        u_ref[...] = acc[:, :half] * jax.nn.sigmoid(acc[:, half:])

    @pl.when(inside("mg"))
    def _():
        mg_ref[...] = jax.nn.sigmoid(acc).astype(_BF16)


def _proj(x, g, w_all, wng, tm, seq_len=None):
    m, d = x.shape
    tn = PROJ_TILE
    assert w_all.shape == (d, PROJ_TILES * tn)
    transposed_v = seq_len is not None
    seg = {name: (lo, n) for name, lo, n in PROJ_SEGMENTS}

    def spec(name, width=tn):
        lo, n = seg[name]
        return pl.BlockSpec((tm, width), lambda i, j: (i, jnp.clip(j - lo, 0, n - 1)))

    f32 = lambda n: jax.ShapeDtypeStruct((m, n), _F32)
    b16 = lambda n: jax.ShapeDtypeStruct((m, n), _BF16)
    if transposed_v:
        assert seq_len % tm == 0
        spb = seq_len // tm
        nb = m // seq_len
        t_spec = lambda rows: pl.BlockSpec((1, rows, tm), lambda i, j: (i // spb, 0, i % spb))
        row_spec = lambda width: pl.BlockSpec((tm, width), lambda i, j: (i, 0))
        f32_t = jax.ShapeDtypeStruct((nb, tn, seq_len), _F32)
        kc_specs, kc_shapes = [t_spec(tn), row_spec(tn)], [f32_t, b16(tn)]
        kv_specs = [t_spec(tn), row_spec(KV_DIM), t_spec(KV_DIM)]
        kv_shapes = [f32_t, b16(KV_DIM), jax.ShapeDtypeStruct((nb, KV_DIM, seq_len), _BF16)]
        kw_specs, kw_shapes = kv_specs, kv_shapes
    else:
        kc_specs, kc_shapes = [spec("kvc")], [f32(tn)]
        kv_specs, kv_shapes = [spec("kvs")], [f32(tn)]
        kw_specs, kw_shapes = [spec("kvw")], [f32(tn)]
    out_specs = ([spec("q")] + kc_specs + kv_specs + kw_specs
                 + [pl.BlockSpec((tm, LANES), lambda i, j: (i, 0)), spec("glu", tn // 2), spec("mg")])
    out_shape = ([b16(seg["q"][1] * tn)] + kc_shapes + kv_shapes + kw_shapes
                 + [f32(LANES), f32(seg["glu"][1] * tn // 2), b16(seg["mg"][1] * tn)])
    outs = pl.pallas_call(
        functools.partial(_proj_kernel, transposed_v=transposed_v),
        grid=(m // tm, PROJ_TILES),
        in_specs=[pl.BlockSpec((tm, d), lambda i, j: (i, 0)),
                  pl.BlockSpec((1, d), lambda i, j: (0, 0)),
                  pl.BlockSpec((d, tn), lambda i, j: (0, j)),
                  pl.BlockSpec((d, LANES), lambda i, j: (0, 0))],
        out_specs=out_specs,
        out_shape=out_shape,
        scratch_shapes=[pltpu.VMEM((tm, d), _BF16)],
        compiler_params=_params("parallel", "arbitrary"),
        name="proj",
    )(x, g, w_all, wng)
    nkc, nkv = len(kc_specs), len(kv_specs)
    q2, kvc = outs[0], tuple(outs[1:1 + nkc])
    kvs, kvw = tuple(outs[1 + nkc:1 + nkc + nkv]), tuple(outs[1 + nkc + nkv:1 + nkc + 2 * nkv])
    ng, u, mg = outs[1 + nkc + 2 * nkv:]
    return q2, kvc, kvs, kvw, ng, u, mg


def _mix_kernel(a_ref, c_ref, ga_ref, gc_ref, wa_ref, wc_ref, o_ref):
    ya = _dot(a_ref[...], wa_ref[...])
    yc = _dot(c_ref[...], wc_ref[...])
    o_ref[...] = (ga_ref[...].astype(_F32) * ya + gc_ref[...].astype(_F32) * yc).astype(_BF16)


def _mix(attn, conv, mg, wao, wco, tm, tn):
    m, ka = attn.shape
    n = wao.shape[1]
    nb = n // tn
    return pl.pallas_call(
        _mix_kernel,
        grid=(m // tm, nb),
        in_specs=[pl.BlockSpec((tm, ka), lambda i, j: (i, 0)),
                  pl.BlockSpec((tm, conv.shape[1]), lambda i, j: (i, 0)),
                  pl.BlockSpec((tm, tn), lambda i, j: (i, j)),
                  pl.BlockSpec((tm, tn), lambda i, j: (i, j + nb)),
                  pl.BlockSpec((ka, tn), lambda i, j: (0, j)),
                  pl.BlockSpec((conv.shape[1], tn), lambda i, j: (0, j))],
        out_specs=pl.BlockSpec((tm, tn), lambda i, j: (i, j)),
        out_shape=jax.ShapeDtypeStruct((m, n), _BF16),
        compiler_params=_params("parallel", "arbitrary"),
        name="mix",
    )(attn, conv, mg, mg, wao, wco)


def _oproj_kernel(mx_ref, x_ref, w_ref, g_ref, o_ref):
    y = _dot(mx_ref[...], w_ref[...])
    o_ref[...] = x_ref[...] + _rms(y, g_ref[...])


def _oproj(mixed, x, wo, g, tm):
    m, d = x.shape
    return pl.pallas_call(
        _oproj_kernel,
        grid=(m // tm,),
        in_specs=[pl.BlockSpec((tm, d), lambda i: (i, 0)),
                  pl.BlockSpec((tm, d), lambda i: (i, 0)),
                  pl.BlockSpec((d, d), lambda i: (0, 0)),
                  pl.BlockSpec((1, d), lambda i: (0, 0))],
        out_specs=pl.BlockSpec((tm, d), lambda i: (i, 0)),
        out_shape=jax.ShapeDtypeStruct((m, d), _F32),
        compiler_params=_params("parallel"),
        name="oproj",
    )(mixed, x, wo, g)


def _ffn_kernel(x_ref, gpre_ref, gpost_ref, wu_ref, wd_ref, o_ref, h_ref, acc_ref):
    j = pl.program_id(1)

    @pl.when(j == 0)
    def _():
        h_ref[...] = _rms(x_ref[...], gpre_ref[...]).astype(_BF16)
        acc_ref[...] = jnp.zeros_like(acc_ref)

    a = jnp.maximum(_dot(h_ref[...], wu_ref[...]), 0.0)
    acc_ref[...] += _dot((a * a).astype(_BF16), wd_ref[...])

    @pl.when(j == pl.num_programs(1) - 1)
    def _():
        o_ref[...] = x_ref[...] + _rms(acc_ref[...], gpost_ref[...])


def _ffn(x, gpre, gpost, wu, wd, tm, tf):
    m, d = x.shape
    f = wu.shape[1]
    return pl.pallas_call(
        _ffn_kernel,
        grid=(m // tm, f // tf),
        in_specs=[pl.BlockSpec((tm, d), lambda i, j: (i, 0)),
                  pl.BlockSpec((1, d), lambda i, j: (0, 0)),
                  pl.BlockSpec((1, d), lambda i, j: (0, 0)),
                  pl.BlockSpec((d, tf), lambda i, j: (0, j)),
                  pl.BlockSpec((tf, d), lambda i, j: (j, 0))],
        out_specs=pl.BlockSpec((tm, d), lambda i, j: (i, 0)),
        out_shape=jax.ShapeDtypeStruct((m, d), _F32),
        scratch_shapes=[pltpu.VMEM((tm, d), _BF16), pltpu.VMEM((tm, d), _F32)],
        compiler_params=_params("parallel", "arbitrary"),
        name="ffn",
    )(x, gpre, gpost, wu, wd)


def _conv_kernel(u_ref, halo_ref, w_ref, b_ref, lg_ref, lb_ref, o_ref, win_ref, *, tt, zero_first):
    c = u_ref.shape[-1]
    halo = halo_ref[0]
    if zero_first:
        halo = jnp.where(pl.program_id(1) == 0, 0.0, halo)
    win_ref[0, 0:HALO, :] = halo
    win_ref[0, HALO:HALO + tt, :] = u_ref[0]
    span = HALO + tt - SUBLANES
    for s in range(1, SUBLANES):
        win_ref[s, 0:span, :] = win_ref[0, s:s + span, :]
    rc = min(ROW_CHUNK, tt)
    off = HALO - (CONV_K - 1)
    for ch in range(tt // rc):
        acc = jnp.zeros((rc, c), _F32) + b_ref[...]
        for k in range(CONV_K):
            s = (off + k) % SUBLANES
            row = ch * rc + off + k - s
            acc = acc + w_ref[k:k + 1, :] * win_ref[s, row:row + rc, :]
        mu = jnp.mean(acc, axis=-1, keepdims=True)
        xc = acc - mu
        var = jnp.mean(xc * xc, axis=-1, keepdims=True)
        y = xc * lax.rsqrt(var + EPS) * lg_ref[...] + lb_ref[...]
        o_ref[0, ch * rc:(ch + 1) * rc, :] = (y * jax.nn.sigmoid(y)).astype(_BF16)


def _conv(u, halo_src, w, b, lg, lb, tt, zero_first):
    bsz, t, c = u.shape
    nhb = tt // HALO
    if zero_first:
        halo_map = lambda bi, ti: (bi, jnp.maximum(ti * nhb - 1, 0), 0)
    else:
        halo_map = lambda bi, ti: (bi, 0, 0)
    kern = functools.partial(_conv_kernel, tt=tt, zero_first=zero_first)
    return pl.pallas_call(
        kern,
        grid=(bsz, t // tt),
        in_specs=[pl.BlockSpec((1, tt, c), lambda bi, ti: (bi, ti, 0)),
                  pl.BlockSpec((1, HALO, c), halo_map),
                  pl.BlockSpec((HALO, c), lambda bi, ti: (0, 0)),
                  pl.BlockSpec((1, c), lambda bi, ti: (0, 0)),
                  pl.BlockSpec((1, c), lambda bi, ti: (0, 0)),
                  pl.BlockSpec((1, c), lambda bi, ti: (0, 0))],
        out_specs=pl.BlockSpec((1, tt, c), lambda bi, ti: (bi, ti, 0)),
        out_shape=jax.ShapeDtypeStruct((bsz, t, c), _BF16),
        scratch_shapes=[pltpu.VMEM((SUBLANES, HALO + tt, c), _F32)],
        compiler_params=_params("parallel", "arbitrary"),
        name="conv",
    )(u, halo_src, w, b, lg, lb)


def _compress_kernel(*refs, n_src, nrow, paged):
    if paged:
        refs = refs[1:]
        rows_ref = refs[-1]
        refs = refs[:-1]
    src_refs = refs[:n_src]
    w1_ref, w2_ref, pe_ref, w1t_ref, kc_ref, vc_ref, vct_ref, carry_ref = refs[n_src:]
    k = pl.program_id(1)

    @pl.when(k == 0)
    def _():
        carry_ref[...] = jnp.zeros_like(carry_ref)

    if paged:
        for p, r in enumerate(src_refs):
            for lc in range(2 * KV_DIM // LANES):
                rows_ref[lc, p * PAGE_SIZE:(p + 1) * PAGE_SIZE, :] = r[0, lc * LANES:(lc + 1) * LANES, :].T
    outs = []
    for c in range(2):
        acc = jnp.zeros((nrow, 2 * KV_DIM), _F32)
        for s in range(CMP_STRIDE):
            if paged:
                lcs = range(c * KV_DIM // LANES, (c + 1) * KV_DIM // LANES)
                xs = jnp.concatenate([rows_ref[lc, pl.ds(s, nrow, stride=CMP_STRIDE), :] for lc in lcs], axis=-1)
            else:
                lo = s * 2 * KV_DIM + c * KV_DIM
                xs = src_refs[0][0, :, lo:lo + KV_DIM]
            acc = acc + _dot(xs.astype(_BF16), w1_ref[c, s])
        pt = jnp.sum(pe_ref[c] * w1t_ref[c], axis=0, keepdims=True)
        pt = jnp.concatenate([pt] * (KV_DIM // LANES), axis=-1)
        a0 = acc[:, :KV_DIM]
        a1 = acc[:, KV_DIM:]
        first = lax.broadcasted_iota(jnp.int32, (nrow, KV_DIM), 0) == 0
        a0s = jnp.where(first, carry_ref[c], pltpu.roll(a0, 1, 0))
        carry_ref[c] = a0[nrow - 1:nrow, :]
        hid = jax.nn.gelu(a0s + a1 + pt)
        outs.append(_dot(hid.astype(_BF16), w2_ref[c]))
    kc_ref[0] = outs[0].astype(_BF16)
    vc_ref[0] = outs[1].astype(_BF16)
    vct_ref[0] = outs[1].T.astype(_BF16)


def _compress(src, src_specs, grid, nrow, n_batch, n_out_rows, w1bd, w2bd, pe_b, w1t, page_table=None):
    paged = page_table is not None
    n_src = len(src)
    kern = functools.partial(_compress_kernel, n_src=n_src, nrow=nrow, paged=paged)
    const = lambda a: pl.BlockSpec(a.shape, lambda *_, _n=a.ndim: (0,) * _n, pipeline_mode=pl.Buffered(1))
    omap = lambda b, k, *_: (b, k, 0)
    scratch = [pltpu.VMEM((2, 1, KV_DIM), _F32)]
    if paged:
        scratch.append(pltpu.VMEM((2 * KV_DIM // LANES, n_src * PAGE_SIZE, LANES), _F32))
    grid_spec = pltpu.PrefetchScalarGridSpec(
        num_scalar_prefetch=1 if paged else 0,
        grid=grid,
        in_specs=list(src_specs) + [const(w1bd), const(w2bd), const(pe_b), const(w1t)],
        out_specs=[pl.BlockSpec((1, nrow, KV_DIM), omap), pl.BlockSpec((1, nrow, KV_DIM), omap),
                   pl.BlockSpec((1, KV_DIM, nrow), lambda b, k, *_: (b, 0, k))],
        scratch_shapes=scratch,
    )
    return pl.pallas_call(
        kern,
        grid_spec=grid_spec,
        out_shape=[jax.ShapeDtypeStruct((n_batch, n_out_rows, KV_DIM), _BF16)] * 2
                  + [jax.ShapeDtypeStruct((n_batch, KV_DIM, n_out_rows), _BF16)],
        compiler_params=_params("parallel", "arbitrary"),
        name="compress_paged" if paged else "compress",
    )(*((page_table,) if paged else ()), *src, w1bd, w2bd, pe_b, w1t)


def _threshold_select(sc, n_sel, tri):
    bits = pltpu.bitcast(sc, jnp.int32)
    key = jnp.where(bits < 0, bits ^ jnp.int32(0x7FFFFFFF), bits)
    count_ge = lambda t: jnp.sum(jnp.where(key >= t, 1.0, 0.0), axis=0, keepdims=True)
    t0 = jnp.where(count_ge(jnp.int32(0)) >= n_sel, jnp.int32(0), jnp.int32(-2 ** 31))

    def body(b, t):
        cand = t | jnp.left_shift(jnp.int32(1), 30 - b)
        return jnp.where(count_ge(cand) >= n_sel, cand, t)

    t = lax.fori_loop(0, 31, body, t0)
    above = jnp.where(key > t, 1.0, 0.0)
    tie = jnp.where(key == t, 1.0, 0.0)
    need = n_sel - jnp.sum(above, axis=0, keepdims=True)
    tie_rank = _dot(tri, tie.astype(_BF16))
    return above + jnp.where(tie_rank <= need, tie, 0.0)


def _softmax_parts(parts):
    ms = [jnp.max(jnp.where(mk, s, NEG), axis=-1, keepdims=True) for s, mk in parts]
    m = functools.reduce(jnp.maximum, ms)
    ps = [jnp.where(mk, jnp.exp(s - m), 0.0) for s, mk in parts]
    l = functools.reduce(lambda a, b: a + b, [jnp.sum(p, axis=-1, keepdims=True) for p in ps])
    inv = 1.0 / jnp.maximum(l, 1e-30)
    return [p * inv for p in ps]


def _softmax_cols(parts):
    ms = [jnp.max(jnp.where(mk, s, NEG), axis=0, keepdims=True) for s, mk in parts]
    m = functools.reduce(jnp.maximum, ms)
    ps = [jnp.where(mk, jnp.exp(s - m), 0.0) for s, mk in parts]
    l = functools.reduce(lambda a, b: a + b, [jnp.sum(p, axis=0, keepdims=True) for p in ps])
    inv = 1.0 / jnp.maximum(l, 1e-30)
    return [p * inv for p in ps]


def _attn_prompt_kernel(q_ref, ng_ref, ks_ref, vst_ref, kw_ref, vwt_ref, kc_ref, vct_ref,
                        tzt_ref, tct_ref, ot_ref, gsel_ref, rsum_ref, rep_ref, tri_ref,
                        o_ref, qzt_ref, otacc_ref, gt_ref, nsel_ref, *, n_cmp_pad, seq_len):
    i = pl.program_id(1)
    qb = Q_BLOCK
    rows = HPG * qb
    lane_g = lax.broadcasted_iota(jnp.int32, (qb, KV_DIM), 1) // HEAD_DIM
    vrows = lambda g: pl.ds(g * HEAD_DIM, HEAD_DIM)

    for g in range(N_KV):
        qz = jnp.concatenate([jnp.where(lane_g == g, q_ref[0, :, r * KV_DIM:(r + 1) * KV_DIM].astype(_F32), 0.0)
                              for r in range(HPG)], axis=0)
        qzt_ref[g] = qz.T.astype(_BF16)

    gparts = _split3(jax.nn.sigmoid(ng_ref[0]))
    gcols = jnp.concatenate([sum(_dot(p, gsel_ref[r]) for p in gparts) for r in range(HPG)], axis=0)
    gt_ref[...] = gcols.T
    otacc_ref[...] = jnp.zeros_like(otacc_ref)

    def emit(branch, g, out_t):
        otacc_ref[g] += gt_ref[pl.ds(branch * N_KV + g, 1), :] * out_t

    jc = lax.broadcasted_iota(jnp.int32, (n_cmp_pad, rows), 0)
    lq = lax.broadcasted_iota(jnp.int32, (n_cmp_pad, rows), 1) % qb
    cmp_valid = (jc >= 1) & (CMP_STRIDE * jc + (CMP_LEN - CMP_STRIDE - 1) - lq <= i * qb)
    mm = lax.broadcasted_iota(jnp.int32, (n_cmp_pad, LANES), 1)
    jj = lax.broadcasted_iota(jnp.int32, (n_cmp_pad, LANES), 0)
    shift_t = jnp.where((mm < 16) & (jj - mm == 4 * i - 8), 1.0, 0.0).astype(_BF16)
    kc_aug = jnp.concatenate([kc_ref[0], shift_t, shift_t, shift_t], axis=1)
    groups = range(N_KV)
    s_c = [_dot(kc_aug, jnp.concatenate([qzt_ref[g], tct_ref[0, g], tct_ref[1, g], tct_ref[2, g]], axis=0))
           for g in groups]
    pn_c = [_softmax_cols([(s, cmp_valid)])[0] for s in s_c]
    for g in groups:
        emit(0, g, _dot(vct_ref[0, vrows(g), :], pn_c[g].astype(_BF16)))
    y_c = [sum(_dot(ot_ref[...], p) for p in _split3(pn_c[g])) for g in groups]
    imp_t = sum(sum(_dot(p, rsum_ref[g]) for p in _split3(y_c[g])) for g in groups)

    jrow = lax.broadcasted_iota(jnp.int32, imp_t.shape, 0)
    forced = (jrow == 0) | (jrow == i) | (jrow == i - 1)
    score = jnp.where(forced, jnp.inf, jnp.where(jrow <= i, imp_t, -jnp.inf))
    sel_t = _threshold_select(score, N_SELECT, tri_ref[...])

    h = jnp.minimum(jnp.maximum(i - 3, 0) // 2, (seq_len - NEAR_KEYS) // LANES)
    ns = pl.multiple_of(h * LANES, LANES)
    v = i - 2 * h
    sel16 = sel_t.astype(_BF16)
    for g in groups:
        neg = (_dot(sel16, rep_ref[g]) - 1.0) * MASK_BIG
        nsel_ref[0, g] = neg
        nsel_ref[1, g] = jnp.where(jrow < 2 * h, neg, -MASK_BIG)

    def add_block_mask(s, far, g, j0, n_blocks):
        return jnp.concatenate([s[jj * SEL_BLOCK:(jj + 1) * SEL_BLOCK, :] + nsel_ref[far, g, pl.ds(j0 + jj, 1), :]
                                for jj in range(n_blocks)], axis=0)

    cn = lax.broadcasted_iota(jnp.int32, (NEAR_KEYS, rows), 0)
    lqn = lax.broadcasted_iota(jnp.int32, (NEAR_KEYS, rows), 1) % qb
    causal = cn - lqn <= v * SEL_BLOCK
    near = pl.ds(ns, NEAR_KEYS)
    n_far = (ns + FAR_TILE - 1) // FAR_TILE

    k_near = ks_ref[0, near, :]
    s_n = [add_block_mask(_dot(k_near, qzt_ref[g]) + tzt_ref[v, g], 0, g, 2 * h, NEAR_KEYS // SEL_BLOCK)
           for g in groups]
    s_n = [jnp.where(causal, s, NEG) for s in s_n]
    m0 = [jnp.max(s, axis=0, keepdims=True) for s in s_n]
    p_n = [jnp.exp(s_n[g] - m0[g]) for g in groups]
    l0 = [jnp.sum(p, axis=0, keepdims=True) for p in p_n]
    acc0 = [_dot(vst_ref[0, vrows(g), near], p_n[g].astype(_BF16)) for g in groups]

    def far_body(tau, carry):
        m_old, l_old, acc_old = carry
        k0 = pl.multiple_of(tau * FAR_TILE, FAR_TILE)
        tile = pl.ds(k0, FAR_TILE)
        kt = ks_ref[0, tile, :]
        s_f = [add_block_mask(_dot(kt, qzt_ref[g]), 1, g, tau * (FAR_TILE // SEL_BLOCK), FAR_TILE // SEL_BLOCK)
               for g in groups]
        m_new = [jnp.maximum(m_old[g], jnp.max(s_f[g], axis=0, keepdims=True)) for g in groups]
        alpha = [jnp.exp(m_old[g] - m_new[g]) for g in groups]
        p_f = [jnp.exp(s_f[g] - m_new[g]) for g in groups]
        l_new = [alpha[g] * l_old[g] + jnp.sum(p_f[g], axis=0, keepdims=True) for g in groups]
        acc = [alpha[g] * acc_old[g] + _dot(vst_ref[0, vrows(g), tile], p_f[g].astype(_BF16)) for g in groups]
        return tuple(m_new), tuple(l_new), tuple(acc)

    _, l1, acc1 = lax.fori_loop(0, n_far, far_body, (tuple(m0), tuple(l0), tuple(acc0)))
    for g in groups:
        emit(1, g, acc1[g] * (1.0 / l1[g]))

    fs = pl.multiple_of((jnp.maximum(i - WINDOW // SEL_BLOCK, 0) // 2) * LANES, LANES)
    wfar = pl.ds(fs, WIN_FAR_KEYS)
    cf = lax.broadcasted_iota(jnp.int32, (WIN_FAR_KEYS, rows), 0)
    lqf = lax.broadcasted_iota(jnp.int32, (WIN_FAR_KEYS, rows), 1) % qb
    far_ok = (i * qb + lqf - fs - cf < WINDOW) & (fs + cf < ns)
    kw_near, kw_far = kw_ref[0, near, :], kw_ref[0, wfar, :]
    s_wn = [_dot(kw_near, qzt_ref[g]) + tzt_ref[v, g] for g in groups]
    s_wf = [_dot(kw_far, qzt_ref[g]) for g in groups]
    p_w = [_softmax_cols([(s_wn[g], causal), (s_wf[g], far_ok)]) for g in groups]
    for g in groups:
        emit(2, g, _dot(vwt_ref[0, vrows(g), near], p_w[g][0].astype(_BF16))
             + _dot(vwt_ref[0, vrows(g), wfar], p_w[g][1].astype(_BF16)))

    res = otacc_ref[...].reshape(N_KV * HEAD_DIM, rows).T
    for r in range(HPG):
        o_ref[0, :, r * KV_DIM:(r + 1) * KV_DIM] = res[r * qb:(r + 1) * qb, :].astype(_BF16)


def _attn_prompt(q2, ng, ks, vst, kw, vwt, kc, vct, tabs):
    bsz, t, _ = q2.shape
    nblk = t // SEL_BLOCK
    n_cmp_pad = kc.shape[1]
    rows = HPG * Q_BLOCK
    assert t % FAR_TILE == 0 and t >= NEAR_KEYS
    names = ["tzt", "tct", "ot", "gsel", "rsum", "rep", "tri"]
    full = lambda a: pl.BlockSpec(a.shape, lambda b, i, _n=a.ndim: (0,) * _n, pipeline_mode=pl.Buffered(1))
    per_b = lambda a: pl.BlockSpec((1,) + a.shape[1:], lambda b, i: (b, 0, 0))
    kern = functools.partial(_attn_prompt_kernel, n_cmp_pad=n_cmp_pad, seq_len=t)
    return pl.pallas_call(
        kern,
        grid=(bsz, nblk),
        in_specs=[pl.BlockSpec((1, Q_BLOCK, ATTN_DIM), lambda b, i: (b, i, 0)),
                  pl.BlockSpec((1, Q_BLOCK, LANES), lambda b, i: (b, i, 0)),
                  per_b(ks), per_b(vst), per_b(kw), per_b(vwt), per_b(kc), per_b(vct)]
                 + [full(tabs[n]) for n in names],
        out_specs=pl.BlockSpec((1, Q_BLOCK, ATTN_DIM), lambda b, i: (b, i, 0)),
        out_shape=jax.ShapeDtypeStruct((bsz, t, ATTN_DIM), _BF16),
        scratch_shapes=[pltpu.VMEM((N_KV, KV_DIM, rows), _BF16),
                        pltpu.VMEM((N_KV, HEAD_DIM, rows), _F32),
                        pltpu.VMEM((LANES, rows), _F32),
                        pltpu.VMEM((2, N_KV, SEL_BLOCK, rows), _F32)],
        compiler_params=_params("parallel", "arbitrary"),
        name="attn_prompt",
    )(q2, ng, ks, vst, kw, vwt, kc, vct, *[tabs[n] for n in names])


def _bias_table(rel, dist, head):
    nmax = max(int(dist.max()), 1) + 1
    bk = _rel_bucket_np(np.arange(nmax))
    rel_h = rel[:, head]
    out = jnp.broadcast_to(rel_h[0], np.broadcast_shapes(dist.shape, head.shape))
    dist = lax.optimization_barrier(jnp.asarray(dist, jnp.int32))
    for b in range(1, int(bk.max()) + 1):
        first = int(np.argmax(bk >= b))
        out = jnp.where(dist >= first, rel_h[b], out)
    return out


def _prompt_tables(rel_bias, t):
    nblk = t // SEL_BLOCK
    n_cmp_pad = t // CMP_STRIDE
    assert nblk <= SEL_BLOCK and n_cmp_pad % LANES == 0
    rows = HPG * Q_BLOCK
    rel = rel_bias.astype(_F32)
    r_idx = np.arange(rows) // Q_BLOCK
    q_idx = np.arange(rows) % Q_BLOCK
    head = np.arange(N_KV)[:, None] * HPG + r_idx[None, :]
    c31 = rel[REL_BUCKETS - 1][head]
    c = np.arange(NEAR_KEYS)
    dist = np.arange(NEAR_VARIANTS)[:, None, None] * SEL_BLOCK + q_idx[None, None, :] - c[None, :, None]
    tzt = _bias_table(rel, dist[:, None], head[None, :, None, :]) - c31[None, :, None, :]
    mmv = np.arange(16)
    dist_c = q_idx[None, :] - CMP_STRIDE * (mmv[:, None] - 8) - (CMP_LEN - CMP_STRIDE - 1)
    delta = _bias_table(rel, dist_c[None], head[:, None, :]) - c31[:, None, :]
    tct = jnp.stack(_split3(jnp.pad(delta, ((0, 0), (0, LANES - 16), (0, 0)))))
    n = np.arange(n_cmp_pad) - 1
    cs = n * CMP_STRIDE
    bs = np.arange(SEL_BLOCK) * SEL_BLOCK
    ov = np.clip(np.minimum(cs[None, :] + CMP_LEN, bs[:, None] + SEL_BLOCK) - np.maximum(cs[None, :], bs[:, None]),
                 0, CMP_LEN).astype(np.float32) / CMP_LEN
    ov[:, 0] = 0.0
    ov[nblk:, :] = 0.0
    gsel = np.zeros((HPG, LANES, LANES), np.float32)
    for j in range(3):
        for r in range(HPG):
            for g in range(N_KV):
                gsel[r, j * N_HEADS + r * N_KV + g, j * N_KV + g] = 1.0
    rsum = np.zeros((N_KV, rows, N_KV * Q_BLOCK), np.float32)
    for g in range(N_KV):
        rsum[g, np.arange(rows), g * Q_BLOCK + q_idx] = 1.0
    bf = lambda a: jnp.asarray(a, _BF16)
    return dict(tzt=tzt, tct=tct, ot=bf(ov), gsel=bf(gsel), rsum=bf(rsum),
                tri=bf(np.tril(np.ones((SEL_BLOCK, SEL_BLOCK), np.float32))),
                rep=bf(rsum.transpose(0, 2, 1)))


def _gate_expand():
    ex = np.zeros((LANES, 3 * ATTN_DIM), np.float32)
    for j in range(3):
        for r in range(HPG):
            for g in range(N_KV):
                col = j * ATTN_DIM + r * KV_DIM + g * HEAD_DIM
                ex[j * N_HEADS + r * N_KV + g, col:col + HEAD_DIM] = 1.0
    return jnp.asarray(ex, _BF16)


PAGES_PER_STEP = 32
SUB_PAGES = 32


def _attn_sample_kernel(pt_ref, q_ref, ng_ref, kc_ref, vc_ref, knew_ref, wnew_ref, wst_ref, *refs,
                        past_len, n_blk, nb_rows):
    del pt_ref
    page_refs = refs[:PAGES_PER_STEP]
    (bc_ref, blast_ref, c31_ref, bnew_ref, bwin_ref, ot_ref, e64_ref, ex_ref, tri_ref,
     o_ref, qall_ref, gate_ref, oacc_ref, acc_ref, m_ref, l_ref, selt_ref) = refs[PAGES_PER_STEP:]
    k = pl.program_id(1)
    nk = pl.num_programs(1)
    tq = q_ref.shape[1]
    rows = N_KV * HPG * tq
    lane_g = lax.broadcasted_iota(jnp.int32, (tq, KV_DIM), 1) // HEAD_DIM
    rq = lax.broadcasted_iota(jnp.int32, (rows, 1), 0) % tq
    sub_keys = SUB_PAGES * PAGE_SIZE

    def emit(branch, out):
        for g in range(N_KV):
            for r in range(HPG):
                col = r * KV_DIM
                row0 = (g * HPG + r) * tq
                gt = gate_ref[:, branch * ATTN_DIM + col:branch * ATTN_DIM + col + KV_DIM]
                oacc_ref[:, col:col + KV_DIM] += jnp.where(lane_g == g, gt * out[row0:row0 + tq, :], 0.0)

    def pad_rows(x, n):
        if n == x.shape[0]:
            return x
        return jnp.concatenate([x, jnp.zeros((n - x.shape[0], x.shape[1]), x.dtype)], axis=0)

    @pl.when(k == 0)
    def _():
        qf = q_ref[0].astype(_F32)
        pieces = []
        for g in range(N_KV):
            for r in range(HPG):
                pieces.append(jnp.where(lane_g == g, qf[:, r * KV_DIM:(r + 1) * KV_DIM], 0.0))
        qall = jnp.concatenate(pieces, axis=0).astype(_BF16)
        qall_ref[...] = qall
        gs = jax.nn.sigmoid(ng_ref[0])
        gate_ref[...] = sum(_dot(p, ex_ref[...]) for p in _split3(gs))
        oacc_ref[...] = jnp.zeros_like(oacc_ref)

        n_cmp_pad = kc_ref.shape[1]
        jc = lax.broadcasted_iota(jnp.int32, (rows, n_cmp_pad), 1)
        cmp_valid = (jc >= 1) & (CMP_STRIDE * jc + (CMP_LEN - CMP_STRIDE - 1) - rq <= past_len)
        s = _dot_nt(qall, kc_ref[0]) + bc_ref[...]
        (pn,) = _softmax_parts([(s, cmp_valid)])
        emit(0, _dot(pn.astype(_BF16), vc_ref[0]))
        imp_rows = []
        for g in range(N_KV):
            sg = sum(pn[(g * HPG + r) * tq:(g * HPG + r + 1) * tq, :] for r in range(HPG))
            imp_rows += [sg] * HPG
        imp = jnp.concatenate(imp_rows, axis=0)
        imp_t = sum(_dot_nt(ot_ref[...], p) for p in _split3(imp))

        jrow = lax.broadcasted_iota(jnp.int32, imp_t.shape, 0)
        tpos = past_len + lax.broadcasted_iota(jnp.int32, imp_t.shape, 1) % tq
        cur = tpos // SEL_BLOCK
        forced = (jrow == 0) | (jrow == cur) | (jrow == cur - 1)
        valid = jrow * SEL_BLOCK <= tpos
        score = jnp.where(forced, jnp.inf, jnp.where(valid, imp_t, -jnp.inf))
        selt_ref[...] = pad_rows(_threshold_select(score, min(N_SELECT, n_blk), tri_ref[...]), selt_ref.shape[0])

        kn = pad_rows(knew_ref[0, :, :KV_DIM], LANES).astype(_BF16)
        vn = pad_rows(knew_ref[0, :, KV_DIM:], LANES).astype(_BF16)
        cn = lax.broadcasted_iota(jnp.int32, (rows, LANES), 1)
        mk = (cn <= rq) & (cn < tq)
        s = _dot_nt(qall, kn) + bnew_ref[...]
        m = jnp.max(jnp.where(mk, s, NEG), axis=-1, keepdims=True)
        p = jnp.where(mk, jnp.exp(s - m), 0.0)
        m_ref[...] = jnp.broadcast_to(m, m_ref.shape)
        l_ref[...] = jnp.broadcast_to(jnp.sum(p, axis=-1, keepdims=True), l_ref.shape)
        acc_ref[...] = _dot(p.astype(_BF16), vn)

    qall = qall_ref[...]
    blk_per_step = PAGES_PER_STEP * PAGE_SIZE // SEL_BLOCK
    j0 = pl.multiple_of(k * blk_per_step, blk_per_step)
    sel_step = selt_ref[pl.ds(j0, LANES), :].T.astype(_BF16)
    c31 = jnp.concatenate([c31_ref[...]] * (sub_keys // LANES), axis=-1)
    n_sub = PAGES_PER_STEP // SUB_PAGES
    for st in range(n_sub):
        pages = page_refs[st * SUB_PAGES:(st + 1) * SUB_PAGES]
        kt = jnp.concatenate([r[0, :KV_DIM, :] for r in pages], axis=1).astype(_BF16)
        vt = jnp.concatenate([r[0, KV_DIM:, :] for r in pages], axis=1).astype(_BF16)
        mk = _dot(sel_step, e64_ref[:, st * sub_keys:(st + 1) * sub_keys]) > 0.5
        if st == n_sub - 1:
            bias = jnp.where(k == nk - 1, blast_ref[...], c31)
        else:
            bias = c31
        s = jnp.where(mk, _dot(qall, kt) + bias, NEG)
        m_old = m_ref[:, :1]
        m_new = jnp.maximum(m_old, jnp.max(s, axis=-1, keepdims=True))
        alpha = jnp.exp(m_old - m_new)
        p = jnp.exp(s - m_new)
        l_ref[...] = jnp.broadcast_to(alpha * l_ref[:, :1] + jnp.sum(p, axis=-1, keepdims=True), l_ref.shape)
        m_ref[...] = jnp.broadcast_to(m_new, m_ref.shape)
        acc_ref[...] = alpha * acc_ref[...] + _dot_nt(p.astype(_BF16), vt)

    @pl.when(k == nk - 1)
    def _():
        emit(1, acc_ref[...] * (1.0 / l_ref[:, :1]))
        lw = wst_ref.shape[2]
        kw = wst_ref[0, :KV_DIM, :].astype(_BF16)
        vw = wst_ref[0, KV_DIM:, :].astype(_BF16)
        kn = pad_rows(wnew_ref[0, :, :KV_DIM], LANES).astype(_BF16)
        vn = pad_rows(wnew_ref[0, :, KV_DIM:], LANES).astype(_BF16)
        cw = lax.broadcasted_iota(jnp.int32, (rows, lw), 1)
        dw = lw + rq - cw
        cn = lax.broadcasted_iota(jnp.int32, (rows, LANES), 1)
        pw, pnw = _softmax_parts([(_dot(qall, kw) + bwin_ref[...], (dw >= 0) & (dw < WINDOW)),
                                  (_dot_nt(qall, kn) + bnew_ref[...], (cn <= rq) & (cn < tq))])
        emit(2, _dot_nt(pw.astype(_BF16), vw) + _dot(pnw.astype(_BF16), vn))
        o_ref[0] = oacc_ref[...].astype(_BF16)


def _page_specs():
    return [pl.BlockSpec((1, 2 * KV_DIM, PAGE_SIZE),
                         lambda b, k, pt, _p=p: (pt[b, k * PAGES_PER_STEP + _p], 0, 0))
            for p in range(PAGES_PER_STEP)]


def _transposed_rows(a):
    n, rows = a.shape[:2]
    return jnp.transpose(a, (0, 2, 3, 4, 1)).reshape(n, 2 * KV_DIM, rows)


def _attn_sample(q2, ng, kc, vc, kvs_new, kvw_new, win_state, slc_pages, page_table, tabs, past_len):
    bsz, tq, _ = q2.shape
    n_pages = page_table.shape[1]
    assert n_pages % PAGES_PER_STEP == 0 and past_len == n_pages * PAGE_SIZE and past_len % SEL_BLOCK == 0
    n_steps = n_pages // PAGES_PER_STEP
    rows = N_KV * HPG * tq
    assert rows == LANES
    n_blk = -(-(past_len + tq) // SEL_BLOCK)
    nb_rows = tabs["ot"].shape[0]
    blk_per_step = PAGES_PER_STEP * PAGE_SIZE // SEL_BLOCK
    selt_rows = max((n_steps - 1) * blk_per_step + LANES, nb_rows)
    full = lambda a: pl.BlockSpec(a.shape, lambda b, k, pt, _n=a.ndim: (0,) * _n, pipeline_mode=pl.Buffered(1))
    per_b = lambda a: pl.BlockSpec((1,) + a.shape[1:], lambda b, k, pt: (b, 0, 0))
    page_specs = _page_specs()
    names = ["bc", "blast", "c31", "bnew", "bwin", "ot", "e64", "ex", "tri"]
    kern = functools.partial(_attn_sample_kernel, past_len=past_len, n_blk=n_blk, nb_rows=nb_rows)
    grid_spec = pltpu.PrefetchScalarGridSpec(
        num_scalar_prefetch=1,
        grid=(bsz, n_steps),
        in_specs=[per_b(q2), per_b(ng), per_b(kc), per_b(vc), per_b(kvs_new), per_b(kvw_new), per_b(win_state)]
                 + page_specs + [full(tabs[n]) for n in names],
        out_specs=pl.BlockSpec((1, tq, ATTN_DIM), lambda b, k, pt: (b, 0, 0)),
        scratch_shapes=[pltpu.VMEM((rows, KV_DIM), _BF16),
                        pltpu.VMEM((tq, 3 * ATTN_DIM), _F32),
                        pltpu.VMEM((tq, ATTN_DIM), _F32),
                        pltpu.VMEM((rows, KV_DIM), _F32),
                        pltpu.VMEM((rows, LANES), _F32),
                        pltpu.VMEM((rows, LANES), _F32),
                        pltpu.VMEM((selt_rows, rows), _F32)],
    )
    return pl.pallas_call(
        kern,
        grid_spec=grid_spec,
        out_shape=jax.ShapeDtypeStruct((bsz, tq, ATTN_DIM), _BF16),
        compiler_params=_params("parallel", "arbitrary"),
        name="attn_sample",
    )(page_table, q2, ng, kc, vc, kvs_new, kvw_new, win_state, *([slc_pages] * PAGES_PER_STEP),
      *[tabs[n] for n in names])


def _sample_tables(rel_bias, past_len, tq, lw, n_cmp_pad):
    rows = N_KV * HPG * tq
    rel = rel_bias.astype(_F32)
    ridx = np.arange(rows)
    head = ridx // tq
    qi = ridx % tq
    tpos = past_len + qi

    bias_of = lambda dist: _bias_table(rel, dist, head[:, None])

    jc = np.arange(n_cmp_pad)
    bc = bias_of(tpos[:, None] - (CMP_STRIDE * jc[None, :] + CMP_LEN - CMP_STRIDE - 1))
    sub_keys = SUB_PAGES * PAGE_SIZE
    blast = bias_of(tpos[:, None] - (past_len - sub_keys + np.arange(sub_keys))[None, :])
    bnew = bias_of(qi[:, None] - np.arange(LANES)[None, :])
    bwin = bias_of(lw + qi[:, None] - np.arange(lw)[None, :])
    c31 = jnp.broadcast_to(rel[REL_BUCKETS - 1][head][:, None], (rows, LANES))
    n_blk = -(-(past_len + tq) // SEL_BLOCK)
    nb_rows = -(-n_blk // LANES) * LANES
    n = jc - 1
    cs = n * CMP_STRIDE
    bs = np.arange(nb_rows) * SEL_BLOCK
    ov = np.clip(np.minimum(cs[None, :] + CMP_LEN, bs[:, None] + SEL_BLOCK) - np.maximum(cs[None, :], bs[:, None]),
                 0, CMP_LEN).astype(np.float32) / CMP_LEN
    ov[:, 0] = 0.0
    ov[n_blk:, :] = 0.0
    step_keys = PAGES_PER_STEP * PAGE_SIZE
    e64 = (np.arange(step_keys)[None, :] // SEL_BLOCK == np.arange(LANES)[:, None]).astype(np.float32)
    tri = np.tril(np.ones((nb_rows, nb_rows), np.float32))
    return dict(bc=bc, blast=blast, c31=c31, bnew=bnew, bwin=bwin, ot=jnp.asarray(ov, _BF16),
                e64=jnp.asarray(e64, _BF16), ex=_gate_expand(), tri=jnp.asarray(tri, _BF16))


def _layer_weights(w_in, phi_pe, phi_w1, phi_w2, w_attn_out, conv_w, w_conv_out, w_o, w_up, w_down):
    d = w_in.shape[0]
    o_q, o_kc, o_ks, o_kw = 0, ATTN_DIM, ATTN_DIM + 2 * KV_DIM, ATTN_DIM + 4 * KV_DIM
    o_ng = ATTN_DIM + 6 * KV_DIM
    o_glu = o_ng + 3 * N_HEADS
    o_mg = o_glu + 2 * CONV_DIM
    bf = lambda a: a.astype(_BF16)
    wq = w_in[:, o_q:o_kc].reshape(d, N_KV, HPG, HEAD_DIM).transpose(0, 2, 1, 3).reshape(d, ATTN_DIM)
    wng = w_in[:, o_ng:o_glu].reshape(d, N_KV, HPG, 3).transpose(0, 3, 2, 1).reshape(d, 3 * N_HEADS)
    wng = jnp.pad(wng, ((0, 0), (0, LANES - 3 * N_HEADS)))
    w5 = phi_w1.reshape(2, CMP_R, CMP_STRIDE, HEAD_DIM, PHI_HIDDEN)
    eye = jnp.eye(N_KV, dtype=_F32)
    w1bd = jnp.einsum("crsde,gh->csgdrhe", w5, eye).reshape(2, CMP_STRIDE, KV_DIM, CMP_R * KV_DIM)
    w2bd = jnp.einsum("che,gk->cghke", phi_w2, eye).reshape(2, KV_DIM, KV_DIM)
    rep = LANES // PHI_HIDDEN
    pe_b = jnp.broadcast_to(phi_pe.reshape(2, CMP_LEN * HEAD_DIM, 1), (2, CMP_LEN * HEAD_DIM, LANES))
    w1t = jnp.tile(phi_w1, (1, 1, rep))
    wao = w_attn_out.reshape(N_KV, HPG, HEAD_DIM, d).transpose(1, 0, 2, 3).reshape(ATTN_DIM, d)
    half = PROJ_TILE // 2
    wga = w_in[:, o_glu:o_glu + CONV_DIM].reshape(d, CONV_DIM // half, half)
    wgb = w_in[:, o_glu + CONV_DIM:o_mg].reshape(d, CONV_DIM // half, half)
    wglu = jnp.concatenate([wga, wgb], axis=2).reshape(d, 2 * CONV_DIM)
    w_all = jnp.concatenate([wq, w_in[:, o_kc:o_ng], wglu, w_in[:, o_mg:]], axis=1)
    return dict(
        w_all=bf(w_all), wng=bf(wng), w1bd=bf(w1bd), w2bd=bf(w2bd), pe_b=pe_b, w1t=w1t,
        wao=bf(wao), wco=bf(w_conv_out), wo=bf(w_o), wup=bf(w_up), wdown=bf(w_down),
        conv_w=jnp.pad(conv_w, ((0, HALO - CONV_K), (0, 0))))


def _finish(x2, attn2, conv, mg, w, g_post_mix, g_pre_ffn, g_post_ffn, tm):
    mixed = _mix(attn2, conv, mg, w["wao"], w["wco"], tm, 512)
    x1 = _oproj(mixed, x2, w["wo"], g_post_mix, tm)
    return _ffn(x1, g_pre_ffn, g_post_ffn, w["wup"], w["wdown"], tm, 512)


def kernel(x_prompt, x_sample, cache_kv_cmp, cache_kv_slc, state_kv_win, state_conv, page_table, w_in, phi_pe,
           phi_w1, phi_w2, rel_bias, w_attn_out, conv_w, conv_b, conv_ln_g, conv_ln_b, w_conv_out, w_o, w_up,
           w_down, g_pre_mix, g_post_mix, g_pre_ffn, g_post_ffn):
    depth = w_in.shape[0]
    bp, tp, d = x_prompt.shape
    bs, ts, _ = x_sample.shape
    n_pages = page_table.shape[1]
    past_len = n_pages * PAGE_SIZE
    lw = state_kv_win.shape[2]
    chunks_per_page = PAGE_SIZE // CMP_STRIDE
    chunk_cols = CMP_STRIDE * 2 * KV_DIM
    assert ts < CMP_STRIDE and tp % CMP_STRIDE == 0 and lw == WINDOW and tp >= WINDOW

    tabs_p = _prompt_tables(rel_bias, tp)
    tabs_s = _sample_tables(rel_bias, past_len, ts, lw, past_len // CMP_STRIDE)
    yp, ys = x_prompt.reshape(bp * tp, d), x_sample.reshape(bs * ts, d)
    outs = [[] for _ in range(8)]
    row = lambda a: a.reshape(1, -1)
    kv5 = lambda a, b, t: a.reshape(b, t, 2, N_KV, HEAD_DIM)
    kv5_t = lambda a: jnp.transpose(a.reshape(a.shape[0], 2, N_KV, HEAD_DIM, a.shape[2]), (0, 4, 1, 2, 3))
    for l in range(depth):
        w = _layer_weights(w_in[l], phi_pe[l], phi_w1[l], phi_w2[l], w_attn_out[l], conv_w[l], w_conv_out[l],
                           w_o[l], w_up[l], w_down[l])
        gpm, gqm, gpf, gqf = row(g_pre_mix[l]), row(g_post_mix[l]), row(g_pre_ffn[l]), row(g_post_ffn[l])
        cargs = (w["conv_w"], row(conv_b[l]), row(conv_ln_g[l]), row(conv_ln_b[l]))
        cmp_w = (w["w1bd"], w["w2bd"], w["pe_b"], w["w1t"])

        tm = 512
        q2, (kvc_t, kvc16), (kvs_t, ks16, vst16), (kvw_t, kw16, vwt16), ng, u, mg = _proj(
            yp, gpm, w["w_all"], w["wng"], tm, tp)
        n_chunk = tp // CMP_STRIDE
        kc, _, vct = _compress([kvc16.reshape(bp, n_chunk, chunk_cols)],
                               [pl.BlockSpec((1, n_chunk, chunk_cols), lambda b, k: (b, 0, 0))],
                               (bp, 1), n_chunk, bp, n_chunk, *cmp_w)
        attn2 = _attn_prompt(q2.reshape(bp, tp, ATTN_DIM), ng.reshape(bp, tp, LANES),
                             ks16.reshape(bp, tp, KV_DIM), vst16, kw16.reshape(bp, tp, KV_DIM), vwt16,
                             kc, vct, tabs_p)
        u3 = u.reshape(bp, tp, CONV_DIM)
        conv = _conv(u3, u3, *cargs, 256, True)
        yp = _finish(yp, attn2.reshape(bp * tp, ATTN_DIM), conv.reshape(bp * tp, CONV_DIM), mg, w, gqm, gpf, gqf, tm)
        outs[0].append(kv5_t(kvc_t))
        outs[2].append(kv5_t(kvs_t))
        outs[4].append(kv5_t(kvw_t[:, :, tp - WINDOW:]))
        outs[6].append(u3[:, tp - (CONV_K - 1):])

        tm = bs * ts
        q2, (kvc,), (kvs,), (kvw,), ng, u, mg = _proj(ys, gpm, w["w_all"], w["wng"], tm)
        kc, vc, _ = _compress([_transposed_rows(cache_kv_cmp[l])] * PAGES_PER_STEP, _page_specs(),
                           (bs, n_pages // PAGES_PER_STEP), PAGES_PER_STEP * chunks_per_page, bs,
                           past_len // CMP_STRIDE, *cmp_w, page_table=page_table)
        attn2 = _attn_sample(q2.reshape(bs, ts, ATTN_DIM), ng.reshape(bs, ts, LANES), kc, vc,
                             kvs.reshape(bs, ts, 2 * KV_DIM), kvw.reshape(bs, ts, 2 * KV_DIM),
                             _transposed_rows(state_kv_win[l]), _transposed_rows(cache_kv_slc[l]),
                             page_table, tabs_s, past_len)
        u3 = u.reshape(bs, ts, CONV_DIM)
        hist = jnp.pad(state_conv[l], ((0, 0), (HALO - (CONV_K - 1), 0), (0, 0)))
        conv = _conv(u3, hist, *cargs, ts, False)
        ys = _finish(ys, attn2.reshape(bs * ts, ATTN_DIM), conv.reshape(bs * ts, CONV_DIM), mg, w, gqm, gpf, gqf, tm)
        outs[1].append(kv5(kvc, bs, ts))
        outs[3].append(kv5(kvs, bs, ts))
        win_rows = jnp.concatenate([state_kv_win[l], kv5(kvw, bs, ts)], axis=1)
        outs[5].append(win_rows[:, win_rows.shape[1] - min(WINDOW, win_rows.shape[1]):])
        up = jnp.concatenate([state_conv[l], u3], axis=1)
        outs[7].append(up[:, up.shape[1] - (CONV_K - 1):])

    stack = lambda i: jnp.stack(outs[i])
    return (yp.reshape(bp, tp, d), ys.reshape(bs, ts, d), stack(0), stack(1), stack(2), stack(3),
            stack(4), stack(5), stack(6), stack(7))
```

```python
import functools
import math

import jax
import jax.numpy as jnp
import numpy as np
from jax import lax
from jax.experimental import pallas as pl
from jax.experimental.pallas import tpu as pltpu

D_MODEL = 2048
N_HEADS = 16
HEAD_DIM = 64
N_KV = 4
HPG = N_HEADS // N_KV
ATTN_DIM = N_HEADS * HEAD_DIM
KV_DIM = N_KV * HEAD_DIM
CMP_LEN = 32
CMP_STRIDE = 16
CMP_R = CMP_LEN // CMP_STRIDE
PHI_HIDDEN = HEAD_DIM
SEL_BLOCK = 64
N_SELECT = 16
WINDOW = 512
Q_BLOCK = 64
CONV_DIM = D_MODEL // 2
CONV_K = 31
D_FF = 4 * D_MODEL
REL_BUCKETS = 32
REL_MAX_DIST = 128
EPS = 1e-6
NEG = -1e30
PAGE_SIZE = 128

LANES = 128
SUBLANES = 8
VMEM_LIMIT_BYTES = 56 * 1024 * 1024

NEAR_KEYS = 384
NEAR_VARIANTS = 6
FAR_TILE = 512
WIN_FAR_KEYS = 384
MASK_BIG = 1e30
ROW_CHUNK = 32
HALO = 32

_F32 = jnp.float32
_BF16 = jnp.bfloat16


def _params(*sem):
    return pltpu.CompilerParams(dimension_semantics=sem, vmem_limit_bytes=VMEM_LIMIT_BYTES)


def _dot(a, b):
    return jnp.dot(a, b, preferred_element_type=_F32)


def _dot_nt(a, b):
    return lax.dot_general(a, b, (((1,), (1,)), ((), ())), preferred_element_type=_F32)


def _split3(x):
    hi = x.astype(_BF16)
    r1 = x - hi.astype(_F32)
    mid = r1.astype(_BF16)
    lo = (r1 - mid.astype(_F32)).astype(_BF16)
    return hi, mid, lo


def _rms(x, g):
    return x * lax.rsqrt(jnp.mean(x * x, axis=-1, keepdims=True) + EPS) * g


def _rel_bucket_np(dist):
    n = np.maximum(dist, 0)
    exact = REL_BUCKETS // 2
    logb = exact + (np.log(np.maximum(n, 1).astype(np.float32) / np.float32(exact))
                    / np.float32(math.log(REL_MAX_DIST / exact)) * (REL_BUCKETS - exact)).astype(np.int32)
    return np.where(n < exact, n, np.minimum(logb, REL_BUCKETS - 1)).astype(np.int32)


PROJ_TILE = 512
PROJ_SEGMENTS = (("q", 0, 2), ("kvc", 2, 1), ("kvs", 3, 1), ("kvw", 4, 1), ("glu", 5, 4), ("mg", 9, 8))
PROJ_TILES = 17


def _proj_kernel(x_ref, g_ref, w_ref, wng_ref, *refs, transposed_v):
    h_ref = refs[-1]
    if transposed_v:
        (q_ref, kvc_ref, kvc16_ref, kvs_ref, ks16_ref, vst_ref, kvw_ref, kw16_ref, vwt_ref,
         ng_ref, u_ref, mg_ref) = refs[:-1]
    else:
        q_ref, kvc_ref, kvs_ref, kvw_ref, ng_ref, u_ref, mg_ref = refs[:-1]
    j = pl.program_id(1)
    seg = {name: (lo, lo + n) for name, lo, n in PROJ_SEGMENTS}
    inside = lambda name: (j >= seg[name][0]) & (j < seg[name][1])

    @pl.when(j == 0)
    def _():
        h_ref[...] = _rms(x_ref[...], g_ref[...]).astype(_BF16)
        ng_ref[...] = _dot(h_ref[...], wng_ref[...])

    tm = h_ref.shape[0]
    n_split = 2 if tm % (2 * LANES) == 0 else 1

    def segment(name, epilogue):
        @pl.when(inside(name))
        def _():
            for part in range(n_split):
                rs = slice(part * tm // n_split, (part + 1) * tm // n_split)
                epilogue(rs, _dot(h_ref[rs, :], w_ref[...]))

    def q_out(rs, acc):
        q_ref[rs, :] = (acc * (HEAD_DIM ** -0.5)).astype(_BF16)

    def kvc_out(rs, acc):
        if transposed_v:
            kvc_ref[0, :, rs] = acc.T
            kvc16_ref[rs, :] = acc.astype(_BF16)
        else:
            kvc_ref[rs, :] = acc

    def kv_out(f32_ref, k16_ref, vt_ref):
        def out(rs, acc):
            if transposed_v:
                acc_t = acc.T
                f32_ref[0, :, rs] = acc_t
                k16_ref[rs, :] = acc[:, :KV_DIM].astype(_BF16)
                vt_ref[0, :, rs] = acc_t[KV_DIM:, :].astype(_BF16)
            else:
                f32_ref[rs, :] = acc
        return out

    def glu_out(rs, acc):
        half = PROJ_TILE // 2
        u_ref[rs, :] = acc[:, :half] * jax.nn.sigmoid(acc[:, half:])

    def mg_out(rs, acc):
        mg_ref[rs, :] = jax.nn.sigmoid(acc).astype(_BF16)

    segment("q", q_out)
    segment("kvc", kvc_out)
    segment("kvs", kv_out(kvs_ref, ks16_ref if transposed_v else None, vst_ref if transposed_v else None))
    segment("kvw", kv_out(kvw_ref, kw16_ref if transposed_v else None, vwt_ref if transposed_v else None))
    segment("glu", glu_out)
    segment("mg", mg_out)


def _proj(x, g, w_all, wng, tm, seq_len=None):
    m, d = x.shape
    tn = PROJ_TILE
    assert w_all.shape == (d, PROJ_TILES * tn)
    transposed_v = seq_len is not None
    seg = {name: (lo, n) for name, lo, n in PROJ_SEGMENTS}

    def spec(name, width=tn):
        lo, n = seg[name]
        return pl.BlockSpec((tm, width), lambda i, j: (i, jnp.clip(j - lo, 0, n - 1)))

    f32 = lambda n: jax.ShapeDtypeStruct((m, n), _F32)
    b16 = lambda n: jax.ShapeDtypeStruct((m, n), _BF16)
    if transposed_v:
        assert seq_len % tm == 0
        spb = seq_len // tm
        nb = m // seq_len
        t_spec = lambda rows: pl.BlockSpec((1, rows, tm), lambda i, j: (i // spb, 0, i % spb))
        row_spec = lambda width: pl.BlockSpec((tm, width), lambda i, j: (i, 0))
        f32_t = jax.ShapeDtypeStruct((nb, tn, seq_len), _F32)
        kc_specs, kc_shapes = [t_spec(tn), row_spec(tn)], [f32_t, b16(tn)]
        kv_specs = [t_spec(tn), row_spec(KV_DIM), t_spec(KV_DIM)]
        kv_shapes = [f32_t, b16(KV_DIM), jax.ShapeDtypeStruct((nb, KV_DIM, seq_len), _BF16)]
        kw_specs, kw_shapes = kv_specs, kv_shapes
    else:
        kc_specs, kc_shapes = [spec("kvc")], [f32(tn)]
        kv_specs, kv_shapes = [spec("kvs")], [f32(tn)]
        kw_specs, kw_shapes = [spec("kvw")], [f32(tn)]
    out_specs = ([spec("q")] + kc_specs + kv_specs + kw_specs
                 + [pl.BlockSpec((tm, LANES), lambda i, j: (i, 0)), spec("glu", tn // 2), spec("mg")])
    out_shape = ([b16(seg["q"][1] * tn)] + kc_shapes + kv_shapes + kw_shapes
                 + [f32(LANES), f32(seg["glu"][1] * tn // 2), b16(seg["mg"][1] * tn)])
    outs = pl.pallas_call(
        functools.partial(_proj_kernel, transposed_v=transposed_v),
        grid=(m // tm, PROJ_TILES),
        in_specs=[pl.BlockSpec((tm, d), lambda i, j: (i, 0)),
                  pl.BlockSpec((1, d), lambda i, j: (0, 0)),
                  pl.BlockSpec((d, tn), lambda i, j: (0, j)),
                  pl.BlockSpec((d, LANES), lambda i, j: (0, 0))],
        out_specs=out_specs,
        out_shape=out_shape,
        scratch_shapes=[pltpu.VMEM((tm, d), _BF16)],
        compiler_params=_params("parallel", "arbitrary"),
        name="proj",
    )(x, g, w_all, wng)
    nkc, nkv = len(kc_specs), len(kv_specs)
    q2, kvc = outs[0], tuple(outs[1:1 + nkc])
    kvs, kvw = tuple(outs[1 + nkc:1 + nkc + nkv]), tuple(outs[1 + nkc + nkv:1 + nkc + 2 * nkv])
    ng, u, mg = outs[1 + nkc + 2 * nkv:]
    return q2, kvc, kvs, kvw, ng, u, mg


def _mix_kernel(a_ref, c_ref, ga_ref, gc_ref, wa_ref, wc_ref, o_ref):
    ya = _dot(a_ref[...], wa_ref[...])
    yc = _dot(c_ref[...], wc_ref[...])
    o_ref[...] = (ga_ref[...].astype(_F32) * ya + gc_ref[...].astype(_F32) * yc).astype(_BF16)


def _mix(attn, conv, mg, wao, wco, tm, tn):
    m, ka = attn.shape
    n = wao.shape[1]
    nb = n // tn
    return pl.pallas_call(
        _mix_kernel,
        grid=(m // tm, nb),
        in_specs=[pl.BlockSpec((tm, ka), lambda i, j: (i, 0)),
                  pl.BlockSpec((tm, conv.shape[1]), lambda i, j: (i, 0)),
                  pl.BlockSpec((tm, tn), lambda i, j: (i, j)),
                  pl.BlockSpec((tm, tn), lambda i, j: (i, j + nb)),
                  pl.BlockSpec((ka, tn), lambda i, j: (0, j)),
                  pl.BlockSpec((conv.shape[1], tn), lambda i, j: (0, j))],
        out_specs=pl.BlockSpec((tm, tn), lambda i, j: (i, j)),
        out_shape=jax.ShapeDtypeStruct((m, n), _BF16),
        compiler_params=_params("parallel", "arbitrary"),
        name="mix",
    )(attn, conv, mg, mg, wao, wco)


def _oproj_kernel(mx_ref, x_ref, w_ref, g_ref, o_ref):
    y = _dot(mx_ref[...], w_ref[...])
    o_ref[...] = x_ref[...] + _rms(y, g_ref[...])


def _oproj(mixed, x, wo, g, tm):
    m, d = x.shape
    return pl.pallas_call(
        _oproj_kernel,
        grid=(m // tm,),
        in_specs=[pl.BlockSpec((tm, d), lambda i: (i, 0)),
                  pl.BlockSpec((tm, d), lambda i: (i, 0)),
                  pl.BlockSpec((d, d), lambda i: (0, 0)),
                  pl.BlockSpec((1, d), lambda i: (0, 0))],
        out_specs=pl.BlockSpec((tm, d), lambda i: (i, 0)),
        out_shape=jax.ShapeDtypeStruct((m, d), _F32),
        compiler_params=_params("parallel"),
        name="oproj",
    )(mixed, x, wo, g)


def _ffn_kernel(x_ref, gpre_ref, gpost_ref, wu_ref, wd_ref, o_ref, h_ref, acc_ref):
    j = pl.program_id(1)

    @pl.when(j == 0)
    def _():
        h_ref[...] = _rms(x_ref[...], gpre_ref[...]).astype(_BF16)
        acc_ref[...] = jnp.zeros_like(acc_ref)

    tm = h_ref.shape[0]
    n_split = 2 if tm % (2 * LANES) == 0 else 1
    for part in range(n_split):
        rs = slice(part * tm // n_split, (part + 1) * tm // n_split)
        a = jnp.maximum(_dot(h_ref[rs, :], wu_ref[...]), 0.0)
        acc_ref[rs, :] += _dot((a * a).astype(_BF16), wd_ref[...])

    @pl.when(j == pl.num_programs(1) - 1)
    def _():
        o_ref[...] = x_ref[...] + _rms(acc_ref[...], gpost_ref[...])


def _ffn(x, gpre, gpost, wu, wd, tm, tf):
    m, d = x.shape
    f = wu.shape[1]
    return pl.pallas_call(
        _ffn_kernel,
        grid=(m // tm, f // tf),
        in_specs=[pl.BlockSpec((tm, d), lambda i, j: (i, 0)),
                  pl.BlockSpec((1, d), lambda i, j: (0, 0)),
                  pl.BlockSpec((1, d), lambda i, j: (0, 0)),
                  pl.BlockSpec((d, tf), lambda i, j: (0, j)),
                  pl.BlockSpec((tf, d), lambda i, j: (j, 0))],
        out_specs=pl.BlockSpec((tm, d), lambda i, j: (i, 0)),
        out_shape=jax.ShapeDtypeStruct((m, d), _F32),
        scratch_shapes=[pltpu.VMEM((tm, d), _BF16), pltpu.VMEM((tm, d), _F32)],
        compiler_params=_params("parallel", "arbitrary"),
        name="ffn",
    )(x, gpre, gpost, wu, wd)


def _conv_kernel(u_ref, halo_ref, w_ref, b_ref, lg_ref, lb_ref, o_ref, win_ref, *, tt, zero_first):
    c = u_ref.shape[-1]
    halo = halo_ref[0]
    if zero_first:
        halo = jnp.where(pl.program_id(1) == 0, 0.0, halo)
    win_ref[0, 0:HALO, :] = halo
    win_ref[0, HALO:HALO + tt, :] = u_ref[0]
    span = HALO + tt - SUBLANES
    for s in range(1, SUBLANES):
        win_ref[s, 0:span, :] = win_ref[0, s:s + span, :]
    rc = min(ROW_CHUNK, tt)
    off = HALO - (CONV_K - 1)
    for ch in range(tt // rc):
        acc = jnp.zeros((rc, c), _F32) + b_ref[...]
        for k in range(CONV_K):
            s = (off + k) % SUBLANES
            row = ch * rc + off + k - s
            acc = acc + w_ref[k:k + 1, :] * win_ref[s, row:row + rc, :]
        mu = jnp.mean(acc, axis=-1, keepdims=True)
        xc = acc - mu
        var = jnp.mean(xc * xc, axis=-1, keepdims=True)
        y = xc * lax.rsqrt(var + EPS) * lg_ref[...] + lb_ref[...]
        o_ref[0, ch * rc:(ch + 1) * rc, :] = (y * jax.nn.sigmoid(y)).astype(_BF16)


def _conv(u, halo_src, w, b, lg, lb, tt, zero_first):
    bsz, t, c = u.shape
    nhb = tt // HALO
    if zero_first:
        halo_map = lambda bi, ti: (bi, jnp.maximum(ti * nhb - 1, 0), 0)
    else:
        halo_map = lambda bi, ti: (bi, 0, 0)
    kern = functools.partial(_conv_kernel, tt=tt, zero_first=zero_first)
    return pl.pallas_call(
        kern,
        grid=(bsz, t // tt),
        in_specs=[pl.BlockSpec((1, tt, c), lambda bi, ti: (bi, ti, 0)),
                  pl.BlockSpec((1, HALO, c), halo_map),
                  pl.BlockSpec((HALO, c), lambda bi, ti: (0, 0)),
                  pl.BlockSpec((1, c), lambda bi, ti: (0, 0)),
                  pl.BlockSpec((1, c), lambda bi, ti: (0, 0)),
                  pl.BlockSpec((1, c), lambda bi, ti: (0, 0))],
        out_specs=pl.BlockSpec((1, tt, c), lambda bi, ti: (bi, ti, 0)),
        out_shape=jax.ShapeDtypeStruct((bsz, t, c), _BF16),
        scratch_shapes=[pltpu.VMEM((SUBLANES, HALO + tt, c), _F32)],
        compiler_params=_params("parallel", "arbitrary"),
        name="conv",
    )(u, halo_src, w, b, lg, lb)


def _compress_rows(chunk_row, nrow, first_block, w1_ref, w2_ref, pe_ref, w1t_ref, carry_ref, out_refs):
    kc_ref, vc_ref, vct_ref = out_refs
    outs = []
    for c in range(2):
        acc = jnp.zeros((nrow, 2 * KV_DIM), _F32)
        for s in range(CMP_STRIDE):
            acc = acc + _dot(chunk_row(c, s).astype(_BF16), w1_ref[c, s])
        pt = jnp.sum(pe_ref[c] * w1t_ref[c], axis=0, keepdims=True)
        pt = jnp.concatenate([pt] * (KV_DIM // LANES), axis=-1)
        a0 = acc[:, :KV_DIM]
        a1 = acc[:, KV_DIM:]
        first = lax.broadcasted_iota(jnp.int32, (nrow, KV_DIM), 0) == 0
        prev = 0.0 if first_block is True else jnp.where(first_block, 0.0, carry_ref[c])
        a0s = jnp.where(first, prev, pltpu.roll(a0, 1, 0))
        carry_ref[c] = a0[nrow - 1:nrow, :]
        hid = jax.nn.gelu(a0s + a1 + pt)
        outs.append(_dot(hid.astype(_BF16), w2_ref[c]))
    kc_ref[0] = outs[0].astype(_BF16)
    vc_ref[0] = outs[1].astype(_BF16)
    vct_ref[0] = outs[1].T.astype(_BF16)


def _compress_kernel(src_ref, w1_ref, w2_ref, pe_ref, w1t_ref, kc_ref, vc_ref, vct_ref, carry_ref, *, nrow):
    chunk_row = lambda c, s: src_ref[0, :, s * 2 * KV_DIM + c * KV_DIM:s * 2 * KV_DIM + (c + 1) * KV_DIM]
    _compress_rows(chunk_row, nrow, True, w1_ref, w2_ref, pe_ref, w1t_ref, carry_ref, (kc_ref, vc_ref, vct_ref))


def _compress_paged_kernel(pt_ref, *refs, n_src, steps_per_row):
    del pt_ref
    src_refs = refs[:n_src]
    w1_ref, w2_ref, pe_ref, w1t_ref, kc_ref, vc_ref, vct_ref, carry_ref, rows_a, rows_b = refs[n_src:]
    t = pl.program_id(0)
    nrow = n_src * PAGE_SIZE // CMP_STRIDE
    first_block = (jnp.maximum(t - 1, 0) % steps_per_row) == 0

    @pl.when(t == 0)
    def _():
        rows_b[...] = jnp.zeros_like(rows_b)
        carry_ref[...] = jnp.zeros_like(carry_ref)

    def step(fill_ref, read_ref):
        for p, r in enumerate(src_refs):
            for lc in range(2 * KV_DIM // LANES):
                fill_ref[lc, p * PAGE_SIZE:(p + 1) * PAGE_SIZE, :] = r[0, lc * LANES:(lc + 1) * LANES, :].T

        def chunk_row(c, s):
            lcs = range(c * KV_DIM // LANES, (c + 1) * KV_DIM // LANES)
            return jnp.concatenate([read_ref[lc, pl.ds(s, nrow, stride=CMP_STRIDE), :] for lc in lcs], axis=-1)

        _compress_rows(chunk_row, nrow, first_block, w1_ref, w2_ref, pe_ref, w1t_ref, carry_ref,
                       (kc_ref, vc_ref, vct_ref))

    @pl.when(t % 2 == 0)
    def _():
        step(rows_a, rows_b)

    @pl.when(t % 2 == 1)
    def _():
        step(rows_b, rows_a)


def _compress_call(kern, grid, in_specs, out_specs, scratch, n_prefetch, n_batch, n_out_rows, name, args):
    grid_spec = pltpu.PrefetchScalarGridSpec(num_scalar_prefetch=n_prefetch, grid=grid, in_specs=in_specs,
                                             out_specs=out_specs, scratch_shapes=scratch)
    return pl.pallas_call(
        kern,
        grid_spec=grid_spec,
        out_shape=[jax.ShapeDtypeStruct((n_batch, n_out_rows, KV_DIM), _BF16)] * 2
                  + [jax.ShapeDtypeStruct((n_batch, KV_DIM, n_out_rows), _BF16)],
        compiler_params=_params(*(["arbitrary"] * len(grid))),
        name=name,
    )(*args)


def _const_spec(a):
    return pl.BlockSpec(a.shape, lambda *_, _n=a.ndim: (0,) * _n, pipeline_mode=pl.Buffered(1))


def _compress(src, w1bd, w2bd, pe_b, w1t):
    bsz, nrow, cols = src.shape
    consts = (w1bd, w2bd, pe_b, w1t)
    return _compress_call(
        functools.partial(_compress_kernel, nrow=nrow), (bsz,),
        [pl.BlockSpec((1, nrow, cols), lambda b: (b, 0, 0))] + [_const_spec(a) for a in consts],
        [pl.BlockSpec((1, nrow, KV_DIM), lambda b: (b, 0, 0))] * 2 + [pl.BlockSpec((1, KV_DIM, nrow), lambda b: (b, 0, 0))],
        [pltpu.VMEM((2, 1, KV_DIM), _F32)], 0, bsz, nrow, "compress", (src,) + consts)


def _compress_paged(pages, page_table, w1bd, w2bd, pe_b, w1t):
    bsz, n_pages = page_table.shape
    spr = n_pages // PAGES_PER_STEP
    n_steps = bsz * spr
    nrow = PAGES_PER_STEP * PAGE_SIZE // CMP_STRIDE
    consts = (w1bd, w2bd, pe_b, w1t)

    def page_spec(p):
        def index(t, pt):
            tc = jnp.minimum(t, n_steps - 1)
            return (pt[tc // spr, (tc % spr) * PAGES_PER_STEP + p], 0, 0)
        return pl.BlockSpec((1, 2 * KV_DIM, PAGE_SIZE), index)

    prev = lambda t: jnp.maximum(t - 1, 0)
    row_out = pl.BlockSpec((1, nrow, KV_DIM), lambda t, pt: (prev(t) // spr, prev(t) % spr, 0))
    col_out = pl.BlockSpec((1, KV_DIM, nrow), lambda t, pt: (prev(t) // spr, 0, prev(t) % spr))
    rows_buf = pltpu.VMEM((2 * KV_DIM // LANES, PAGES_PER_STEP * PAGE_SIZE, LANES), _F32)
    return _compress_call(
        functools.partial(_compress_paged_kernel, n_src=PAGES_PER_STEP, steps_per_row=spr), (n_steps + 1,),
        [page_spec(p) for p in range(PAGES_PER_STEP)] + [_const_spec(a) for a in consts],
        [row_out, row_out, col_out], [pltpu.VMEM((2, 1, KV_DIM), _F32), rows_buf, rows_buf],
        1, bsz, n_pages * PAGE_SIZE // CMP_STRIDE, "compress_paged",
        (page_table,) + (pages,) * PAGES_PER_STEP + consts)


def _threshold_select(sc, n_sel, tri):
    bits = pltpu.bitcast(sc, jnp.int32)
    key = jnp.where(bits < 0, bits ^ jnp.int32(0x7FFFFFFF), bits)
    count_ge = lambda t: jnp.sum(jnp.where(key >= t, 1.0, 0.0), axis=0, keepdims=True)
    t0 = jnp.where(count_ge(jnp.int32(0)) >= n_sel, jnp.int32(0), jnp.int32(-2 ** 31))

    def body(b, t):
        cand = t | jnp.left_shift(jnp.int32(1), 30 - b)
        return jnp.where(count_ge(cand) >= n_sel, cand, t)

    t = lax.fori_loop(0, 31, body, t0)
    above = jnp.where(key > t, 1.0, 0.0)
    tie = jnp.where(key == t, 1.0, 0.0)
    need = n_sel - jnp.sum(above, axis=0, keepdims=True)
    tie_rank = _dot(tri, tie.astype(_BF16))
    return above + jnp.where(tie_rank <= need, tie, 0.0)


def _softmax_parts(parts):
    ms = [jnp.max(jnp.where(mk, s, NEG), axis=-1, keepdims=True) for s, mk in parts]
    m = functools.reduce(jnp.maximum, ms)
    ps = [jnp.where(mk, jnp.exp(s - m), 0.0) for s, mk in parts]
    l = functools.reduce(lambda a, b: a + b, [jnp.sum(p, axis=-1, keepdims=True) for p in ps])
    inv = 1.0 / jnp.maximum(l, 1e-30)
    return [p * inv for p in ps]


def _softmax_cols(parts, always_valid):
    masked = [jnp.where(mk, s, NEG) for s, mk in parts]
    m = functools.reduce(jnp.maximum, [jnp.max(s, axis=0, keepdims=True) for s in masked])
    if always_valid:
        ps = [jnp.exp(s - m) for s in masked]
    else:
        ps = [jnp.where(mk, jnp.exp(s - m), 0.0) for s, mk in parts]
    l = functools.reduce(lambda a, b: a + b, [jnp.sum(p, axis=0, keepdims=True) for p in ps])
    return ps, 1.0 / jnp.maximum(l, 1e-30)


def _attn_prompt_kernel(q_ref, ng_ref, ks_ref, vst_ref, kw_ref, vwt_ref, kc_ref, vct_ref,
                        tzt_ref, tct_ref, ot_ref, gsel_ref, rsum_ref, rep_ref, tri_ref,
                        o_ref, qzt_ref, otacc_ref, gt_ref, nsel_ref, *, n_cmp_pad, seq_len):
    i = pl.program_id(1)
    qb = Q_BLOCK
    rows = HPG * qb
    lane_g = lax.broadcasted_iota(jnp.int32, (qb, KV_DIM), 1) // HEAD_DIM
    vrows = lambda g: pl.ds(g * HEAD_DIM, HEAD_DIM)

    for g in range(N_KV):
        qz = jnp.concatenate([jnp.where(lane_g == g, q_ref[0, :, r * KV_DIM:(r + 1) * KV_DIM].astype(_F32), 0.0)
                              for r in range(HPG)], axis=0)
        qzt_ref[g] = qz.T.astype(_BF16)

    gparts = _split3(jax.nn.sigmoid(ng_ref[0]))
    gcols = jnp.concatenate([sum(_dot(p, gsel_ref[r]) for p in gparts) for r in range(HPG)], axis=0)
    gt_ref[...] = gcols.T
    otacc_ref[...] = jnp.zeros_like(otacc_ref)

    def emit(branch, g, out_t):
        otacc_ref[g] += gt_ref[pl.ds(branch * N_KV + g, 1), :] * out_t

    jc = lax.broadcasted_iota(jnp.int32, (n_cmp_pad, rows), 0)
    lq = lax.broadcasted_iota(jnp.int32, (n_cmp_pad, rows), 1) % qb
    cmp_valid = (jc >= 1) & (CMP_STRIDE * jc + (CMP_LEN - CMP_STRIDE - 1) - lq <= i * qb)
    mm = lax.broadcasted_iota(jnp.int32, (n_cmp_pad, LANES), 1)
    jj = lax.broadcasted_iota(jnp.int32, (n_cmp_pad, LANES), 0)
    shift_t = jnp.where((mm < 16) & (jj - mm == 4 * i - 8), 1.0, 0.0).astype(_BF16)
    kc_aug = jnp.concatenate([kc_ref[0], shift_t, shift_t, shift_t], axis=1)
    groups = range(N_KV)
    s_c = [_dot(kc_aug, jnp.concatenate([qzt_ref[g], tct_ref[0, g], tct_ref[1, g], tct_ref[2, g]], axis=0))
           for g in groups]
    sm_c = [_softmax_cols([(s, cmp_valid)], False) for s in s_c]
    pn_c = [ps[0] for ps, _ in sm_c]
    inv_c = [inv for _, inv in sm_c]
    for g in groups:
        emit(0, g, _dot(vct_ref[0, vrows(g), :], pn_c[g].astype(_BF16)) * inv_c[g])
    y_c = [sum(_dot(ot_ref[...], p) for p in _split3(pn_c[g])) * inv_c[g] for g in groups]
    imp_t = sum(sum(_dot(p, rsum_ref[g]) for p in _split3(y_c[g])) for g in groups)

    jrow = lax.broadcasted_iota(jnp.int32, imp_t.shape, 0)
    forced = (jrow == 0) | (jrow == i) | (jrow == i - 1)
    score = jnp.where(forced, jnp.inf, jnp.where(jrow <= i, imp_t, -jnp.inf))
    sel_t = _threshold_select(score, N_SELECT, tri_ref[...])

    h = jnp.minimum(jnp.maximum(i - 3, 0) // 2, (seq_len - NEAR_KEYS) // LANES)
    ns = pl.multiple_of(h * LANES, LANES)
    v = i - 2 * h
    sel16 = sel_t.astype(_BF16)
    for g in groups:
        neg = (_dot(sel16, rep_ref[g]) - 1.0) * MASK_BIG
        nsel_ref[0, g] = neg
        nsel_ref[1, g] = jnp.where(jrow < 2 * h, neg, -MASK_BIG)

    def add_block_mask(s, far, g, j0, n_blocks):
        return jnp.concatenate([s[jj * SEL_BLOCK:(jj + 1) * SEL_BLOCK, :] + nsel_ref[far, g, pl.ds(j0 + jj, 1), :]
                                for jj in range(n_blocks)], axis=0)

    cn = lax.broadcasted_iota(jnp.int32, (NEAR_KEYS, rows), 0)
    lqn = lax.broadcasted_iota(jnp.int32, (NEAR_KEYS, rows), 1) % qb
    causal = cn - lqn <= v * SEL_BLOCK
    near = pl.ds(ns, NEAR_KEYS)
    n_far = (ns + FAR_TILE - 1) // FAR_TILE

    k_near = ks_ref[0, near, :]
    s_n = [add_block_mask(_dot(k_near, qzt_ref[g]) + tzt_ref[v, g], 0, g, 2 * h, NEAR_KEYS // SEL_BLOCK)
           for g in groups]
    s_n = [jnp.where(causal, s, NEG) for s in s_n]
    m0 = [jnp.max(s, axis=0, keepdims=True) for s in s_n]
    p_n = [jnp.exp(s_n[g] - m0[g]) for g in groups]
    l0 = [jnp.sum(p, axis=0, keepdims=True) for p in p_n]
    acc0 = [_dot(vst_ref[0, vrows(g), near], p_n[g].astype(_BF16)) for g in groups]

    def far_body(tau, carry):
        m_old, l_old, acc_old = carry
        k0 = pl.multiple_of(tau * FAR_TILE, FAR_TILE)
        tile = pl.ds(k0, FAR_TILE)
        kt = ks_ref[0, tile, :]
        s_f = [add_block_mask(_dot(kt, qzt_ref[g]), 1, g, tau * (FAR_TILE // SEL_BLOCK), FAR_TILE // SEL_BLOCK)
               for g in groups]
        m_new = [jnp.maximum(m_old[g], jnp.max(s_f[g], axis=0, keepdims=True)) for g in groups]
        alpha = [jnp.exp(m_old[g] - m_new[g]) for g in groups]
        p_f = [jnp.exp(s_f[g] - m_new[g]) for g in groups]
        l_new = [alpha[g] * l_old[g] + jnp.sum(p_f[g], axis=0, keepdims=True) for g in groups]
        acc = [alpha[g] * acc_old[g] + _dot(vst_ref[0, vrows(g), tile], p_f[g].astype(_BF16)) for g in groups]
        return tuple(m_new), tuple(l_new), tuple(acc)

    _, l1, acc1 = lax.fori_loop(0, n_far, far_body, (tuple(m0), tuple(l0), tuple(acc0)))
    for g in groups:
        emit(1, g, acc1[g] * (1.0 / l1[g]))

    fs = pl.multiple_of((jnp.maximum(i - WINDOW // SEL_BLOCK, 0) // 2) * LANES, LANES)
    wfar = pl.ds(fs, WIN_FAR_KEYS)
    cf = lax.broadcasted_iota(jnp.int32, (WIN_FAR_KEYS, rows), 0)
    lqf = lax.broadcasted_iota(jnp.int32, (WIN_FAR_KEYS, rows), 1) % qb
    far_ok = (i * qb + lqf - fs - cf < WINDOW) & (fs + cf < ns)
    kw_near, kw_far = kw_ref[0, near, :], kw_ref[0, wfar, :]
    s_wn = [_dot(kw_near, qzt_ref[g]) + tzt_ref[v, g] for g in groups]
    s_wf = [_dot(kw_far, qzt_ref[g]) for g in groups]
    p_w = [_softmax_cols([(s_wn[g], causal), (s_wf[g], far_ok)], True) for g in groups]
    for g in groups:
        (p_near, p_far), inv_w = p_w[g]
        emit(2, g, (_dot(vwt_ref[0, vrows(g), near], p_near.astype(_BF16))
                    + _dot(vwt_ref[0, vrows(g), wfar], p_far.astype(_BF16))) * inv_w)

    res = otacc_ref[...].reshape(N_KV * HEAD_DIM, rows).T
    for r in range(HPG):
        o_ref[0, :, r * KV_DIM:(r + 1) * KV_DIM] = res[r * qb:(r + 1) * qb, :].astype(_BF16)


def _attn_prompt(q2, ng, ks, vst, kw, vwt, kc, vct, tabs):
    bsz, t, _ = q2.shape
    nblk = t // SEL_BLOCK
    n_cmp_pad = kc.shape[1]
    rows = HPG * Q_BLOCK
    assert t % FAR_TILE == 0 and t >= NEAR_KEYS
    names = ["tzt", "tct", "ot", "gsel", "rsum", "rep", "tri"]
    full = lambda a: pl.BlockSpec(a.shape, lambda b, i, _n=a.ndim: (0,) * _n, pipeline_mode=pl.Buffered(1))
    per_b = lambda a: pl.BlockSpec((1,) + a.shape[1:], lambda b, i: (b, 0, 0))
    kern = functools.partial(_attn_prompt_kernel, n_cmp_pad=n_cmp_pad, seq_len=t)
    return pl.pallas_call(
        kern,
        grid=(bsz, nblk),
        in_specs=[pl.BlockSpec((1, Q_BLOCK, ATTN_DIM), lambda b, i: (b, i, 0)),
                  pl.BlockSpec((1, Q_BLOCK, LANES), lambda b, i: (b, i, 0)),
                  per_b(ks), per_b(vst), per_b(kw), per_b(vwt), per_b(kc), per_b(vct)]
                 + [full(tabs[n]) for n in names],
        out_specs=pl.BlockSpec((1, Q_BLOCK, ATTN_DIM), lambda b, i: (b, i, 0)),
        out_shape=jax.ShapeDtypeStruct((bsz, t, ATTN_DIM), _BF16),
        scratch_shapes=[pltpu.VMEM((N_KV, KV_DIM, rows), _BF16),
                        pltpu.VMEM((N_KV, HEAD_DIM, rows), _F32),
                        pltpu.VMEM((LANES, rows), _F32),
                        pltpu.VMEM((2, N_KV, SEL_BLOCK, rows), _F32)],
        compiler_params=_params("parallel", "arbitrary"),
        name="attn_prompt",
    )(q2, ng, ks, vst, kw, vwt, kc, vct, *[tabs[n] for n in names])


def _bias_table(rel, dist, head):
    nmax = max(int(dist.max()), 1) + 1
    bk = _rel_bucket_np(np.arange(nmax))
    rel_h = rel[:, head]
    out = jnp.broadcast_to(rel_h[0], np.broadcast_shapes(dist.shape, head.shape))
    dist = lax.optimization_barrier(jnp.asarray(dist, jnp.int32))
    for b in range(1, int(bk.max()) + 1):
        first = int(np.argmax(bk >= b))
        out = jnp.where(dist >= first, rel_h[b], out)
    return out


def _prompt_tables(rel_bias, t):
    nblk = t // SEL_BLOCK
    n_cmp_pad = t // CMP_STRIDE
    assert nblk <= SEL_BLOCK and n_cmp_pad % LANES == 0
    rows = HPG * Q_BLOCK
    rel = rel_bias.astype(_F32)
    r_idx = np.arange(rows) // Q_BLOCK
    q_idx = np.arange(rows) % Q_BLOCK
    head = np.arange(N_KV)[:, None] * HPG + r_idx[None, :]
    c31 = rel[REL_BUCKETS - 1][head]
    c = np.arange(NEAR_KEYS)
    dist = np.arange(NEAR_VARIANTS)[:, None, None] * SEL_BLOCK + q_idx[None, None, :] - c[None, :, None]
    tzt = _bias_table(rel, dist[:, None], head[None, :, None, :]) - c31[None, :, None, :]
    mmv = np.arange(16)
    dist_c = q_idx[None, :] - CMP_STRIDE * (mmv[:, None] - 8) - (CMP_LEN - CMP_STRIDE - 1)
    delta = _bias_table(rel, dist_c[None], head[:, None, :]) - c31[:, None, :]
    tct = jnp.stack(_split3(jnp.pad(delta, ((0, 0), (0, LANES - 16), (0, 0)))))
    n = np.arange(n_cmp_pad) - 1
    cs = n * CMP_STRIDE
    bs = np.arange(SEL_BLOCK) * SEL_BLOCK
    ov = np.clip(np.minimum(cs[None, :] + CMP_LEN, bs[:, None] + SEL_BLOCK) - np.maximum(cs[None, :], bs[:, None]),
                 0, CMP_LEN).astype(np.float32) / CMP_LEN
    ov[:, 0] = 0.0
    ov[nblk:, :] = 0.0
    gsel = np.zeros((HPG, LANES, LANES), np.float32)
    for j in range(3):
        for r in range(HPG):
            for g in range(N_KV):
                gsel[r, j * N_HEADS + r * N_KV + g, j * N_KV + g] = 1.0
    rsum = np.zeros((N_KV, rows, N_KV * Q_BLOCK), np.float32)
    for g in range(N_KV):
        rsum[g, np.arange(rows), g * Q_BLOCK + q_idx] = 1.0
    bf = lambda a: jnp.asarray(a, _BF16)
    return dict(tzt=tzt, tct=tct, ot=bf(ov), gsel=bf(gsel), rsum=bf(rsum),
                tri=bf(np.tril(np.ones((SEL_BLOCK, SEL_BLOCK), np.float32))),
                rep=bf(rsum.transpose(0, 2, 1)))


def _gate_expand():
    ex = np.zeros((LANES, 3 * ATTN_DIM), np.float32)
    for j in range(3):
        for r in range(HPG):
            for g in range(N_KV):
                col = j * ATTN_DIM + r * KV_DIM + g * HEAD_DIM
                ex[j * N_HEADS + r * N_KV + g, col:col + HEAD_DIM] = 1.0
    return jnp.asarray(ex, _BF16)


PAGES_PER_STEP = 32
SUB_PAGES = 32


def _attn_sample_kernel(pt_ref, q_ref, ng_ref, kc_ref, vc_ref, knew_ref, wnew_ref, wst_ref, *refs,
                        past_len, n_blk, nb_rows):
    del pt_ref
    page_refs = refs[:PAGES_PER_STEP]
    (bc_ref, blast_ref, c31_ref, bnew_ref, bwin_ref, ot_ref, e64_ref, ex_ref, tri_ref,
     o_ref, qall_ref, gate_ref, oacc_ref, acc_ref, m_ref, l_ref, selt_ref) = refs[PAGES_PER_STEP:]
    k = pl.program_id(1)
    nk = pl.num_programs(1)
    tq = q_ref.shape[1]
    rows = N_KV * HPG * tq
    lane_g = lax.broadcasted_iota(jnp.int32, (tq, KV_DIM), 1) // HEAD_DIM
    rq = lax.broadcasted_iota(jnp.int32, (rows, 1), 0) % tq
    sub_keys = SUB_PAGES * PAGE_SIZE

    def emit(branch, out):
        for g in range(N_KV):
            for r in range(HPG):
                col = r * KV_DIM
                row0 = (g * HPG + r) * tq
                gt = gate_ref[:, branch * ATTN_DIM + col:branch * ATTN_DIM + col + KV_DIM]
                oacc_ref[:, col:col + KV_DIM] += jnp.where(lane_g == g, gt * out[row0:row0 + tq, :], 0.0)

    def pad_rows(x, n):
        if n == x.shape[0]:
            return x
        return jnp.concatenate([x, jnp.zeros((n - x.shape[0], x.shape[1]), x.dtype)], axis=0)

    @pl.when(k == 0)
    def _():
        qf = q_ref[0].astype(_F32)
        pieces = []
        for g in range(N_KV):
            for r in range(HPG):
                pieces.append(jnp.where(lane_g == g, qf[:, r * KV_DIM:(r + 1) * KV_DIM], 0.0))
        qall = jnp.concatenate(pieces, axis=0).astype(_BF16)
        qall_ref[...] = qall
        gs = jax.nn.sigmoid(ng_ref[0])
        gate_ref[...] = sum(_dot(p, ex_ref[...]) for p in _split3(gs))
        oacc_ref[...] = jnp.zeros_like(oacc_ref)

        n_cmp_pad = kc_ref.shape[1]
        jc = lax.broadcasted_iota(jnp.int32, (rows, n_cmp_pad), 1)
        cmp_valid = (jc >= 1) & (CMP_STRIDE * jc + (CMP_LEN - CMP_STRIDE - 1) - rq <= past_len)
        s = _dot_nt(qall, kc_ref[0]) + bc_ref[...]
        (pn,) = _softmax_parts([(s, cmp_valid)])
        emit(0, _dot(pn.astype(_BF16), vc_ref[0]))
        imp_rows = []
        for g in range(N_KV):
            sg = sum(pn[(g * HPG + r) * tq:(g * HPG + r + 1) * tq, :] for r in range(HPG))
            imp_rows += [sg] * HPG
        imp = jnp.concatenate(imp_rows, axis=0)
        imp_t = sum(_dot_nt(ot_ref[...], p) for p in _split3(imp))

        jrow = lax.broadcasted_iota(jnp.int32, imp_t.shape, 0)
        tpos = past_len + lax.broadcasted_iota(jnp.int32, imp_t.shape, 1) % tq
        cur = tpos // SEL_BLOCK
        forced = (jrow == 0) | (jrow == cur) | (jrow == cur - 1)
        valid = jrow * SEL_BLOCK <= tpos
        score = jnp.where(forced, jnp.inf, jnp.where(valid, imp_t, -jnp.inf))
        selt_ref[...] = pad_rows(_threshold_select(score, min(N_SELECT, n_blk), tri_ref[...]), selt_ref.shape[0])

        kn = pad_rows(knew_ref[0, :, :KV_DIM], LANES).astype(_BF16)
        vn = pad_rows(knew_ref[0, :, KV_DIM:], LANES).astype(_BF16)
        cn = lax.broadcasted_iota(jnp.int32, (rows, LANES), 1)
        mk = (cn <= rq) & (cn < tq)
        s = _dot_nt(qall, kn) + bnew_ref[...]
        m = jnp.max(jnp.where(mk, s, NEG), axis=-1, keepdims=True)
        p = jnp.where(mk, jnp.exp(s - m), 0.0)
        m_ref[...] = jnp.broadcast_to(m, m_ref.shape)
        l_ref[...] = jnp.broadcast_to(jnp.sum(p, axis=-1, keepdims=True), l_ref.shape)
        acc_ref[...] = _dot(p.astype(_BF16), vn)

    qall = qall_ref[...]
    blk_per_step = PAGES_PER_STEP * PAGE_SIZE // SEL_BLOCK
    j0 = pl.multiple_of(k * blk_per_step, blk_per_step)
    sel_step = selt_ref[pl.ds(j0, LANES), :].T.astype(_BF16)
    c31 = jnp.concatenate([c31_ref[...]] * (sub_keys // LANES), axis=-1)
    n_sub = PAGES_PER_STEP // SUB_PAGES
    for st in range(n_sub):
        pages = page_refs[st * SUB_PAGES:(st + 1) * SUB_PAGES]
        kt = jnp.concatenate([r[0, :KV_DIM, :] for r in pages], axis=1).astype(_BF16)
        vt = jnp.concatenate([r[0, KV_DIM:, :] for r in pages], axis=1).astype(_BF16)
        mk = _dot(sel_step, e64_ref[:, st * sub_keys:(st + 1) * sub_keys]) > 0.5
        if st == n_sub - 1:
            bias = jnp.where(k == nk - 1, blast_ref[...], c31)
        else:
            bias = c31
        s = jnp.where(mk, _dot(qall, kt) + bias, NEG)
        m_old = m_ref[:, :1]
        m_new = jnp.maximum(m_old, jnp.max(s, axis=-1, keepdims=True))
        alpha = jnp.exp(m_old - m_new)
        p = jnp.exp(s - m_new)
        l_ref[...] = jnp.broadcast_to(alpha * l_ref[:, :1] + jnp.sum(p, axis=-1, keepdims=True), l_ref.shape)
        m_ref[...] = jnp.broadcast_to(m_new, m_ref.shape)
        acc_ref[...] = alpha * acc_ref[...] + _dot_nt(p.astype(_BF16), vt)

    @pl.when(k == nk - 1)
    def _():
        emit(1, acc_ref[...] * (1.0 / l_ref[:, :1]))
        lw = wst_ref.shape[2]
        kw = wst_ref[0, :KV_DIM, :].astype(_BF16)
        vw = wst_ref[0, KV_DIM:, :].astype(_BF16)
        kn = pad_rows(wnew_ref[0, :, :KV_DIM], LANES).astype(_BF16)
        vn = pad_rows(wnew_ref[0, :, KV_DIM:], LANES).astype(_BF16)
        cw = lax.broadcasted_iota(jnp.int32, (rows, lw), 1)
        dw = lw + rq - cw
        cn = lax.broadcasted_iota(jnp.int32, (rows, LANES), 1)
        pw, pnw = _softmax_parts([(_dot(qall, kw) + bwin_ref[...], (dw >= 0) & (dw < WINDOW)),
                                  (_dot_nt(qall, kn) + bnew_ref[...], (cn <= rq) & (cn < tq))])
        emit(2, _dot_nt(pw.astype(_BF16), vw) + _dot(pnw.astype(_BF16), vn))
        o_ref[0] = oacc_ref[...].astype(_BF16)


def _page_specs():
    return [pl.BlockSpec((1, 2 * KV_DIM, PAGE_SIZE),
                         lambda b, k, pt, _p=p: (pt[b, k * PAGES_PER_STEP + _p], 0, 0))
            for p in range(PAGES_PER_STEP)]


def _transposed_rows(a):
    n, rows = a.shape[:2]
    return jnp.transpose(a, (0, 2, 3, 4, 1)).reshape(n, 2 * KV_DIM, rows)


def _attn_sample(q2, ng, kc, vc, kvs_new, kvw_new, win_state, slc_pages, page_table, tabs, past_len):
    bsz, tq, _ = q2.shape
    n_pages = page_table.shape[1]
    assert n_pages % PAGES_PER_STEP == 0 and past_len == n_pages * PAGE_SIZE and past_len % SEL_BLOCK == 0
    n_steps = n_pages // PAGES_PER_STEP
    rows = N_KV * HPG * tq
    assert rows == LANES
    n_blk = -(-(past_len + tq) // SEL_BLOCK)
    nb_rows = tabs["ot"].shape[0]
    blk_per_step = PAGES_PER_STEP * PAGE_SIZE // SEL_BLOCK
    selt_rows = max((n_steps - 1) * blk_per_step + LANES, nb_rows)
    full = lambda a: pl.BlockSpec(a.shape, lambda b, k, pt, _n=a.ndim: (0,) * _n, pipeline_mode=pl.Buffered(1))
    per_b = lambda a: pl.BlockSpec((1,) + a.shape[1:], lambda b, k, pt: (b, 0, 0))
    page_specs = _page_specs()
    names = ["bc", "blast", "c31", "bnew", "bwin", "ot", "e64", "ex", "tri"]
    kern = functools.partial(_attn_sample_kernel, past_len=past_len, n_blk=n_blk, nb_rows=nb_rows)
    grid_spec = pltpu.PrefetchScalarGridSpec(
        num_scalar_prefetch=1,
        grid=(bsz, n_steps),
        in_specs=[per_b(q2), per_b(ng), per_b(kc), per_b(vc), per_b(kvs_new), per_b(kvw_new), per_b(win_state)]
                 + page_specs + [full(tabs[n]) for n in names],
        out_specs=pl.BlockSpec((1, tq, ATTN_DIM), lambda b, k, pt: (b, 0, 0)),
        scratch_shapes=[pltpu.VMEM((rows, KV_DIM), _BF16),
                        pltpu.VMEM((tq, 3 * ATTN_DIM), _F32),
                        pltpu.VMEM((tq, ATTN_DIM), _F32),
                        pltpu.VMEM((rows, KV_DIM), _F32),
                        pltpu.VMEM((rows, LANES), _F32),
                        pltpu.VMEM((rows, LANES), _F32),
                        pltpu.VMEM((selt_rows, rows), _F32)],
    )
    return pl.pallas_call(
        kern,
        grid_spec=grid_spec,
        out_shape=jax.ShapeDtypeStruct((bsz, tq, ATTN_DIM), _BF16),
        compiler_params=_params("parallel", "arbitrary"),
        name="attn_sample",
    )(page_table, q2, ng, kc, vc, kvs_new, kvw_new, win_state, *([slc_pages] * PAGES_PER_STEP),
      *[tabs[n] for n in names])


def _sample_tables(rel_bias, past_len, tq, lw, n_cmp_pad):
    rows = N_KV * HPG * tq
    rel = rel_bias.astype(_F32)
    ridx = np.arange(rows)
    head = ridx // tq
    qi = ridx % tq
    tpos = past_len + qi

    bias_of = lambda dist: _bias_table(rel, dist, head[:, None])

    jc = np.arange(n_cmp_pad)
    bc = bias_of(tpos[:, None] - (CMP_STRIDE * jc[None, :] + CMP_LEN - CMP_STRIDE - 1))
    sub_keys = SUB_PAGES * PAGE_SIZE
    blast = bias_of(tpos[:, None] - (past_len - sub_keys + np.arange(sub_keys))[None, :])
    bnew = bias_of(qi[:, None] - np.arange(LANES)[None, :])
    bwin = bias_of(lw + qi[:, None] - np.arange(lw)[None, :])
    c31 = jnp.broadcast_to(rel[REL_BUCKETS - 1][head][:, None], (rows, LANES))
    n_blk = -(-(past_len + tq) // SEL_BLOCK)
    nb_rows = -(-n_blk // LANES) * LANES
    n = jc - 1
    cs = n * CMP_STRIDE
    bs = np.arange(nb_rows) * SEL_BLOCK
    ov = np.clip(np.minimum(cs[None, :] + CMP_LEN, bs[:, None] + SEL_BLOCK) - np.maximum(cs[None, :], bs[:, None]),
                 0, CMP_LEN).astype(np.float32) / CMP_LEN
    ov[:, 0] = 0.0
    ov[n_blk:, :] = 0.0
    step_keys = PAGES_PER_STEP * PAGE_SIZE
    e64 = (np.arange(step_keys)[None, :] // SEL_BLOCK == np.arange(LANES)[:, None]).astype(np.float32)
    tri = np.tril(np.ones((nb_rows, nb_rows), np.float32))
    return dict(bc=bc, blast=blast, c31=c31, bnew=bnew, bwin=bwin, ot=jnp.asarray(ov, _BF16),
                e64=jnp.asarray(e64, _BF16), ex=_gate_expand(), tri=jnp.asarray(tri, _BF16))


def _layer_weights(w_in, phi_pe, phi_w1, phi_w2, w_attn_out, conv_w, w_conv_out, w_o, w_up, w_down):
    d = w_in.shape[0]
    o_q, o_kc, o_ks, o_kw = 0, ATTN_DIM, ATTN_DIM + 2 * KV_DIM, ATTN_DIM + 4 * KV_DIM
    o_ng = ATTN_DIM + 6 * KV_DIM
    o_glu = o_ng + 3 * N_HEADS
    o_mg = o_glu + 2 * CONV_DIM
    bf = lambda a: a.astype(_BF16)
    wq = w_in[:, o_q:o_kc].reshape(d, N_KV, HPG, HEAD_DIM).transpose(0, 2, 1, 3).reshape(d, ATTN_DIM)
    wng = w_in[:, o_ng:o_glu].reshape(d, N_KV, HPG, 3).transpose(0, 3, 2, 1).reshape(d, 3 * N_HEADS)
    wng = jnp.pad(wng, ((0, 0), (0, LANES - 3 * N_HEADS)))
    w5 = phi_w1.reshape(2, CMP_R, CMP_STRIDE, HEAD_DIM, PHI_HIDDEN)
    eye = jnp.eye(N_KV, dtype=_F32)
    w1bd = jnp.einsum("crsde,gh->csgdrhe", w5, eye).reshape(2, CMP_STRIDE, KV_DIM, CMP_R * KV_DIM)
    w2bd = jnp.einsum("che,gk->cghke", phi_w2, eye).reshape(2, KV_DIM, KV_DIM)
    rep = LANES // PHI_HIDDEN
    pe_b = jnp.broadcast_to(phi_pe.reshape(2, CMP_LEN * HEAD_DIM, 1), (2, CMP_LEN * HEAD_DIM, LANES))
    w1t = jnp.tile(phi_w1, (1, 1, rep))
    wao = w_attn_out.reshape(N_KV, HPG, HEAD_DIM, d).transpose(1, 0, 2, 3).reshape(ATTN_DIM, d)
    half = PROJ_TILE // 2
    wga = w_in[:, o_glu:o_glu + CONV_DIM].reshape(d, CONV_DIM // half, half)
    wgb = w_in[:, o_glu + CONV_DIM:o_mg].reshape(d, CONV_DIM // half, half)
    wglu = jnp.concatenate([wga, wgb], axis=2).reshape(d, 2 * CONV_DIM)
    w_all = jnp.concatenate([wq, w_in[:, o_kc:o_ng], wglu, w_in[:, o_mg:]], axis=1)
    return dict(
        w_all=bf(w_all), wng=bf(wng), w1bd=bf(w1bd), w2bd=bf(w2bd), pe_b=pe_b, w1t=w1t,
        wao=bf(wao), wco=bf(w_conv_out), wo=bf(w_o), wup=bf(w_up), wdown=bf(w_down),
        conv_w=jnp.pad(conv_w, ((0, HALO - CONV_K), (0, 0))))


def _finish(x2, attn2, conv, mg, w, g_post_mix, g_pre_ffn, g_post_ffn, tm):
    mixed = _mix(attn2, conv, mg, w["wao"], w["wco"], tm, 512)
    x1 = _oproj(mixed, x2, w["wo"], g_post_mix, tm)
    return _ffn(x1, g_pre_ffn, g_post_ffn, w["wup"], w["wdown"], tm, 512)


def kernel(x_prompt, x_sample, cache_kv_cmp, cache_kv_slc, state_kv_win, state_conv, page_table, w_in, phi_pe,
           phi_w1, phi_w2, rel_bias, w_attn_out, conv_w, conv_b, conv_ln_g, conv_ln_b, w_conv_out, w_o, w_up,
           w_down, g_pre_mix, g_post_mix, g_pre_ffn, g_post_ffn):
    depth = w_in.shape[0]
    bp, tp, d = x_prompt.shape
    bs, ts, _ = x_sample.shape
    n_pages = page_table.shape[1]
    past_len = n_pages * PAGE_SIZE
    lw = state_kv_win.shape[2]
    chunk_cols = CMP_STRIDE * 2 * KV_DIM
    assert ts < CMP_STRIDE and tp % CMP_STRIDE == 0 and lw == WINDOW and tp >= WINDOW

    tabs_p = _prompt_tables(rel_bias, tp)
    tabs_s = _sample_tables(rel_bias, past_len, ts, lw, past_len // CMP_STRIDE)
    yp, ys = x_prompt.reshape(bp * tp, d), x_sample.reshape(bs * ts, d)
    outs = [[] for _ in range(8)]
    row = lambda a: a.reshape(1, -1)
    kv5 = lambda a, b, t: a.reshape(b, t, 2, N_KV, HEAD_DIM)
    kv5_t = lambda a: jnp.transpose(a.reshape(a.shape[0], 2, N_KV, HEAD_DIM, a.shape[2]), (0, 4, 1, 2, 3))
    for l in range(depth):
        w = _layer_weights(w_in[l], phi_pe[l], phi_w1[l], phi_w2[l], w_attn_out[l], conv_w[l], w_conv_out[l],
                           w_o[l], w_up[l], w_down[l])
        gpm, gqm, gpf, gqf = row(g_pre_mix[l]), row(g_post_mix[l]), row(g_pre_ffn[l]), row(g_post_ffn[l])
        cargs = (w["conv_w"], row(conv_b[l]), row(conv_ln_g[l]), row(conv_ln_b[l]))
        cmp_w = (w["w1bd"], w["w2bd"], w["pe_b"], w["w1t"])

        tm = 512
        q2, (kvc_t, kvc16), (kvs_t, ks16, vst16), (kvw_t, kw16, vwt16), ng, u, mg = _proj(
            yp, gpm, w["w_all"], w["wng"], tm, tp)
        n_chunk = tp // CMP_STRIDE
        kc, _, vct = _compress(kvc16.reshape(bp, n_chunk, chunk_cols), *cmp_w)
        attn2 = _attn_prompt(q2.reshape(bp, tp, ATTN_DIM), ng.reshape(bp, tp, LANES),
                             ks16.reshape(bp, tp, KV_DIM), vst16, kw16.reshape(bp, tp, KV_DIM), vwt16,
                             kc, vct, tabs_p)
        u3 = u.reshape(bp, tp, CONV_DIM)
        conv = _conv(u3, u3, *cargs, 256, True)
        yp = _finish(yp, attn2.reshape(bp * tp, ATTN_DIM), conv.reshape(bp * tp, CONV_DIM), mg, w, gqm, gpf, gqf, tm)
        outs[0].append(kv5_t(kvc_t))
        outs[2].append(kv5_t(kvs_t))
        outs[4].append(kv5_t(kvw_t[:, :, tp - WINDOW:]))
        outs[6].append(u3[:, tp - (CONV_K - 1):])

        tm = bs * ts
        q2, (kvc,), (kvs,), (kvw,), ng, u, mg = _proj(ys, gpm, w["w_all"], w["wng"], tm)
        kc, vc, _ = _compress_paged(_transposed_rows(cache_kv_cmp[l]), page_table, *cmp_w)
        attn2 = _attn_sample(q2.reshape(bs, ts, ATTN_DIM), ng.reshape(bs, ts, LANES), kc, vc,
                             kvs.reshape(bs, ts, 2 * KV_DIM), kvw.reshape(bs, ts, 2 * KV_DIM),
                             _transposed_rows(state_kv_win[l]), _transposed_rows(cache_kv_slc[l]),
                             page_table, tabs_s, past_len)
        u3 = u.reshape(bs, ts, CONV_DIM)
        hist = jnp.pad(state_conv[l], ((0, 0), (HALO - (CONV_K - 1), 0), (0, 0)))
        conv = _conv(u3, hist, *cargs, ts, False)
        ys = _finish(ys, attn2.reshape(bs * ts, ATTN_DIM), conv.reshape(bs * ts, CONV_DIM), mg, w, gqm, gpf, gqf, tm)
        outs[1].append(kv5(kvc, bs, ts))
        outs[3].append(kv5(kvs, bs, ts))
        win_rows = jnp.concatenate([state_kv_win[l], kv5(kvw, bs, ts)], axis=1)
        outs[5].append(win_rows[:, win_rows.shape[1] - min(WINDOW, win_rows.shape[1]):])
        up = jnp.concatenate([state_conv[l], u3], axis=1)
        outs[7].append(up[:, up.shape[1] - (CONV_K - 1):])

    stack = lambda i: jnp.stack(outs[i])
    return (yp.reshape(bp, tp, d), ys.reshape(bs, ts, d), stack(0), stack(1), stack(2), stack(3),
            stack(4), stack(5), stack(6), stack(7))
```

```python
import functools
import math

import jax
import jax.numpy as jnp
import numpy as np
from jax import lax
from jax.experimental import pallas as pl
from jax.experimental.pallas import tpu as pltpu

D_MODEL = 2048
N_HEADS = 16
HEAD_DIM = 64
N_KV = 4
HPG = N_HEADS // N_KV
ATTN_DIM = N_HEADS * HEAD_DIM
KV_DIM = N_KV * HEAD_DIM
CMP_LEN = 32
CMP_STRIDE = 16
CMP_R = CMP_LEN // CMP_STRIDE
PHI_HIDDEN = HEAD_DIM
SEL_BLOCK = 64
N_SELECT = 16
WINDOW = 512
Q_BLOCK = 64
CONV_DIM = D_MODEL // 2
CONV_K = 31
D_FF = 4 * D_MODEL
REL_BUCKETS = 32
REL_MAX_DIST = 128
EPS = 1e-6
NEG = -1e30
PAGE_SIZE = 128

LANES = 128
SUBLANES = 8
VMEM_LIMIT_BYTES = 56 * 1024 * 1024

NEAR_KEYS = 384
NEAR_VARIANTS = 6
FAR_TILE = 512
WIN_FAR_KEYS = 384
MASK_BIG = 1e30
ROW_TILE = 512
MIX_TILE_N = 1024
FFN_TILE_F = 1024
CONV_TILE_T = 256
ROW_CHUNK = 32
HALO = 32

_F32 = jnp.float32
_BF16 = jnp.bfloat16


def _params(*sem):
    return pltpu.CompilerParams(dimension_semantics=sem, vmem_limit_bytes=VMEM_LIMIT_BYTES)


def _dot(a, b):
    return jnp.dot(a, b, preferred_element_type=_F32)


def _dot_nt(a, b):
    return lax.dot_general(a, b, (((1,), (1,)), ((), ())), preferred_element_type=_F32)


def _split3(x):
    hi = x.astype(_BF16)
    r1 = x - hi.astype(_F32)
    mid = r1.astype(_BF16)
    lo = (r1 - mid.astype(_F32)).astype(_BF16)
    return hi, mid, lo


def _rms(x, g):
    return x * lax.rsqrt(jnp.mean(x * x, axis=-1, keepdims=True) + EPS) * g


def _rel_bucket_np(dist):
    n = np.maximum(dist, 0)
    exact = REL_BUCKETS // 2
    logb = exact + (np.log(np.maximum(n, 1).astype(np.float32) / np.float32(exact))
                    / np.float32(math.log(REL_MAX_DIST / exact)) * (REL_BUCKETS - exact)).astype(np.int32)
    return np.where(n < exact, n, np.minimum(logb, REL_BUCKETS - 1)).astype(np.int32)


PROJ_TILE = 512
PROJ_SEGMENTS = (("q", 0, 2), ("kvc", 2, 1), ("kvs", 3, 1), ("kvw", 4, 1), ("glu", 5, 4), ("mg", 9, 8))
PROJ_TILES = 17


def _proj_kernel(x_ref, g_ref, w_ref, wng_ref, *refs, transposed_v):
    h_ref = refs[-1]
    if transposed_v:
        (q_ref, kvc_ref, kvc16_ref, kvs_ref, ks16_ref, vst_ref, kvw_ref, kw16_ref, vwt_ref,
         ng_ref, u_ref, mg_ref) = refs[:-1]
    else:
        q_ref, kvc_ref, kvs_ref, kvw_ref, ng_ref, u_ref, mg_ref = refs[:-1]
    j = pl.program_id(1)
    seg = {name: (lo, lo + n) for name, lo, n in PROJ_SEGMENTS}
    inside = lambda name: (j >= seg[name][0]) & (j < seg[name][1])

    @pl.when(j == 0)
    def _():
        h_ref[...] = _rms(x_ref[...], g_ref[...]).astype(_BF16)
        ng_ref[...] = _dot(h_ref[...], wng_ref[...])

    tm = h_ref.shape[0]
    n_split = 2 if tm % (2 * LANES) == 0 else 1

    def segment(name, epilogue):
        @pl.when(inside(name))
        def _():
            for part in range(n_split):
                rs = slice(part * tm // n_split, (part + 1) * tm // n_split)
                epilogue(rs, _dot(h_ref[rs, :], w_ref[...]))

    def q_out(rs, acc):
        q_ref[rs, :] = (acc * (HEAD_DIM ** -0.5)).astype(_BF16)

    def kvc_out(rs, acc):
        if transposed_v:
            kvc_ref[0, :, rs] = acc.T
            kvc16_ref[rs, :] = acc.astype(_BF16)
        else:
            kvc_ref[rs, :] = acc

    def kv_out(f32_ref, k16_ref, vt_ref):
        def out(rs, acc):
            if transposed_v:
                acc_t = acc.T
                f32_ref[0, :, rs] = acc_t
                k16_ref[rs, :] = acc[:, :KV_DIM].astype(_BF16)
                vt_ref[0, :, rs] = acc_t[KV_DIM:, :].astype(_BF16)
            else:
                f32_ref[rs, :] = acc
        return out

    def glu_out(rs, acc):
        half = PROJ_TILE // 2
        u_ref[rs, :] = acc[:, :half] * jax.nn.sigmoid(acc[:, half:])

    def mg_out(rs, acc):
        mg_ref[rs, :] = jax.nn.sigmoid(acc).astype(_BF16)

    segment("q", q_out)
    segment("kvc", kvc_out)
    segment("kvs", kv_out(kvs_ref, ks16_ref if transposed_v else None, vst_ref if transposed_v else None))
    segment("kvw", kv_out(kvw_ref, kw16_ref if transposed_v else None, vwt_ref if transposed_v else None))
    segment("glu", glu_out)
    segment("mg", mg_out)


def _proj(x, g, w_all, wng, tm, seq_len=None):
    m, d = x.shape
    tn = PROJ_TILE
    assert w_all.shape == (d, PROJ_TILES * tn)
    transposed_v = seq_len is not None
    seg = {name: (lo, n) for name, lo, n in PROJ_SEGMENTS}

    def spec(name, width=tn):
        lo, n = seg[name]
        return pl.BlockSpec((tm, width), lambda i, j: (i, jnp.clip(j - lo, 0, n - 1)))

    f32 = lambda n: jax.ShapeDtypeStruct((m, n), _F32)
    b16 = lambda n: jax.ShapeDtypeStruct((m, n), _BF16)
    if transposed_v:
        assert seq_len % tm == 0
        spb = seq_len // tm
        nb = m // seq_len
        t_spec = lambda rows: pl.BlockSpec((1, rows, tm), lambda i, j: (i // spb, 0, i % spb))
        row_spec = lambda width: pl.BlockSpec((tm, width), lambda i, j: (i, 0))
        f32_t = jax.ShapeDtypeStruct((nb, tn, seq_len), _F32)
        kc_specs, kc_shapes = [t_spec(tn), row_spec(tn)], [f32_t, b16(tn)]
        kv_specs = [t_spec(tn), row_spec(KV_DIM), t_spec(KV_DIM)]
        kv_shapes = [f32_t, b16(KV_DIM), jax.ShapeDtypeStruct((nb, KV_DIM, seq_len), _BF16)]
        kw_specs, kw_shapes = kv_specs, kv_shapes
    else:
        kc_specs, kc_shapes = [spec("kvc")], [f32(tn)]
        kv_specs, kv_shapes = [spec("kvs")], [f32(tn)]
        kw_specs, kw_shapes = [spec("kvw")], [f32(tn)]
    out_specs = ([spec("q")] + kc_specs + kv_specs + kw_specs
                 + [pl.BlockSpec((tm, LANES), lambda i, j: (i, 0)), spec("glu", tn // 2), spec("mg")])
    out_shape = ([b16(seg["q"][1] * tn)] + kc_shapes + kv_shapes + kw_shapes
                 + [f32(LANES), f32(seg["glu"][1] * tn // 2), b16(seg["mg"][1] * tn)])
    outs = pl.pallas_call(
        functools.partial(_proj_kernel, transposed_v=transposed_v),
        grid=(m // tm, PROJ_TILES),
        in_specs=[pl.BlockSpec((tm, d), lambda i, j: (i, 0)),
                  pl.BlockSpec((1, d), lambda i, j: (0, 0)),
                  pl.BlockSpec((d, tn), lambda i, j: (0, j)),
                  pl.BlockSpec((d, LANES), lambda i, j: (0, 0))],
        out_specs=out_specs,
        out_shape=out_shape,
        scratch_shapes=[pltpu.VMEM((tm, d), _BF16)],
        compiler_params=_params("parallel", "arbitrary"),
        name="proj",
    )(x, g, w_all, wng)
    nkc, nkv = len(kc_specs), len(kv_specs)
    q2, kvc = outs[0], tuple(outs[1:1 + nkc])
    kvs, kvw = tuple(outs[1 + nkc:1 + nkc + nkv]), tuple(outs[1 + nkc + nkv:1 + nkc + 2 * nkv])
    ng, u, mg = outs[1 + nkc + 2 * nkv:]
    return q2, kvc, kvs, kvw, ng, u, mg


def _mix_kernel(a_ref, c_ref, ga_ref, gc_ref, wa_ref, wc_ref, o_ref):
    ya = _dot(a_ref[...], wa_ref[...])
    yc = _dot(c_ref[...], wc_ref[...])
    o_ref[...] = (ga_ref[...].astype(_F32) * ya + gc_ref[...].astype(_F32) * yc).astype(_BF16)


def _mix(attn, conv, mg, wao, wco, tm, tn):
    m, ka = attn.shape
    n = wao.shape[1]
    nb = n // tn
    return pl.pallas_call(
        _mix_kernel,
        grid=(m // tm, nb),
        in_specs=[pl.BlockSpec((tm, ka), lambda i, j: (i, 0)),
                  pl.BlockSpec((tm, conv.shape[1]), lambda i, j: (i, 0)),
                  pl.BlockSpec((tm, tn), lambda i, j: (i, j)),
                  pl.BlockSpec((tm, tn), lambda i, j: (i, j + nb)),
                  pl.BlockSpec((ka, tn), lambda i, j: (0, j)),
                  pl.BlockSpec((conv.shape[1], tn), lambda i, j: (0, j))],
        out_specs=pl.BlockSpec((tm, tn), lambda i, j: (i, j)),
        out_shape=jax.ShapeDtypeStruct((m, n), _BF16),
        compiler_params=_params("parallel", "arbitrary"),
        name="mix",
    )(attn, conv, mg, mg, wao, wco)


def _oproj_kernel(mx_ref, x_ref, w_ref, g_ref, o_ref):
    y = _dot(mx_ref[...], w_ref[...])
    o_ref[...] = x_ref[...] + _rms(y, g_ref[...])


def _oproj(mixed, x, wo, g, tm):
    m, d = x.shape
    return pl.pallas_call(
        _oproj_kernel,
        grid=(m // tm,),
        in_specs=[pl.BlockSpec((tm, d), lambda i: (i, 0)),
                  pl.BlockSpec((tm, d), lambda i: (i, 0)),
                  pl.BlockSpec((d, d), lambda i: (0, 0)),
                  pl.BlockSpec((1, d), lambda i: (0, 0))],
        out_specs=pl.BlockSpec((tm, d), lambda i: (i, 0)),
        out_shape=jax.ShapeDtypeStruct((m, d), _F32),
        compiler_params=_params("parallel"),
        name="oproj",
    )(mixed, x, wo, g)


def _ffn_kernel(x_ref, gpre_ref, gpost_ref, wu_ref, wd_ref, o_ref, h_ref, acc_ref):
    j = pl.program_id(1)

    @pl.when(j == 0)
    def _():
        h_ref[...] = _rms(x_ref[...], gpre_ref[...]).astype(_BF16)
        acc_ref[...] = jnp.zeros_like(acc_ref)

    a = jnp.maximum(_dot(h_ref[...], wu_ref[...]), 0.0)
    acc_ref[...] += _dot((a * a).astype(_BF16), wd_ref[...])

    @pl.when(j == pl.num_programs(1) - 1)
    def _():
        o_ref[...] = x_ref[...] + _rms(acc_ref[...], gpost_ref[...])


def _ffn(x, gpre, gpost, wu, wd, tm, tf):
    m, d = x.shape
    f = wu.shape[1]
    return pl.pallas_call(
        _ffn_kernel,
        grid=(m // tm, f // tf),
        in_specs=[pl.BlockSpec((tm, d), lambda i, j: (i, 0)),
                  pl.BlockSpec((1, d), lambda i, j: (0, 0)),
                  pl.BlockSpec((1, d), lambda i, j: (0, 0)),
                  pl.BlockSpec((d, tf), lambda i, j: (0, j)),
                  pl.BlockSpec((tf, d), lambda i, j: (j, 0))],
        out_specs=pl.BlockSpec((tm, d), lambda i, j: (i, 0)),
        out_shape=jax.ShapeDtypeStruct((m, d), _F32),
        scratch_shapes=[pltpu.VMEM((tm, d), _BF16), pltpu.VMEM((tm, d), _F32)],
        compiler_params=_params("parallel", "arbitrary"),
        name="ffn",
    )(x, gpre, gpost, wu, wd)


def _conv_kernel(u_ref, halo_ref, w_ref, b_ref, lg_ref, lb_ref, o_ref, win_ref, *, tt, zero_first):
    c = u_ref.shape[-1]
    halo = halo_ref[0]
    if zero_first:
        halo = jnp.where(pl.program_id(1) == 0, 0.0, halo)
    win_ref[0, 0:HALO, :] = halo
    win_ref[0, HALO:HALO + tt, :] = u_ref[0]
    span = HALO + tt - SUBLANES
    for s in range(1, SUBLANES):
        win_ref[s, 0:span, :] = win_ref[0, s:s + span, :]
    rc = min(ROW_CHUNK, tt)
    off = HALO - (CONV_K - 1)
    for ch in range(tt // rc):
        acc = jnp.zeros((rc, c), _F32) + b_ref[...]
        for k in range(CONV_K):
            s = (off + k) % SUBLANES
            row = ch * rc + off + k - s
            acc = acc + w_ref[k:k + 1, :] * win_ref[s, row:row + rc, :]
        mu = jnp.mean(acc, axis=-1, keepdims=True)
        xc = acc - mu
        var = jnp.mean(xc * xc, axis=-1, keepdims=True)
        y = xc * lax.rsqrt(var + EPS) * lg_ref[...] + lb_ref[...]
        o_ref[0, ch * rc:(ch + 1) * rc, :] = (y * jax.nn.sigmoid(y)).astype(_BF16)


def _conv(u, halo_src, w, b, lg, lb, tt, zero_first):
    bsz, t, c = u.shape
    nhb = tt // HALO
    if zero_first:
        halo_map = lambda bi, ti: (bi, jnp.maximum(ti * nhb - 1, 0), 0)
    else:
        halo_map = lambda bi, ti: (bi, 0, 0)
    kern = functools.partial(_conv_kernel, tt=tt, zero_first=zero_first)
    return pl.pallas_call(
        kern,
        grid=(bsz, t // tt),
        in_specs=[pl.BlockSpec((1, tt, c), lambda bi, ti: (bi, ti, 0)),
                  pl.BlockSpec((1, HALO, c), halo_map),
                  pl.BlockSpec((HALO, c), lambda bi, ti: (0, 0)),
                  pl.BlockSpec((1, c), lambda bi, ti: (0, 0)),
                  pl.BlockSpec((1, c), lambda bi, ti: (0, 0)),
                  pl.BlockSpec((1, c), lambda bi, ti: (0, 0))],
        out_specs=pl.BlockSpec((1, tt, c), lambda bi, ti: (bi, ti, 0)),
        out_shape=jax.ShapeDtypeStruct((bsz, t, c), _BF16),
        scratch_shapes=[pltpu.VMEM((SUBLANES, HALO + tt, c), _F32)],
        compiler_params=_params("parallel", "arbitrary"),
        name="conv",
    )(u, halo_src, w, b, lg, lb)


def _compress_rows(chunk_row, nrow, first_block, w1_ref, w2_ref, pe_ref, w1t_ref, carry_ref, out_refs):
    kc_ref, vc_ref, vct_ref = out_refs
    outs = []
    for c in range(2):
        acc = jnp.zeros((nrow, 2 * KV_DIM), _F32)
        for s in range(CMP_STRIDE):
            acc = acc + _dot(chunk_row(c, s).astype(_BF16), w1_ref[c, s])
        pt = jnp.sum(pe_ref[c] * w1t_ref[c], axis=0, keepdims=True)
        pt = jnp.concatenate([pt] * (KV_DIM // LANES), axis=-1)
        a0 = acc[:, :KV_DIM]
        a1 = acc[:, KV_DIM:]
        first = lax.broadcasted_iota(jnp.int32, (nrow, KV_DIM), 0) == 0
        prev = 0.0 if first_block is True else jnp.where(first_block, 0.0, carry_ref[c])
        a0s = jnp.where(first, prev, pltpu.roll(a0, 1, 0))
        carry_ref[c] = a0[nrow - 1:nrow, :]
        hid = jax.nn.gelu(a0s + a1 + pt)
        outs.append(_dot(hid.astype(_BF16), w2_ref[c]))
    kc_ref[0] = outs[0].astype(_BF16)
    vc_ref[0] = outs[1].astype(_BF16)
    vct_ref[0] = outs[1].T.astype(_BF16)


def _compress_kernel(src_ref, w1_ref, w2_ref, pe_ref, w1t_ref, kc_ref, vc_ref, vct_ref, carry_ref, *, nrow):
    chunk_row = lambda c, s: src_ref[0, :, s * 2 * KV_DIM + c * KV_DIM:s * 2 * KV_DIM + (c + 1) * KV_DIM]
    _compress_rows(chunk_row, nrow, True, w1_ref, w2_ref, pe_ref, w1t_ref, carry_ref, (kc_ref, vc_ref, vct_ref))


def _compress_paged_kernel(pt_ref, *refs, n_src, steps_per_row):
    del pt_ref
    src_refs = refs[:n_src]
    w1_ref, w2_ref, pe_ref, w1t_ref, kc_ref, vc_ref, vct_ref, carry_ref, rows_a, rows_b = refs[n_src:]
    t = pl.program_id(0)
    nrow = n_src * PAGE_SIZE // CMP_STRIDE
    first_block = (jnp.maximum(t - 1, 0) % steps_per_row) == 0

    @pl.when(t == 0)
    def _():
        rows_b[...] = jnp.zeros_like(rows_b)
        carry_ref[...] = jnp.zeros_like(carry_ref)

    def step(fill_ref, read_ref):
        for p, r in enumerate(src_refs):
            for lc in range(2 * KV_DIM // LANES):
                fill_ref[lc, p * PAGE_SIZE:(p + 1) * PAGE_SIZE, :] = r[0, lc * LANES:(lc + 1) * LANES, :].T

        def chunk_row(c, s):
            lcs = range(c * KV_DIM // LANES, (c + 1) * KV_DIM // LANES)
            return jnp.concatenate([read_ref[lc, pl.ds(s, nrow, stride=CMP_STRIDE), :] for lc in lcs], axis=-1)

        _compress_rows(chunk_row, nrow, first_block, w1_ref, w2_ref, pe_ref, w1t_ref, carry_ref,
                       (kc_ref, vc_ref, vct_ref))

    @pl.when(t % 2 == 0)
    def _():
        step(rows_a, rows_b)

    @pl.when(t % 2 == 1)
    def _():
        step(rows_b, rows_a)


def _compress_call(kern, grid, in_specs, out_specs, scratch, n_prefetch, n_batch, n_out_rows, name, args):
    grid_spec = pltpu.PrefetchScalarGridSpec(num_scalar_prefetch=n_prefetch, grid=grid, in_specs=in_specs,
                                             out_specs=out_specs, scratch_shapes=scratch)
    return pl.pallas_call(
        kern,
        grid_spec=grid_spec,
        out_shape=[jax.ShapeDtypeStruct((n_batch, n_out_rows, KV_DIM), _BF16)] * 2
                  + [jax.ShapeDtypeStruct((n_batch, KV_DIM, n_out_rows), _BF16)],
        compiler_params=_params(*(["arbitrary"] * len(grid))),
        name=name,
    )(*args)


def _const_spec(a):
    return pl.BlockSpec(a.shape, lambda *_, _n=a.ndim: (0,) * _n, pipeline_mode=pl.Buffered(1))


def _compress(src, w1bd, w2bd, pe_b, w1t):
    bsz, nrow, cols = src.shape
    consts = (w1bd, w2bd, pe_b, w1t)
    return _compress_call(
        functools.partial(_compress_kernel, nrow=nrow), (bsz,),
        [pl.BlockSpec((1, nrow, cols), lambda b: (b, 0, 0))] + [_const_spec(a) for a in consts],
        [pl.BlockSpec((1, nrow, KV_DIM), lambda b: (b, 0, 0))] * 2 + [pl.BlockSpec((1, KV_DIM, nrow), lambda b: (b, 0, 0))],
        [pltpu.VMEM((2, 1, KV_DIM), _F32)], 0, bsz, nrow, "compress", (src,) + consts)


def _compress_paged(pages, page_table, w1bd, w2bd, pe_b, w1t):
    bsz, n_pages = page_table.shape
    spr = n_pages // PAGES_PER_STEP
    n_steps = bsz * spr
    nrow = PAGES_PER_STEP * PAGE_SIZE // CMP_STRIDE
    consts = (w1bd, w2bd, pe_b, w1t)

    def page_spec(p):
        def index(t, pt):
            tc = jnp.minimum(t, n_steps - 1)
            return (pt[tc // spr, (tc % spr) * PAGES_PER_STEP + p], 0, 0)
        return pl.BlockSpec((1, 2 * KV_DIM, PAGE_SIZE), index)

    prev = lambda t: jnp.maximum(t - 1, 0)
    row_out = pl.BlockSpec((1, nrow, KV_DIM), lambda t, pt: (prev(t) // spr, prev(t) % spr, 0))
    col_out = pl.BlockSpec((1, KV_DIM, nrow), lambda t, pt: (prev(t) // spr, 0, prev(t) % spr))
    rows_buf = pltpu.VMEM((2 * KV_DIM // LANES, PAGES_PER_STEP * PAGE_SIZE, LANES), _F32)
    return _compress_call(
        functools.partial(_compress_paged_kernel, n_src=PAGES_PER_STEP, steps_per_row=spr), (n_steps + 1,),
        [page_spec(p) for p in range(PAGES_PER_STEP)] + [_const_spec(a) for a in consts],
        [row_out, row_out, col_out], [pltpu.VMEM((2, 1, KV_DIM), _F32), rows_buf, rows_buf],
        1, bsz, n_pages * PAGE_SIZE // CMP_STRIDE, "compress_paged",
        (page_table,) + (pages,) * PAGES_PER_STEP + consts)


def _threshold_select(sc, n_sel, tri):
    bits = pltpu.bitcast(sc, jnp.int32)
    key = jnp.where(bits < 0, bits ^ jnp.int32(0x7FFFFFFF), bits)
    count_ge = lambda t: jnp.sum(jnp.where(key >= t, 1.0, 0.0), axis=0, keepdims=True)
    t0 = jnp.where(count_ge(jnp.int32(0)) >= n_sel, jnp.int32(0), jnp.int32(-2 ** 31))

    def body(b, t):
        cand = t | jnp.left_shift(jnp.int32(1), 30 - b)
        return jnp.where(count_ge(cand) >= n_sel, cand, t)

    t = lax.fori_loop(0, 31, body, t0)
    above = jnp.where(key > t, 1.0, 0.0)
    tie = jnp.where(key == t, 1.0, 0.0)
    need = n_sel - jnp.sum(above, axis=0, keepdims=True)
    tie_rank = _dot(tri, tie.astype(_BF16))
    return above + jnp.where(tie_rank <= need, tie, 0.0)


def _softmax_parts(parts):
    ms = [jnp.max(jnp.where(mk, s, NEG), axis=-1, keepdims=True) for s, mk in parts]
    m = functools.reduce(jnp.maximum, ms)
    ps = [jnp.where(mk, jnp.exp(s - m), 0.0) for s, mk in parts]
    l = functools.reduce(lambda a, b: a + b, [jnp.sum(p, axis=-1, keepdims=True) for p in ps])
    inv = 1.0 / jnp.maximum(l, 1e-30)
    return [p * inv for p in ps]


def _softmax_cols(parts, always_valid):
    masked = [jnp.where(mk, s, NEG) for s, mk in parts]
    m = functools.reduce(jnp.maximum, [jnp.max(s, axis=0, keepdims=True) for s in masked])
    if always_valid:
        ps = [jnp.exp(s - m) for s in masked]
    else:
        ps = [jnp.where(mk, jnp.exp(s - m), 0.0) for s, mk in parts]
    l = functools.reduce(lambda a, b: a + b, [jnp.sum(p, axis=0, keepdims=True) for p in ps])
    return ps, 1.0 / jnp.maximum(l, 1e-30)


def _attn_prompt_kernel(q_ref, ng_ref, ks_ref, vst_ref, kw_ref, vwt_ref, kc_ref, vct_ref,
                        tzt_ref, tct_ref, ot_ref, gsel_ref, rsum_ref, rep_ref, tri_ref,
                        o_ref, qzt_ref, otacc_ref, gt_ref, nsel_ref, *, n_cmp_pad, seq_len):
    i = pl.program_id(1)
    qb = Q_BLOCK
    rows = HPG * qb
    lane_g = lax.broadcasted_iota(jnp.int32, (qb, KV_DIM), 1) // HEAD_DIM
    vrows = lambda g: pl.ds(g * HEAD_DIM, HEAD_DIM)

    for g in range(N_KV):
        qz = jnp.concatenate([jnp.where(lane_g == g, q_ref[0, :, r * KV_DIM:(r + 1) * KV_DIM].astype(_F32), 0.0)
                              for r in range(HPG)], axis=0)
        qzt_ref[g] = qz.T.astype(_BF16)

    gparts = _split3(jax.nn.sigmoid(ng_ref[0]))
    gcols = jnp.concatenate([sum(_dot(p, gsel_ref[r]) for p in gparts) for r in range(HPG)], axis=0)
    gt_ref[...] = gcols.T
    otacc_ref[...] = jnp.zeros_like(otacc_ref)

    def emit(branch, g, out_t):
        otacc_ref[g] += gt_ref[pl.ds(branch * N_KV + g, 1), :] * out_t

    jc = lax.broadcasted_iota(jnp.int32, (n_cmp_pad, rows), 0)
    lq = lax.broadcasted_iota(jnp.int32, (n_cmp_pad, rows), 1) % qb
    cmp_valid = (jc >= 1) & (CMP_STRIDE * jc + (CMP_LEN - CMP_STRIDE - 1) - lq <= i * qb)
    mm = lax.broadcasted_iota(jnp.int32, (n_cmp_pad, LANES), 1)
    jj = lax.broadcasted_iota(jnp.int32, (n_cmp_pad, LANES), 0)
    shift_t = jnp.where((mm < 16) & (jj - mm == 4 * i - 8), 1.0, 0.0).astype(_BF16)
    kc_aug = jnp.concatenate([kc_ref[0], shift_t, shift_t, shift_t], axis=1)
    groups = range(N_KV)
    s_c = [_dot(kc_aug, jnp.concatenate([qzt_ref[g], tct_ref[0, g], tct_ref[1, g], tct_ref[2, g]], axis=0))
           for g in groups]
    sm_c = [_softmax_cols([(s, cmp_valid)], False) for s in s_c]
    pn_c = [ps[0] for ps, _ in sm_c]
    inv_c = [inv for _, inv in sm_c]
    for g in groups:
        emit(0, g, _dot(vct_ref[0, vrows(g), :], pn_c[g].astype(_BF16)) * inv_c[g])
    y_c = [sum(_dot(ot_ref[...], p) for p in _split3(pn_c[g])) * inv_c[g] for g in groups]
    imp_t = sum(sum(_dot(p, rsum_ref[g]) for p in _split3(y_c[g])) for g in groups)

    jrow = lax.broadcasted_iota(jnp.int32, imp_t.shape, 0)
    forced = (jrow == 0) | (jrow == i) | (jrow == i - 1)
    score = jnp.where(forced, jnp.inf, jnp.where(jrow <= i, imp_t, -jnp.inf))
    sel_t = _threshold_select(score, N_SELECT, tri_ref[...])

    h = jnp.minimum(jnp.maximum(i - 3, 0) // 2, (seq_len - NEAR_KEYS) // LANES)
    ns = pl.multiple_of(h * LANES, LANES)
    v = i - 2 * h
    sel16 = sel_t.astype(_BF16)
    for g in groups:
        neg = (_dot(sel16, rep_ref[g]) - 1.0) * MASK_BIG
        nsel_ref[0, g] = neg
        nsel_ref[1, g] = jnp.where(jrow < 2 * h, neg, -MASK_BIG)

    def add_block_mask(s, far, g, j0, n_blocks):
        return jnp.concatenate([s[jj * SEL_BLOCK:(jj + 1) * SEL_BLOCK, :] + nsel_ref[far, g, pl.ds(j0 + jj, 1), :]
                                for jj in range(n_blocks)], axis=0)

    cn = lax.broadcasted_iota(jnp.int32, (NEAR_KEYS, rows), 0)
    lqn = lax.broadcasted_iota(jnp.int32, (NEAR_KEYS, rows), 1) % qb
    causal = cn - lqn <= v * SEL_BLOCK
    near = pl.ds(ns, NEAR_KEYS)
    n_far = (ns + FAR_TILE - 1) // FAR_TILE

    k_near = ks_ref[0, near, :]
    s_n = [add_block_mask(_dot(k_near, qzt_ref[g]) + tzt_ref[v, g], 0, g, 2 * h, NEAR_KEYS // SEL_BLOCK)
           for g in groups]
    s_n = [jnp.where(causal, s, NEG) for s in s_n]
    m0 = [jnp.max(s, axis=0, keepdims=True) for s in s_n]
    p_n = [jnp.exp(s_n[g] - m0[g]) for g in groups]
    l0 = [jnp.sum(p, axis=0, keepdims=True) for p in p_n]
    acc0 = [_dot(vst_ref[0, vrows(g), near], p_n[g].astype(_BF16)) for g in groups]

    def far_body(tau, carry):
        m_old, l_old, acc_old = carry
        k0 = pl.multiple_of(tau * FAR_TILE, FAR_TILE)
        tile = pl.ds(k0, FAR_TILE)
        kt = ks_ref[0, tile, :]
        s_f = [add_block_mask(_dot(kt, qzt_ref[g]), 1, g, tau * (FAR_TILE // SEL_BLOCK), FAR_TILE // SEL_BLOCK)
               for g in groups]
        m_new = [jnp.maximum(m_old[g], jnp.max(s_f[g], axis=0, keepdims=True)) for g in groups]
        alpha = [jnp.exp(m_old[g] - m_new[g]) for g in groups]
        p_f = [jnp.exp(s_f[g] - m_new[g]) for g in groups]
        l_new = [alpha[g] * l_old[g] + jnp.sum(p_f[g], axis=0, keepdims=True) for g in groups]
        acc = [alpha[g] * acc_old[g] + _dot(vst_ref[0, vrows(g), tile], p_f[g].astype(_BF16)) for g in groups]
        return tuple(m_new), tuple(l_new), tuple(acc)

    _, l1, acc1 = lax.fori_loop(0, n_far, far_body, (tuple(m0), tuple(l0), tuple(acc0)))
    for g in groups:
        emit(1, g, acc1[g] * (1.0 / l1[g]))

    fs = pl.multiple_of((jnp.maximum(i - WINDOW // SEL_BLOCK, 0) // 2) * LANES, LANES)
    wfar = pl.ds(fs, WIN_FAR_KEYS)
    cf = lax.broadcasted_iota(jnp.int32, (WIN_FAR_KEYS, rows), 0)
    lqf = lax.broadcasted_iota(jnp.int32, (WIN_FAR_KEYS, rows), 1) % qb
    far_ok = (i * qb + lqf - fs - cf < WINDOW) & (fs + cf < ns)
    kw_near, kw_far = kw_ref[0, near, :], kw_ref[0, wfar, :]
    s_wn = [_dot(kw_near, qzt_ref[g]) + tzt_ref[v, g] for g in groups]
    s_wf = [_dot(kw_far, qzt_ref[g]) for g in groups]
    p_w = [_softmax_cols([(s_wn[g], causal), (s_wf[g], far_ok)], True) for g in groups]
    for g in groups:
        (p_near, p_far), inv_w = p_w[g]
        emit(2, g, (_dot(vwt_ref[0, vrows(g), near], p_near.astype(_BF16))
                    + _dot(vwt_ref[0, vrows(g), wfar], p_far.astype(_BF16))) * inv_w)

    res = otacc_ref[...].reshape(N_KV * HEAD_DIM, rows).T
    for r in range(HPG):
        o_ref[0, :, r * KV_DIM:(r + 1) * KV_DIM] = res[r * qb:(r + 1) * qb, :].astype(_BF16)


def _attn_prompt(q2, ng, ks, vst, kw, vwt, kc, vct, tabs):
    bsz, t, _ = q2.shape
    nblk = t // SEL_BLOCK
    n_cmp_pad = kc.shape[1]
    rows = HPG * Q_BLOCK
    assert t % FAR_TILE == 0 and t >= NEAR_KEYS
    names = ["tzt", "tct", "ot", "gsel", "rsum", "rep", "tri"]
    full = lambda a: pl.BlockSpec(a.shape, lambda b, i, _n=a.ndim: (0,) * _n, pipeline_mode=pl.Buffered(1))
    per_b = lambda a: pl.BlockSpec((1,) + a.shape[1:], lambda b, i: (b, 0, 0))
    kern = functools.partial(_attn_prompt_kernel, n_cmp_pad=n_cmp_pad, seq_len=t)
    return pl.pallas_call(
        kern,
        grid=(bsz, nblk),
        in_specs=[pl.BlockSpec((1, Q_BLOCK, ATTN_DIM), lambda b, i: (b, i, 0)),
                  pl.BlockSpec((1, Q_BLOCK, LANES), lambda b, i: (b, i, 0)),
                  per_b(ks), per_b(vst), per_b(kw), per_b(vwt), per_b(kc), per_b(vct)]
                 + [full(tabs[n]) for n in names],
        out_specs=pl.BlockSpec((1, Q_BLOCK, ATTN_DIM), lambda b, i: (b, i, 0)),
        out_shape=jax.ShapeDtypeStruct((bsz, t, ATTN_DIM), _BF16),
        scratch_shapes=[pltpu.VMEM((N_KV, KV_DIM, rows), _BF16),
                        pltpu.VMEM((N_KV, HEAD_DIM, rows), _F32),
                        pltpu.VMEM((LANES, rows), _F32),
                        pltpu.VMEM((2, N_KV, SEL_BLOCK, rows), _F32)],
        compiler_params=_params("parallel", "arbitrary"),
        name="attn_prompt",
    )(q2, ng, ks, vst, kw, vwt, kc, vct, *[tabs[n] for n in names])


def _bias_table(rel, dist, head):
    nmax = max(int(dist.max()), 1) + 1
    bk = _rel_bucket_np(np.arange(nmax))
    rel_h = rel[:, head]
    out = jnp.broadcast_to(rel_h[0], np.broadcast_shapes(dist.shape, head.shape))
    dist = lax.optimization_barrier(jnp.asarray(dist, jnp.int32))
    for b in range(1, int(bk.max()) + 1):
        first = int(np.argmax(bk >= b))
        out = jnp.where(dist >= first, rel_h[b], out)
    return out


def _prompt_tables(rel_bias, t):
    nblk = t // SEL_BLOCK
    n_cmp_pad = t // CMP_STRIDE
    assert nblk <= SEL_BLOCK and n_cmp_pad % LANES == 0
    rows = HPG * Q_BLOCK
    rel = rel_bias.astype(_F32)
    r_idx = np.arange(rows) // Q_BLOCK
    q_idx = np.arange(rows) % Q_BLOCK
    head = np.arange(N_KV)[:, None] * HPG + r_idx[None, :]
    c31 = rel[REL_BUCKETS - 1][head]
    c = np.arange(NEAR_KEYS)
    dist = np.arange(NEAR_VARIANTS)[:, None, None] * SEL_BLOCK + q_idx[None, None, :] - c[None, :, None]
    tzt = _bias_table(rel, dist[:, None], head[None, :, None, :]) - c31[None, :, None, :]
    mmv = np.arange(16)
    dist_c = q_idx[None, :] - CMP_STRIDE * (mmv[:, None] - 8) - (CMP_LEN - CMP_STRIDE - 1)
    delta = _bias_table(rel, dist_c[None], head[:, None, :]) - c31[:, None, :]
    tct = jnp.stack(_split3(jnp.pad(delta, ((0, 0), (0, LANES - 16), (0, 0)))))
    n = np.arange(n_cmp_pad) - 1
    cs = n * CMP_STRIDE
    bs = np.arange(SEL_BLOCK) * SEL_BLOCK
    ov = np.clip(np.minimum(cs[None, :] + CMP_LEN, bs[:, None] + SEL_BLOCK) - np.maximum(cs[None, :], bs[:, None]),
                 0, CMP_LEN).astype(np.float32) / CMP_LEN
    ov[:, 0] = 0.0
    ov[nblk:, :] = 0.0
    gsel = np.zeros((HPG, LANES, LANES), np.float32)
    for j in range(3):
        for r in range(HPG):
            for g in range(N_KV):
                gsel[r, j * N_HEADS + r * N_KV + g, j * N_KV + g] = 1.0
    rsum = np.zeros((N_KV, rows, N_KV * Q_BLOCK), np.float32)
    for g in range(N_KV):
        rsum[g, np.arange(rows), g * Q_BLOCK + q_idx] = 1.0
    bf = lambda a: jnp.asarray(a, _BF16)
    return dict(tzt=tzt, tct=tct, ot=bf(ov), gsel=bf(gsel), rsum=bf(rsum),
                tri=bf(np.tril(np.ones((SEL_BLOCK, SEL_BLOCK), np.float32))),
                rep=bf(rsum.transpose(0, 2, 1)))


def _gate_expand():
    ex = np.zeros((LANES, 3 * ATTN_DIM), np.float32)
    for j in range(3):
        for r in range(HPG):
            for g in range(N_KV):
                col = j * ATTN_DIM + r * KV_DIM + g * HEAD_DIM
                ex[j * N_HEADS + r * N_KV + g, col:col + HEAD_DIM] = 1.0
    return jnp.asarray(ex, _BF16)


PAGES_PER_STEP = 32
SUB_PAGES = 32


def _attn_sample_kernel(pt_ref, q_ref, ng_ref, kc_ref, vc_ref, knew_ref, wnew_ref, wst_ref, *refs,
                        past_len, n_blk, nb_rows):
    del pt_ref
    page_refs = refs[:PAGES_PER_STEP]
    (bc_ref, blast_ref, c31_ref, bnew_ref, bwin_ref, ot_ref, e64_ref, ex_ref, tri_ref,
     o_ref, qall_ref, gate_ref, oacc_ref, acc_ref, m_ref, l_ref, selt_ref) = refs[PAGES_PER_STEP:]
    k = pl.program_id(1)
    nk = pl.num_programs(1)
    tq = q_ref.shape[1]
    rows = N_KV * HPG * tq
    lane_g = lax.broadcasted_iota(jnp.int32, (tq, KV_DIM), 1) // HEAD_DIM
    rq = lax.broadcasted_iota(jnp.int32, (rows, 1), 0) % tq
    sub_keys = SUB_PAGES * PAGE_SIZE

    def emit(branch, out):
        for g in range(N_KV):
            for r in range(HPG):
                col = r * KV_DIM
                row0 = (g * HPG + r) * tq
                gt = gate_ref[:, branch * ATTN_DIM + col:branch * ATTN_DIM + col + KV_DIM]
                oacc_ref[:, col:col + KV_DIM] += jnp.where(lane_g == g, gt * out[row0:row0 + tq, :], 0.0)

    def pad_rows(x, n):
        if n == x.shape[0]:
            return x
        return jnp.concatenate([x, jnp.zeros((n - x.shape[0], x.shape[1]), x.dtype)], axis=0)

    @pl.when(k == 0)
    def _():
        qf = q_ref[0].astype(_F32)
        pieces = []
        for g in range(N_KV):
            for r in range(HPG):
                pieces.append(jnp.where(lane_g == g, qf[:, r * KV_DIM:(r + 1) * KV_DIM], 0.0))
        qall = jnp.concatenate(pieces, axis=0).astype(_BF16)
        qall_ref[...] = qall
        gs = jax.nn.sigmoid(ng_ref[0])
        gate_ref[...] = sum(_dot(p, ex_ref[...]) for p in _split3(gs))
        oacc_ref[...] = jnp.zeros_like(oacc_ref)

        n_cmp_pad = kc_ref.shape[1]
        jc = lax.broadcasted_iota(jnp.int32, (rows, n_cmp_pad), 1)
        cmp_valid = (jc >= 1) & (CMP_STRIDE * jc + (CMP_LEN - CMP_STRIDE - 1) - rq <= past_len)
        s = _dot_nt(qall, kc_ref[0]) + bc_ref[...]
        (pn,) = _softmax_parts([(s, cmp_valid)])
        emit(0, _dot(pn.astype(_BF16), vc_ref[0]))
        imp_rows = []
        for g in range(N_KV):
            sg = sum(pn[(g * HPG + r) * tq:(g * HPG + r + 1) * tq, :] for r in range(HPG))
            imp_rows += [sg] * HPG
        imp = jnp.concatenate(imp_rows, axis=0)
        imp_t = sum(_dot_nt(ot_ref[...], p) for p in _split3(imp))

        jrow = lax.broadcasted_iota(jnp.int32, imp_t.shape, 0)
        tpos = past_len + lax.broadcasted_iota(jnp.int32, imp_t.shape, 1) % tq
        cur = tpos // SEL_BLOCK
        forced = (jrow == 0) | (jrow == cur) | (jrow == cur - 1)
        valid = jrow * SEL_BLOCK <= tpos
        score = jnp.where(forced, jnp.inf, jnp.where(valid, imp_t, -jnp.inf))
        selt_ref[...] = pad_rows(_threshold_select(score, min(N_SELECT, n_blk), tri_ref[...]), selt_ref.shape[0])

        kn = pad_rows(knew_ref[0, :, :KV_DIM], LANES).astype(_BF16)
        vn = pad_rows(knew_ref[0, :, KV_DIM:], LANES).astype(_BF16)
        cn = lax.broadcasted_iota(jnp.int32, (rows, LANES), 1)
        mk = (cn <= rq) & (cn < tq)
        s = _dot_nt(qall, kn) + bnew_ref[...]
        m = jnp.max(jnp.where(mk, s, NEG), axis=-1, keepdims=True)
        p = jnp.where(mk, jnp.exp(s - m), 0.0)
        m_ref[...] = jnp.broadcast_to(m, m_ref.shape)
        l_ref[...] = jnp.broadcast_to(jnp.sum(p, axis=-1, keepdims=True), l_ref.shape)
        acc_ref[...] = _dot(p.astype(_BF16), vn)

    qall = qall_ref[...]
    blk_per_step = PAGES_PER_STEP * PAGE_SIZE // SEL_BLOCK
    j0 = pl.multiple_of(k * blk_per_step, blk_per_step)
    sel_step = selt_ref[pl.ds(j0, LANES), :].T.astype(_BF16)
    c31 = jnp.concatenate([c31_ref[...]] * (sub_keys // LANES), axis=-1)
    n_sub = PAGES_PER_STEP // SUB_PAGES
    for st in range(n_sub):
        pages = page_refs[st * SUB_PAGES:(st + 1) * SUB_PAGES]
        kt = jnp.concatenate([r[0, :KV_DIM, :] for r in pages], axis=1).astype(_BF16)
        vt = jnp.concatenate([r[0, KV_DIM:, :] for r in pages], axis=1).astype(_BF16)
        mk = _dot(sel_step, e64_ref[:, st * sub_keys:(st + 1) * sub_keys]) > 0.5
        if st == n_sub - 1:
            bias = jnp.where(k == nk - 1, blast_ref[...], c31)
        else:
            bias = c31
        s = jnp.where(mk, _dot(qall, kt) + bias, NEG)
        m_old = m_ref[:, :1]
        m_new = jnp.maximum(m_old, jnp.max(s, axis=-1, keepdims=True))
        alpha = jnp.exp(m_old - m_new)
        p = jnp.exp(s - m_new)
        l_ref[...] = jnp.broadcast_to(alpha * l_ref[:, :1] + jnp.sum(p, axis=-1, keepdims=True), l_ref.shape)
        m_ref[...] = jnp.broadcast_to(m_new, m_ref.shape)
        acc_ref[...] = alpha * acc_ref[...] + _dot_nt(p.astype(_BF16), vt)

    @pl.when(k == nk - 1)
    def _():
        emit(1, acc_ref[...] * (1.0 / l_ref[:, :1]))
        lw = wst_ref.shape[2]
        kw = wst_ref[0, :KV_DIM, :].astype(_BF16)
        vw = wst_ref[0, KV_DIM:, :].astype(_BF16)
        kn = pad_rows(wnew_ref[0, :, :KV_DIM], LANES).astype(_BF16)
        vn = pad_rows(wnew_ref[0, :, KV_DIM:], LANES).astype(_BF16)
        cw = lax.broadcasted_iota(jnp.int32, (rows, lw), 1)
        dw = lw + rq - cw
        cn = lax.broadcasted_iota(jnp.int32, (rows, LANES), 1)
        pw, pnw = _softmax_parts([(_dot(qall, kw) + bwin_ref[...], (dw >= 0) & (dw < WINDOW)),
                                  (_dot_nt(qall, kn) + bnew_ref[...], (cn <= rq) & (cn < tq))])
        emit(2, _dot_nt(pw.astype(_BF16), vw) + _dot(pnw.astype(_BF16), vn))
        o_ref[0] = oacc_ref[...].astype(_BF16)


def _page_specs():
    return [pl.BlockSpec((1, 2 * KV_DIM, PAGE_SIZE),
                         lambda b, k, pt, _p=p: (pt[b, k * PAGES_PER_STEP + _p], 0, 0))
            for p in range(PAGES_PER_STEP)]


def _transposed_rows(a):
    n, rows = a.shape[:2]
    return jnp.transpose(a, (0, 2, 3, 4, 1)).reshape(n, 2 * KV_DIM, rows)


def _attn_sample(q2, ng, kc, vc, kvs_new, kvw_new, win_state, slc_pages, page_table, tabs, past_len):
    bsz, tq, _ = q2.shape
    n_pages = page_table.shape[1]
    assert n_pages % PAGES_PER_STEP == 0 and past_len == n_pages * PAGE_SIZE and past_len % SEL_BLOCK == 0
    n_steps = n_pages // PAGES_PER_STEP
    rows = N_KV * HPG * tq
    assert rows == LANES
    n_blk = -(-(past_len + tq) // SEL_BLOCK)
    nb_rows = tabs["ot"].shape[0]
    blk_per_step = PAGES_PER_STEP * PAGE_SIZE // SEL_BLOCK
    selt_rows = max((n_steps - 1) * blk_per_step + LANES, nb_rows)
    full = lambda a: pl.BlockSpec(a.shape, lambda b, k, pt, _n=a.ndim: (0,) * _n, pipeline_mode=pl.Buffered(1))
    per_b = lambda a: pl.BlockSpec((1,) + a.shape[1:], lambda b, k, pt: (b, 0, 0))
    page_specs = _page_specs()
    names = ["bc", "blast", "c31", "bnew", "bwin", "ot", "e64", "ex", "tri"]
    kern = functools.partial(_attn_sample_kernel, past_len=past_len, n_blk=n_blk, nb_rows=nb_rows)
    grid_spec = pltpu.PrefetchScalarGridSpec(
        num_scalar_prefetch=1,
        grid=(bsz, n_steps),
        in_specs=[per_b(q2), per_b(ng), per_b(kc), per_b(vc), per_b(kvs_new), per_b(kvw_new), per_b(win_state)]
                 + page_specs + [full(tabs[n]) for n in names],
        out_specs=pl.BlockSpec((1, tq, ATTN_DIM), lambda b, k, pt: (b, 0, 0)),
        scratch_shapes=[pltpu.VMEM((rows, KV_DIM), _BF16),
                        pltpu.VMEM((tq, 3 * ATTN_DIM), _F32),
                        pltpu.VMEM((tq, ATTN_DIM), _F32),
                        pltpu.VMEM((rows, KV_DIM), _F32),
                        pltpu.VMEM((rows, LANES), _F32),
                        pltpu.VMEM((rows, LANES), _F32),
                        pltpu.VMEM((selt_rows, rows), _F32)],
    )
    return pl.pallas_call(
        kern,
        grid_spec=grid_spec,
        out_shape=jax.ShapeDtypeStruct((bsz, tq, ATTN_DIM), _BF16),
        compiler_params=_params("parallel", "arbitrary"),
        name="attn_sample",
    )(page_table, q2, ng, kc, vc, kvs_new, kvw_new, win_state, *([slc_pages] * PAGES_PER_STEP),
      *[tabs[n] for n in names])


def _sample_tables(rel_bias, past_len, tq, lw, n_cmp_pad):
    rows = N_KV * HPG * tq
    rel = rel_bias.astype(_F32)
    ridx = np.arange(rows)
    head = ridx // tq
    qi = ridx % tq
    tpos = past_len + qi

    bias_of = lambda dist: _bias_table(rel, dist, head[:, None])

    jc = np.arange(n_cmp_pad)
    bc = bias_of(tpos[:, None] - (CMP_STRIDE * jc[None, :] + CMP_LEN - CMP_STRIDE - 1))
    sub_keys = SUB_PAGES * PAGE_SIZE
    blast = bias_of(tpos[:, None] - (past_len - sub_keys + np.arange(sub_keys))[None, :])
    bnew = bias_of(qi[:, None] - np.arange(LANES)[None, :])
    bwin = bias_of(lw + qi[:, None] - np.arange(lw)[None, :])
    c31 = jnp.broadcast_to(rel[REL_BUCKETS - 1][head][:, None], (rows, LANES))
    n_blk = -(-(past_len + tq) // SEL_BLOCK)
    nb_rows = -(-n_blk // LANES) * LANES
    n = jc - 1
    cs = n * CMP_STRIDE
    bs = np.arange(nb_rows) * SEL_BLOCK
    ov = np.clip(np.minimum(cs[None, :] + CMP_LEN, bs[:, None] + SEL_BLOCK) - np.maximum(cs[None, :], bs[:, None]),
                 0, CMP_LEN).astype(np.float32) / CMP_LEN
    ov[:, 0] = 0.0
    ov[n_blk:, :] = 0.0
    step_keys = PAGES_PER_STEP * PAGE_SIZE
    e64 = (np.arange(step_keys)[None, :] // SEL_BLOCK == np.arange(LANES)[:, None]).astype(np.float32)
    tri = np.tril(np.ones((nb_rows, nb_rows), np.float32))
    return dict(bc=bc, blast=blast, c31=c31, bnew=bnew, bwin=bwin, ot=jnp.asarray(ov, _BF16),
                e64=jnp.asarray(e64, _BF16), ex=_gate_expand(), tri=jnp.asarray(tri, _BF16))


def _layer_weights(w_in, phi_pe, phi_w1, phi_w2, w_attn_out, conv_w, w_conv_out, w_o, w_up, w_down):
    d = w_in.shape[0]
    o_q, o_kc, o_ks, o_kw = 0, ATTN_DIM, ATTN_DIM + 2 * KV_DIM, ATTN_DIM + 4 * KV_DIM
    o_ng = ATTN_DIM + 6 * KV_DIM
    o_glu = o_ng + 3 * N_HEADS
    o_mg = o_glu + 2 * CONV_DIM
    bf = lambda a: a.astype(_BF16)
    wq = w_in[:, o_q:o_kc].reshape(d, N_KV, HPG, HEAD_DIM).transpose(0, 2, 1, 3).reshape(d, ATTN_DIM)
    wng = w_in[:, o_ng:o_glu].reshape(d, N_KV, HPG, 3).transpose(0, 3, 2, 1).reshape(d, 3 * N_HEADS)
    wng = jnp.pad(wng, ((0, 0), (0, LANES - 3 * N_HEADS)))
    w5 = phi_w1.reshape(2, CMP_R, CMP_STRIDE, HEAD_DIM, PHI_HIDDEN)
    eye = jnp.eye(N_KV, dtype=_F32)
    w1bd = jnp.einsum("crsde,gh->csgdrhe", w5, eye).reshape(2, CMP_STRIDE, KV_DIM, CMP_R * KV_DIM)
    w2bd = jnp.einsum("che,gk->cghke", phi_w2, eye).reshape(2, KV_DIM, KV_DIM)
    rep = LANES // PHI_HIDDEN
    pe_b = jnp.broadcast_to(phi_pe.reshape(2, CMP_LEN * HEAD_DIM, 1), (2, CMP_LEN * HEAD_DIM, LANES))
    w1t = jnp.tile(phi_w1, (1, 1, rep))
    wao = w_attn_out.reshape(N_KV, HPG, HEAD_DIM, d).transpose(1, 0, 2, 3).reshape(ATTN_DIM, d)
    half = PROJ_TILE // 2
    wga = w_in[:, o_glu:o_glu + CONV_DIM].reshape(d, CONV_DIM // half, half)
    wgb = w_in[:, o_glu + CONV_DIM:o_mg].reshape(d, CONV_DIM // half, half)
    wglu = jnp.concatenate([wga, wgb], axis=2).reshape(d, 2 * CONV_DIM)
    w_all = jnp.concatenate([wq, w_in[:, o_kc:o_ng], wglu, w_in[:, o_mg:]], axis=1)
    return dict(
        w_all=bf(w_all), wng=bf(wng), w1bd=bf(w1bd), w2bd=bf(w2bd), pe_b=pe_b, w1t=w1t,
        wao=bf(wao), wco=bf(w_conv_out), wo=bf(w_o), wup=bf(w_up), wdown=bf(w_down),
        conv_w=jnp.pad(conv_w, ((0, HALO - CONV_K), (0, 0))))


def _finish(x2, attn2, conv, mg, w, g_post_mix, g_pre_ffn, g_post_ffn, tm):
    mixed = _mix(attn2, conv, mg, w["wao"], w["wco"], tm, MIX_TILE_N)
    x1 = _oproj(mixed, x2, w["wo"], g_post_mix, tm)
    return _ffn(x1, g_pre_ffn, g_post_ffn, w["wup"], w["wdown"], tm, FFN_TILE_F)


def kernel(x_prompt, x_sample, cache_kv_cmp, cache_kv_slc, state_kv_win, state_conv, page_table, w_in, phi_pe,
           phi_w1, phi_w2, rel_bias, w_attn_out, conv_w, conv_b, conv_ln_g, conv_ln_b, w_conv_out, w_o, w_up,
           w_down, g_pre_mix, g_post_mix, g_pre_ffn, g_post_ffn):
    depth = w_in.shape[0]
    bp, tp, d = x_prompt.shape
    bs, ts, _ = x_sample.shape
    n_pages = page_table.shape[1]
    past_len = n_pages * PAGE_SIZE
    lw = state_kv_win.shape[2]
    chunk_cols = CMP_STRIDE * 2 * KV_DIM
    assert ts < CMP_STRIDE and tp % CMP_STRIDE == 0 and lw == WINDOW and tp >= WINDOW

    tabs_p = _prompt_tables(rel_bias, tp)
    tabs_s = _sample_tables(rel_bias, past_len, ts, lw, past_len // CMP_STRIDE)
    yp, ys = x_prompt.reshape(bp * tp, d), x_sample.reshape(bs * ts, d)
    outs = [[] for _ in range(8)]
    row = lambda a: a.reshape(1, -1)
    kv5 = lambda a, b, t: a.reshape(b, t, 2, N_KV, HEAD_DIM)
    kv5_t = lambda a: jnp.transpose(a.reshape(a.shape[0], 2, N_KV, HEAD_DIM, a.shape[2]), (0, 4, 1, 2, 3))
    for l in range(depth):
        w = _layer_weights(w_in[l], phi_pe[l], phi_w1[l], phi_w2[l], w_attn_out[l], conv_w[l], w_conv_out[l],
                           w_o[l], w_up[l], w_down[l])
        gpm, gqm, gpf, gqf = row(g_pre_mix[l]), row(g_post_mix[l]), row(g_pre_ffn[l]), row(g_post_ffn[l])
        cargs = (w["conv_w"], row(conv_b[l]), row(conv_ln_g[l]), row(conv_ln_b[l]))
        cmp_w = (w["w1bd"], w["w2bd"], w["pe_b"], w["w1t"])

        tm = ROW_TILE
        q2, (kvc_t, kvc16), (kvs_t, ks16, vst16), (kvw_t, kw16, vwt16), ng, u, mg = _proj(
            yp, gpm, w["w_all"], w["wng"], tm, tp)
        n_chunk = tp // CMP_STRIDE
        kc, _, vct = _compress(kvc16.reshape(bp, n_chunk, chunk_cols), *cmp_w)
        attn2 = _attn_prompt(q2.reshape(bp, tp, ATTN_DIM), ng.reshape(bp, tp, LANES),
                             ks16.reshape(bp, tp, KV_DIM), vst16, kw16.reshape(bp, tp, KV_DIM), vwt16,
                             kc, vct, tabs_p)
        u3 = u.reshape(bp, tp, CONV_DIM)
        conv = _conv(u3, u3, *cargs, CONV_TILE_T, True)
        yp = _finish(yp, attn2.reshape(bp * tp, ATTN_DIM), conv.reshape(bp * tp, CONV_DIM), mg, w, gqm, gpf, gqf, tm)
        outs[0].append(kv5_t(kvc_t))
        outs[2].append(kv5_t(kvs_t))
        outs[4].append(kv5_t(kvw_t[:, :, tp - WINDOW:]))
        outs[6].append(u3[:, tp - (CONV_K - 1):])

        tm = bs * ts
        q2, (kvc,), (kvs,), (kvw,), ng, u, mg = _proj(ys, gpm, w["w_all"], w["wng"], tm)
        kc, vc, _ = _compress_paged(_transposed_rows(cache_kv_cmp[l]), page_table, *cmp_w)
        attn2 = _attn_sample(q2.reshape(bs, ts, ATTN_DIM), ng.reshape(bs, ts, LANES), kc, vc,
                             kvs.reshape(bs, ts, 2 * KV_DIM), kvw.reshape(bs, ts, 2 * KV_DIM),
                             _transposed_rows(state_kv_win[l]), _transposed_rows(cache_kv_slc[l]),
                             page_table, tabs_s, past_len)
        u3 = u.reshape(bs, ts, CONV_DIM)
        hist = jnp.pad(state_conv[l], ((0, 0), (HALO - (CONV_K - 1), 0), (0, 0)))
        conv = _conv(u3, hist, *cargs, ts, False)
        ys = _finish(ys, attn2.reshape(bs * ts, ATTN_DIM), conv.reshape(bs * ts, CONV_DIM), mg, w, gqm, gpf, gqf, tm)
        outs[1].append(kv5(kvc, bs, ts))
        outs[3].append(kv5(kvs, bs, ts))
        win_rows = jnp.concatenate([state_kv_win[l], kv5(kvw, bs, ts)], axis=1)
        outs[5].append(win_rows[:, win_rows.shape[1] - min(WINDOW, win_rows.shape[1]):])
        up = jnp.concatenate([state_conv[l], u3], axis=1)
        outs[7].append(up[:, up.shape[1] - (CONV_K - 1):])

    stack = lambda i: jnp.stack(outs[i])
    return (yp.reshape(bp, tp, d), ys.reshape(bs, ts, d), stack(0), stack(1), stack(2), stack(3),
            stack(4), stack(5), stack(6), stack(7))
```

```python
import functools
import math

import jax
import jax.numpy as jnp
import numpy as np
from jax import lax
from jax.experimental import pallas as pl
from jax.experimental.pallas import tpu as pltpu

D_MODEL = 2048
N_HEADS = 16
HEAD_DIM = 64
N_KV = 4
HPG = N_HEADS // N_KV
ATTN_DIM = N_HEADS * HEAD_DIM
KV_DIM = N_KV * HEAD_DIM
CMP_LEN = 32
CMP_STRIDE = 16
CMP_R = CMP_LEN // CMP_STRIDE
PHI_HIDDEN = HEAD_DIM
SEL_BLOCK = 64
N_SELECT = 16
WINDOW = 512
Q_BLOCK = 64
CONV_DIM = D_MODEL // 2
CONV_K = 31
D_FF = 4 * D_MODEL
REL_BUCKETS = 32
REL_MAX_DIST = 128
EPS = 1e-6
NEG = -1e30
PAGE_SIZE = 128

LANES = 128
SUBLANES = 8
VMEM_LIMIT_BYTES = 56 * 1024 * 1024

NEAR_KEYS = 384
NEAR_VARIANTS = 6
FAR_TILE = 1024
WIN_FAR_KEYS = 384
MASK_BIG = 1e30
ROW_TILE = 512
PROJ_ROW_TILE = 1024
MIX_TILE_N = 1024
FFN_TILE_F = 1024
CONV_TILE_T = 256
ROW_CHUNK = 32
HALO = 32

_F32 = jnp.float32
_BF16 = jnp.bfloat16


def _params(*sem):
    return pltpu.CompilerParams(dimension_semantics=sem, vmem_limit_bytes=VMEM_LIMIT_BYTES)


def _dot(a, b):
    return jnp.dot(a, b, preferred_element_type=_F32)


def _dot_nt(a, b):
    return lax.dot_general(a, b, (((1,), (1,)), ((), ())), preferred_element_type=_F32)


def _split3(x):
    hi = x.astype(_BF16)
    r1 = x - hi.astype(_F32)
    mid = r1.astype(_BF16)
    lo = (r1 - mid.astype(_F32)).astype(_BF16)
    return hi, mid, lo


def _rms(x, g):
    return x * lax.rsqrt(jnp.mean(x * x, axis=-1, keepdims=True) + EPS) * g


def _rel_bucket_np(dist):
    n = np.maximum(dist, 0)
    exact = REL_BUCKETS // 2
    logb = exact + (np.log(np.maximum(n, 1).astype(np.float32) / np.float32(exact))
                    / np.float32(math.log(REL_MAX_DIST / exact)) * (REL_BUCKETS - exact)).astype(np.int32)
    return np.where(n < exact, n, np.minimum(logb, REL_BUCKETS - 1)).astype(np.int32)


PROJ_TILE = 512
PROJ_SEGMENTS = (("q", 0, 2), ("kvc", 2, 1), ("kvs", 3, 1), ("kvw", 4, 1), ("glu", 5, 4), ("mg", 9, 8))
PROJ_TILES = 17


def _proj_kernel(x_ref, g_ref, w_ref, wng_ref, *refs, transposed_v):
    h_ref = refs[-1]
    if transposed_v:
        (q_ref, kvc_ref, kvc16_ref, kvs_ref, ks16_ref, vst_ref, kvw_ref, kw16_ref, vwt_ref,
         ng_ref, u_ref, mg_ref) = refs[:-1]
    else:
        q_ref, kvc_ref, kvs_ref, kvw_ref, ng_ref, u_ref, mg_ref = refs[:-1]
    j = pl.program_id(1)
    seg = {name: (lo, lo + n) for name, lo, n in PROJ_SEGMENTS}
    inside = lambda name: (j >= seg[name][0]) & (j < seg[name][1])

    @pl.when(j == 0)
    def _():
        h_ref[...] = _rms(x_ref[...], g_ref[...]).astype(_BF16)
        ng_ref[...] = _dot(h_ref[...], wng_ref[...])

    tm = h_ref.shape[0]
    n_split = 2 if tm % (2 * LANES) == 0 else 1

    def segment(name, epilogue):
        @pl.when(inside(name))
        def _():
            for part in range(n_split):
                rs = slice(part * tm // n_split, (part + 1) * tm // n_split)
                epilogue(rs, _dot(h_ref[rs, :], w_ref[...]))

    def q_out(rs, acc):
        q_ref[rs, :] = (acc * (HEAD_DIM ** -0.5)).astype(_BF16)

    def kvc_out(rs, acc):
        if transposed_v:
            kvc_ref[0, :, rs] = acc.T
            kvc16_ref[rs, :] = acc.astype(_BF16)
        else:
            kvc_ref[rs, :] = acc

    def kv_out(f32_ref, k16_ref, vt_ref):
        def out(rs, acc):
            if transposed_v:
                acc_t = acc.T
                f32_ref[0, :, rs] = acc_t
                k16_ref[rs, :] = acc[:, :KV_DIM].astype(_BF16)
                vt_ref[0, :, rs] = acc_t[KV_DIM:, :].astype(_BF16)
            else:
                f32_ref[rs, :] = acc
        return out

    def glu_out(rs, acc):
        half = PROJ_TILE // 2
        u_ref[rs, :] = acc[:, :half] * jax.nn.sigmoid(acc[:, half:])

    def mg_out(rs, acc):
        mg_ref[rs, :] = jax.nn.sigmoid(acc).astype(_BF16)

    segment("q", q_out)
    segment("kvc", kvc_out)
    segment("kvs", kv_out(kvs_ref, ks16_ref if transposed_v else None, vst_ref if transposed_v else None))
    segment("kvw", kv_out(kvw_ref, kw16_ref if transposed_v else None, vwt_ref if transposed_v else None))
    segment("glu", glu_out)
    segment("mg", mg_out)


def _proj(x, g, w_all, wng, tm, seq_len=None):
    m, d = x.shape
    tn = PROJ_TILE
    assert w_all.shape == (d, PROJ_TILES * tn)
    transposed_v = seq_len is not None
    seg = {name: (lo, n) for name, lo, n in PROJ_SEGMENTS}

    def spec(name, width=tn):
        lo, n = seg[name]
        return pl.BlockSpec((tm, width), lambda i, j: (i, jnp.clip(j - lo, 0, n - 1)))

    f32 = lambda n: jax.ShapeDtypeStruct((m, n), _F32)
    b16 = lambda n: jax.ShapeDtypeStruct((m, n), _BF16)
    if transposed_v:
        assert seq_len % tm == 0
        spb = seq_len // tm
        nb = m // seq_len
        t_spec = lambda rows: pl.BlockSpec((1, rows, tm), lambda i, j: (i // spb, 0, i % spb))
        row_spec = lambda width: pl.BlockSpec((tm, width), lambda i, j: (i, 0))
        f32_t = jax.ShapeDtypeStruct((nb, tn, seq_len), _F32)
        kc_specs, kc_shapes = [t_spec(tn), row_spec(tn)], [f32_t, b16(tn)]
        kv_specs = [t_spec(tn), row_spec(KV_DIM), t_spec(KV_DIM)]
        kv_shapes = [f32_t, b16(KV_DIM), jax.ShapeDtypeStruct((nb, KV_DIM, seq_len), _BF16)]
        kw_specs, kw_shapes = kv_specs, kv_shapes
    else:
        kc_specs, kc_shapes = [spec("kvc")], [f32(tn)]
        kv_specs, kv_shapes = [spec("kvs")], [f32(tn)]
        kw_specs, kw_shapes = [spec("kvw")], [f32(tn)]
    out_specs = ([spec("q")] + kc_specs + kv_specs + kw_specs
                 + [pl.BlockSpec((tm, LANES), lambda i, j: (i, 0)), spec("glu", tn // 2), spec("mg")])
    out_shape = ([b16(seg["q"][1] * tn)] + kc_shapes + kv_shapes + kw_shapes
                 + [f32(LANES), f32(seg["glu"][1] * tn // 2), b16(seg["mg"][1] * tn)])
    outs = pl.pallas_call(
        functools.partial(_proj_kernel, transposed_v=transposed_v),
        grid=(m // tm, PROJ_TILES),
        in_specs=[pl.BlockSpec((tm, d), lambda i, j: (i, 0)),
                  pl.BlockSpec((1, d), lambda i, j: (0, 0)),
                  pl.BlockSpec((d, tn), lambda i, j: (0, j)),
                  pl.BlockSpec((d, LANES), lambda i, j: (0, 0))],
        out_specs=out_specs,
        out_shape=out_shape,
        scratch_shapes=[pltpu.VMEM((tm, d), _BF16)],
        compiler_params=_params("parallel", "arbitrary"),
        name="proj",
    )(x, g, w_all, wng)
    nkc, nkv = len(kc_specs), len(kv_specs)
    q2, kvc = outs[0], tuple(outs[1:1 + nkc])
    kvs, kvw = tuple(outs[1 + nkc:1 + nkc + nkv]), tuple(outs[1 + nkc + nkv:1 + nkc + 2 * nkv])
    ng, u, mg = outs[1 + nkc + 2 * nkv:]
    return q2, kvc, kvs, kvw, ng, u, mg


def _mix_kernel(a_ref, c_ref, ga_ref, gc_ref, wa_ref, wc_ref, o_ref):
    ya = _dot(a_ref[...], wa_ref[...])
    yc = _dot(c_ref[...], wc_ref[...])
    o_ref[...] = (ga_ref[...].astype(_F32) * ya + gc_ref[...].astype(_F32) * yc).astype(_BF16)


def _mix(attn, conv, mg, wao, wco, tm, tn):
    m, ka = attn.shape
    n = wao.shape[1]
    nb = n // tn
    return pl.pallas_call(
        _mix_kernel,
        grid=(m // tm, nb),
        in_specs=[pl.BlockSpec((tm, ka), lambda i, j: (i, 0)),
                  pl.BlockSpec((tm, conv.shape[1]), lambda i, j: (i, 0)),
                  pl.BlockSpec((tm, tn), lambda i, j: (i, j)),
                  pl.BlockSpec((tm, tn), lambda i, j: (i, j + nb)),
                  pl.BlockSpec((ka, tn), lambda i, j: (0, j)),
                  pl.BlockSpec((conv.shape[1], tn), lambda i, j: (0, j))],
        out_specs=pl.BlockSpec((tm, tn), lambda i, j: (i, j)),
        out_shape=jax.ShapeDtypeStruct((m, n), _BF16),
        compiler_params=_params("parallel", "arbitrary"),
        name="mix",
    )(attn, conv, mg, mg, wao, wco)


def _oproj_kernel(mx_ref, x_ref, w_ref, g_ref, o_ref):
    y = _dot(mx_ref[...], w_ref[...])
    o_ref[...] = x_ref[...] + _rms(y, g_ref[...])


def _oproj(mixed, x, wo, g, tm):
    m, d = x.shape
    return pl.pallas_call(
        _oproj_kernel,
        grid=(m // tm,),
        in_specs=[pl.BlockSpec((tm, d), lambda i: (i, 0)),
                  pl.BlockSpec((tm, d), lambda i: (i, 0)),
                  pl.BlockSpec((d, d), lambda i: (0, 0)),
                  pl.BlockSpec((1, d), lambda i: (0, 0))],
        out_specs=pl.BlockSpec((tm, d), lambda i: (i, 0)),
        out_shape=jax.ShapeDtypeStruct((m, d), _F32),
        compiler_params=_params("parallel"),
        name="oproj",
    )(mixed, x, wo, g)


def _ffn_kernel(x_ref, gpre_ref, gpost_ref, wu_ref, wd_ref, o_ref, h_ref, acc_ref):
    j = pl.program_id(1)

    @pl.when(j == 0)
    def _():
        h_ref[...] = _rms(x_ref[...], gpre_ref[...]).astype(_BF16)
        acc_ref[...] = jnp.zeros_like(acc_ref)

    a = jnp.maximum(_dot(h_ref[...], wu_ref[...]), 0.0)
    acc_ref[...] += _dot((a * a).astype(_BF16), wd_ref[...])

    @pl.when(j == pl.num_programs(1) - 1)
    def _():
        o_ref[...] = x_ref[...] + _rms(acc_ref[...], gpost_ref[...])


def _ffn(x, gpre, gpost, wu, wd, tm, tf):
    m, d = x.shape
    f = wu.shape[1]
    return pl.pallas_call(
        _ffn_kernel,
        grid=(m // tm, f // tf),
        in_specs=[pl.BlockSpec((tm, d), lambda i, j: (i, 0)),
                  pl.BlockSpec((1, d), lambda i, j: (0, 0)),
                  pl.BlockSpec((1, d), lambda i, j: (0, 0)),
                  pl.BlockSpec((d, tf), lambda i, j: (0, j)),
                  pl.BlockSpec((tf, d), lambda i, j: (j, 0))],
        out_specs=pl.BlockSpec((tm, d), lambda i, j: (i, 0)),
        out_shape=jax.ShapeDtypeStruct((m, d), _F32),
        scratch_shapes=[pltpu.VMEM((tm, d), _BF16), pltpu.VMEM((tm, d), _F32)],
        compiler_params=_params("parallel", "arbitrary"),
        name="ffn",
    )(x, gpre, gpost, wu, wd)


def _conv_kernel(u_ref, halo_ref, w_ref, b_ref, lg_ref, lb_ref, o_ref, win_ref, *, tt, zero_first):
    c = u_ref.shape[-1]
    halo = halo_ref[0]
    if zero_first:
        halo = jnp.where(pl.program_id(1) == 0, 0.0, halo)
    win_ref[0, 0:HALO, :] = halo
    win_ref[0, HALO:HALO + tt, :] = u_ref[0]
    span = HALO + tt - SUBLANES
    for s in range(1, SUBLANES):
        win_ref[s, 0:span, :] = win_ref[0, s:s + span, :]
    rc = min(ROW_CHUNK, tt)
    off = HALO - (CONV_K - 1)
    for ch in range(tt // rc):
        acc = jnp.zeros((rc, c), _F32) + b_ref[...]
        for k in range(CONV_K):
            s = (off + k) % SUBLANES
            row = ch * rc + off + k - s
            acc = acc + w_ref[k:k + 1, :] * win_ref[s, row:row + rc, :]
        mu = jnp.mean(acc, axis=-1, keepdims=True)
        xc = acc - mu
        var = jnp.mean(xc * xc, axis=-1, keepdims=True)
        y = xc * lax.rsqrt(var + EPS) * lg_ref[...] + lb_ref[...]
        o_ref[0, ch * rc:(ch + 1) * rc, :] = (y * jax.nn.sigmoid(y)).astype(_BF16)


def _conv(u, halo_src, w, b, lg, lb, tt, zero_first):
    bsz, t, c = u.shape
    nhb = tt // HALO
    if zero_first:
        halo_map = lambda bi, ti: (bi, jnp.maximum(ti * nhb - 1, 0), 0)
    else:
        halo_map = lambda bi, ti: (bi, 0, 0)
    kern = functools.partial(_conv_kernel, tt=tt, zero_first=zero_first)
    return pl.pallas_call(
        kern,
        grid=(bsz, t // tt),
        in_specs=[pl.BlockSpec((1, tt, c), lambda bi, ti: (bi, ti, 0)),
                  pl.BlockSpec((1, HALO, c), halo_map),
                  pl.BlockSpec((HALO, c), lambda bi, ti: (0, 0)),
                  pl.BlockSpec((1, c), lambda bi, ti: (0, 0)),
                  pl.BlockSpec((1, c), lambda bi, ti: (0, 0)),
                  pl.BlockSpec((1, c), lambda bi, ti: (0, 0))],
        out_specs=pl.BlockSpec((1, tt, c), lambda bi, ti: (bi, ti, 0)),
        out_shape=jax.ShapeDtypeStruct((bsz, t, c), _BF16),
        scratch_shapes=[pltpu.VMEM((SUBLANES, HALO + tt, c), _F32)],
        compiler_params=_params("parallel", "arbitrary"),
        name="conv",
    )(u, halo_src, w, b, lg, lb)


def _compress_rows(chunk_row, nrow, first_block, w1_ref, w2_ref, pe_ref, w1t_ref, carry_ref, out_refs):
    kc_ref, vc_ref, vct_ref = out_refs
    outs = []
    for c in range(2):
        acc = jnp.zeros((nrow, 2 * KV_DIM), _F32)
        for s in range(CMP_STRIDE):
            acc = acc + _dot(chunk_row(c, s).astype(_BF16), w1_ref[c, s])
        pt = jnp.sum(pe_ref[c] * w1t_ref[c], axis=0, keepdims=True)
        pt = jnp.concatenate([pt] * (KV_DIM // LANES), axis=-1)
        a0 = acc[:, :KV_DIM]
        a1 = acc[:, KV_DIM:]
        first = lax.broadcasted_iota(jnp.int32, (nrow, KV_DIM), 0) == 0
        prev = 0.0 if first_block is True else jnp.where(first_block, 0.0, carry_ref[c])
        a0s = jnp.where(first, prev, pltpu.roll(a0, 1, 0))
        carry_ref[c] = a0[nrow - 1:nrow, :]
        hid = jax.nn.gelu(a0s + a1 + pt)
        outs.append(_dot(hid.astype(_BF16), w2_ref[c]))
    kc_ref[0] = outs[0].astype(_BF16)
    vc_ref[0] = outs[1].astype(_BF16)
    vct_ref[0] = outs[1].T.astype(_BF16)


def _compress_kernel(src_ref, w1_ref, w2_ref, pe_ref, w1t_ref, kc_ref, vc_ref, vct_ref, carry_ref, *, nrow):
    chunk_row = lambda c, s: src_ref[0, :, s * 2 * KV_DIM + c * KV_DIM:s * 2 * KV_DIM + (c + 1) * KV_DIM]
    _compress_rows(chunk_row, nrow, True, w1_ref, w2_ref, pe_ref, w1t_ref, carry_ref, (kc_ref, vc_ref, vct_ref))


def _compress_paged_kernel(pt_ref, *refs, n_src, steps_per_row):
    del pt_ref
    src_refs = refs[:n_src]
    w1_ref, w2_ref, pe_ref, w1t_ref, kc_ref, vc_ref, vct_ref, carry_ref, rows_a, rows_b = refs[n_src:]
    t = pl.program_id(0)
    nrow = n_src * PAGE_SIZE // CMP_STRIDE
    first_block = (jnp.maximum(t - 1, 0) % steps_per_row) == 0

    @pl.when(t == 0)
    def _():
        rows_b[...] = jnp.zeros_like(rows_b)
        carry_ref[...] = jnp.zeros_like(carry_ref)

    def step(fill_ref, read_ref):
        for p, r in enumerate(src_refs):
            for lc in range(2 * KV_DIM // LANES):
                fill_ref[lc, p * PAGE_SIZE:(p + 1) * PAGE_SIZE, :] = r[0, lc * LANES:(lc + 1) * LANES, :].T

        def chunk_row(c, s):
            lcs = range(c * KV_DIM // LANES, (c + 1) * KV_DIM // LANES)
            return jnp.concatenate([read_ref[lc, pl.ds(s, nrow, stride=CMP_STRIDE), :] for lc in lcs], axis=-1)

        _compress_rows(chunk_row, nrow, first_block, w1_ref, w2_ref, pe_ref, w1t_ref, carry_ref,
                       (kc_ref, vc_ref, vct_ref))

    @pl.when(t % 2 == 0)
    def _():
        step(rows_a, rows_b)

    @pl.when(t % 2 == 1)
    def _():
        step(rows_b, rows_a)


def _compress_call(kern, grid, in_specs, out_specs, scratch, n_prefetch, n_batch, n_out_rows, name, args):
    grid_spec = pltpu.PrefetchScalarGridSpec(num_scalar_prefetch=n_prefetch, grid=grid, in_specs=in_specs,
                                             out_specs=out_specs, scratch_shapes=scratch)
    return pl.pallas_call(
        kern,
        grid_spec=grid_spec,
        out_shape=[jax.ShapeDtypeStruct((n_batch, n_out_rows, KV_DIM), _BF16)] * 2
                  + [jax.ShapeDtypeStruct((n_batch, KV_DIM, n_out_rows), _BF16)],
        compiler_params=_params(*(["arbitrary"] * len(grid))),
        name=name,
    )(*args)


def _const_spec(a):
    return pl.BlockSpec(a.shape, lambda *_, _n=a.ndim: (0,) * _n, pipeline_mode=pl.Buffered(1))


def _compress(src, w1bd, w2bd, pe_b, w1t):
    bsz, nrow, cols = src.shape
    consts = (w1bd, w2bd, pe_b, w1t)
    return _compress_call(
        functools.partial(_compress_kernel, nrow=nrow), (bsz,),
        [pl.BlockSpec((1, nrow, cols), lambda b: (b, 0, 0))] + [_const_spec(a) for a in consts],
        [pl.BlockSpec((1, nrow, KV_DIM), lambda b: (b, 0, 0))] * 2 + [pl.BlockSpec((1, KV_DIM, nrow), lambda b: (b, 0, 0))],
        [pltpu.VMEM((2, 1, KV_DIM), _F32)], 0, bsz, nrow, "compress", (src,) + consts)


def _compress_paged(pages, page_table, w1bd, w2bd, pe_b, w1t):
    bsz, n_pages = page_table.shape
    spr = n_pages // PAGES_PER_STEP
    n_steps = bsz * spr
    nrow = PAGES_PER_STEP * PAGE_SIZE // CMP_STRIDE
    consts = (w1bd, w2bd, pe_b, w1t)

    def page_spec(p):
        def index(t, pt):
            tc = jnp.minimum(t, n_steps - 1)
            return (pt[tc // spr, (tc % spr) * PAGES_PER_STEP + p], 0, 0)
        return pl.BlockSpec((1, 2 * KV_DIM, PAGE_SIZE), index)

    prev = lambda t: jnp.maximum(t - 1, 0)
    row_out = pl.BlockSpec((1, nrow, KV_DIM), lambda t, pt: (prev(t) // spr, prev(t) % spr, 0))
    col_out = pl.BlockSpec((1, KV_DIM, nrow), lambda t, pt: (prev(t) // spr, 0, prev(t) % spr))
    rows_buf = pltpu.VMEM((2 * KV_DIM // LANES, PAGES_PER_STEP * PAGE_SIZE, LANES), _F32)
    return _compress_call(
        functools.partial(_compress_paged_kernel, n_src=PAGES_PER_STEP, steps_per_row=spr), (n_steps + 1,),
        [page_spec(p) for p in range(PAGES_PER_STEP)] + [_const_spec(a) for a in consts],
        [row_out, row_out, col_out], [pltpu.VMEM((2, 1, KV_DIM), _F32), rows_buf, rows_buf],
        1, bsz, n_pages * PAGE_SIZE // CMP_STRIDE, "compress_paged",
        (page_table,) + (pages,) * PAGES_PER_STEP + consts)


def _threshold_select(sc, n_sel, tri):
    bits = pltpu.bitcast(sc, jnp.int32)
    key = jnp.where(bits < 0, bits ^ jnp.int32(0x7FFFFFFF), bits)
    count_ge = lambda t: jnp.sum(jnp.where(key >= t, 1.0, 0.0), axis=0, keepdims=True)
    t0 = jnp.where(count_ge(jnp.int32(0)) >= n_sel, jnp.int32(0), jnp.int32(-2 ** 31))

    def body(b, t):
        cand = t | jnp.left_shift(jnp.int32(1), 30 - b)
        return jnp.where(count_ge(cand) >= n_sel, cand, t)

    t = lax.fori_loop(0, 31, body, t0)
    above = jnp.where(key > t, 1.0, 0.0)
    tie = jnp.where(key == t, 1.0, 0.0)
    need = n_sel - jnp.sum(above, axis=0, keepdims=True)
    tie_rank = _dot(tri, tie.astype(_BF16))
    return above + jnp.where(tie_rank <= need, tie, 0.0)


def _softmax_parts(parts):
    ms = [jnp.max(jnp.where(mk, s, NEG), axis=-1, keepdims=True) for s, mk in parts]
    m = functools.reduce(jnp.maximum, ms)
    ps = [jnp.where(mk, jnp.exp(s - m), 0.0) for s, mk in parts]
    l = functools.reduce(lambda a, b: a + b, [jnp.sum(p, axis=-1, keepdims=True) for p in ps])
    inv = 1.0 / jnp.maximum(l, 1e-30)
    return [p * inv for p in ps]


def _softmax_cols(parts, always_valid):
    masked = [jnp.where(mk, s, NEG) for s, mk in parts]
    m = functools.reduce(jnp.maximum, [jnp.max(s, axis=0, keepdims=True) for s in masked])
    if always_valid:
        ps = [jnp.exp(s - m) for s in masked]
    else:
        ps = [jnp.where(mk, jnp.exp(s - m), 0.0) for s, mk in parts]
    l = functools.reduce(lambda a, b: a + b, [jnp.sum(p, axis=0, keepdims=True) for p in ps])
    return ps, 1.0 / jnp.maximum(l, 1e-30)


def _attn_prompt_kernel(q_ref, ng_ref, ks_ref, vst_ref, kw_ref, vwt_ref, kc_ref, vct_ref,
                        tzt_ref, tct_ref, ot_ref, gsel_ref, rsum_ref, rep_ref, tri_ref,
                        o_ref, qzt_ref, otacc_ref, gt_ref, nsel_ref, *, n_cmp_pad, seq_len):
    i = pl.program_id(1)
    qb = Q_BLOCK
    rows = HPG * qb
    lane_g = lax.broadcasted_iota(jnp.int32, (qb, KV_DIM), 1) // HEAD_DIM
    vrows = lambda g: pl.ds(g * HEAD_DIM, HEAD_DIM)

    for g in range(N_KV):
        qz = jnp.concatenate([jnp.where(lane_g == g, q_ref[0, :, r * KV_DIM:(r + 1) * KV_DIM].astype(_F32), 0.0)
                              for r in range(HPG)], axis=0)
        qzt_ref[g] = qz.T.astype(_BF16)

    gparts = _split3(jax.nn.sigmoid(ng_ref[0]))
    gcols = jnp.concatenate([sum(_dot(p, gsel_ref[r]) for p in gparts) for r in range(HPG)], axis=0)
    gt_ref[...] = gcols.T
    otacc_ref[...] = jnp.zeros_like(otacc_ref)

    def emit(branch, g, out_t):
        otacc_ref[g] += gt_ref[pl.ds(branch * N_KV + g, 1), :] * out_t

    groups = range(N_KV)
    h = jnp.minimum(jnp.maximum(i - 3, 0) // 2, (seq_len - NEAR_KEYS) // LANES)
    ns = pl.multiple_of(h * LANES, LANES)
    v = i - 2 * h
    cn = lax.broadcasted_iota(jnp.int32, (NEAR_KEYS, rows), 0)
    lqn = lax.broadcasted_iota(jnp.int32, (NEAR_KEYS, rows), 1) % qb
    causal = cn - lqn <= v * SEL_BLOCK
    near = pl.ds(ns, NEAR_KEYS)

    fs = pl.multiple_of((jnp.maximum(i - WINDOW // SEL_BLOCK, 0) // 2) * LANES, LANES)
    wfar = pl.ds(fs, WIN_FAR_KEYS)
    cf = lax.broadcasted_iota(jnp.int32, (WIN_FAR_KEYS, rows), 0)
    lqf = lax.broadcasted_iota(jnp.int32, (WIN_FAR_KEYS, rows), 1) % qb
    far_ok = (i * qb + lqf - fs - cf < WINDOW) & (fs + cf < ns)
    kw_near, kw_far = kw_ref[0, near, :], kw_ref[0, wfar, :]
    s_wn = [_dot(kw_near, qzt_ref[g]) + tzt_ref[v, g] for g in groups]
    s_wf = [_dot(kw_far, qzt_ref[g]) for g in groups]
    p_w = [_softmax_cols([(s_wn[g], causal), (s_wf[g], far_ok)], True) for g in groups]
    for g in groups:
        (p_near, p_far), inv_w = p_w[g]
        emit(2, g, (_dot(vwt_ref[0, vrows(g), near], p_near.astype(_BF16))
                    + _dot(vwt_ref[0, vrows(g), wfar], p_far.astype(_BF16))) * inv_w)

    jc = lax.broadcasted_iota(jnp.int32, (n_cmp_pad, rows), 0)
    lq = lax.broadcasted_iota(jnp.int32, (n_cmp_pad, rows), 1) % qb
    cmp_valid = (jc >= 1) & (CMP_STRIDE * jc + (CMP_LEN - CMP_STRIDE - 1) - lq <= i * qb)
    mm = lax.broadcasted_iota(jnp.int32, (n_cmp_pad, LANES), 1)
    jj = lax.broadcasted_iota(jnp.int32, (n_cmp_pad, LANES), 0)
    shift_t = jnp.where((mm < 16) & (jj - mm == 4 * i - 8), 1.0, 0.0).astype(_BF16)
    kc_aug = jnp.concatenate([kc_ref[0], shift_t, shift_t, shift_t], axis=1)
    s_c =[_dot(kc_aug, jnp.concatenate([qzt_ref[g], tct_ref[0, g], tct_ref[1, g], tct_ref[2, g]], axis=0))
           for g in groups]
    sm_c = [_softmax_cols([(s, cmp_valid)], False) for s in s_c]
    pn_c = [ps[0] for ps, _ in sm_c]
    inv_c = [inv for _, inv in sm_c]
    for g in groups:
        emit(0, g, _dot(vct_ref[0, vrows(g), :], pn_c[g].astype(_BF16)) * inv_c[g])
    y_c = [sum(_dot(ot_ref[...], p) for p in _split3(pn_c[g])) * inv_c[g] for g in groups]
    imp_t = sum(sum(_dot(p, rsum_ref[g]) for p in _split3(y_c[g])) for g in groups)

    jrow = lax.broadcasted_iota(jnp.int32, imp_t.shape, 0)
    forced = (jrow == 0) | (jrow == i) | (jrow == i - 1)
    score = jnp.where(forced, jnp.inf, jnp.where(jrow <= i, imp_t, -jnp.inf))
    sel_t = _threshold_select(score, N_SELECT, tri_ref[...])

    sel16 = sel_t.astype(_BF16)
    for g in groups:
        neg = (_dot(sel16, rep_ref[g]) - 1.0) * MASK_BIG
        nsel_ref[0, g] = neg
        nsel_ref[1, g] = jnp.where(jrow < 2 * h, neg, -MASK_BIG)

    def add_block_mask(s, far, g, j0, n_blocks):
        return jnp.concatenate([s[jj * SEL_BLOCK:(jj + 1) * SEL_BLOCK, :] + nsel_ref[far, g, pl.ds(j0 + jj, 1), :]
                                for jj in range(n_blocks)], axis=0)

    n_far = (ns + FAR_TILE - 1) // FAR_TILE

    k_near = ks_ref[0, near, :]
    s_n = [add_block_mask(_dot(k_near, qzt_ref[g]) + tzt_ref[v, g], 0, g, 2 * h, NEAR_KEYS // SEL_BLOCK)
           for g in groups]
    s_n = [jnp.where(causal, s, NEG) for s in s_n]
    m0 = [jnp.max(s, axis=0, keepdims=True) for s in s_n]
    p_n = [jnp.exp(s_n[g] - m0[g]) for g in groups]
    l0 = [jnp.sum(p, axis=0, keepdims=True) for p in p_n]
    acc0 = [_dot(vst_ref[0, vrows(g), near], p_n[g].astype(_BF16)) for g in groups]

    def far_body(tau, carry):
        m_old, l_old, acc_old = carry
        k0 = pl.multiple_of(tau * FAR_TILE, FAR_TILE)
        tile = pl.ds(k0, FAR_TILE)
        kt = ks_ref[0, tile, :]
        s_f = [add_block_mask(_dot(kt, qzt_ref[g]), 1, g, tau * (FAR_TILE // SEL_BLOCK), FAR_TILE // SEL_BLOCK)
               for g in groups]
        m_new = [jnp.maximum(m_old[g], jnp.max(s_f[g], axis=0, keepdims=True)) for g in groups]
        alpha = [jnp.exp(m_old[g] - m_new[g]) for g in groups]
        p_f = [jnp.exp(s_f[g] - m_new[g]) for g in groups]
        l_new = [alpha[g] * l_old[g] + jnp.sum(p_f[g], axis=0, keepdims=True) for g in groups]
        acc = [alpha[g] * acc_old[g] + _dot(vst_ref[0, vrows(g), tile], p_f[g].astype(_BF16)) for g in groups]
        return tuple(m_new), tuple(l_new), tuple(acc)

    _, l1, acc1 = lax.fori_loop(0, n_far, far_body, (tuple(m0), tuple(l0), tuple(acc0)))
    for g in groups:
        emit(1, g, acc1[g] * (1.0 / l1[g]))

    res = otacc_ref[...].reshape(N_KV * HEAD_DIM, rows).T
    for r in range(HPG):
        o_ref[0, :, r * KV_DIM:(r + 1) * KV_DIM] = res[r * qb:(r + 1) * qb, :].astype(_BF16)


def _attn_prompt(q2, ng, ks, vst, kw, vwt, kc, vct, tabs):
    bsz, t, _ = q2.shape
    nblk = t // SEL_BLOCK
    n_cmp_pad = kc.shape[1]
    rows = HPG * Q_BLOCK
    assert t % FAR_TILE == 0 and t >= NEAR_KEYS
    names = ["tzt", "tct", "ot", "gsel", "rsum", "rep", "tri"]
    full = lambda a: pl.BlockSpec(a.shape, lambda b, i, _n=a.ndim: (0,) * _n, pipeline_mode=pl.Buffered(1))
    per_b = lambda a: pl.BlockSpec((1,) + a.shape[1:], lambda b, i: (b, 0, 0))
    kern = functools.partial(_attn_prompt_kernel, n_cmp_pad=n_cmp_pad, seq_len=t)
    return pl.pallas_call(
        kern,
        grid=(bsz, nblk),
        in_specs=[pl.BlockSpec((1, Q_BLOCK, ATTN_DIM), lambda b, i: (b, i, 0)),
                  pl.BlockSpec((1, Q_BLOCK, LANES), lambda b, i: (b, i, 0)),
                  per_b(ks), per_b(vst), per_b(kw), per_b(vwt), per_b(kc), per_b(vct)]
                 + [full(tabs[n]) for n in names],
        out_specs=pl.BlockSpec((1, Q_BLOCK, ATTN_DIM), lambda b, i: (b, i, 0)),
        out_shape=jax.ShapeDtypeStruct((bsz, t, ATTN_DIM), _BF16),
        scratch_shapes=[pltpu.VMEM((N_KV, KV_DIM, rows), _BF16),
                        pltpu.VMEM((N_KV, HEAD_DIM, rows), _F32),
                        pltpu.VMEM((LANES, rows), _F32),
                        pltpu.VMEM((2, N_KV, SEL_BLOCK, rows), _F32)],
        compiler_params=_params("parallel", "arbitrary"),
        name="attn_prompt",
    )(q2, ng, ks, vst, kw, vwt, kc, vct, *[tabs[n] for n in names])


def _bias_table(rel, dist, head):
    nmax = max(int(dist.max()), 1) + 1
    bk = _rel_bucket_np(np.arange(nmax))
    rel_h = rel[:, head]
    out = jnp.broadcast_to(rel_h[0], np.broadcast_shapes(dist.shape, head.shape))
    dist = lax.optimization_barrier(jnp.asarray(dist, jnp.int32))
    for b in range(1, int(bk.max()) + 1):
        first = int(np.argmax(bk >= b))
        out = jnp.where(dist >= first, rel_h[b], out)
    return out


def _prompt_tables(rel_bias, t):
    nblk = t // SEL_BLOCK
    n_cmp_pad = t // CMP_STRIDE
    assert nblk <= SEL_BLOCK and n_cmp_pad % LANES == 0
    rows = HPG * Q_BLOCK
    rel = rel_bias.astype(_F32)
    r_idx = np.arange(rows) // Q_BLOCK
    q_idx = np.arange(rows) % Q_BLOCK
    head = np.arange(N_KV)[:, None] * HPG + r_idx[None, :]
    c31 = rel[REL_BUCKETS - 1][head]
    c = np.arange(NEAR_KEYS)
    dist = np.arange(NEAR_VARIANTS)[:, None, None] * SEL_BLOCK + q_idx[None, None, :] - c[None, :, None]
    tzt = _bias_table(rel, dist[:, None], head[None, :, None, :]) - c31[None, :, None, :]
    mmv = np.arange(16)
    dist_c = q_idx[None, :] - CMP_STRIDE * (mmv[:, None] - 8) - (CMP_LEN - CMP_STRIDE - 1)
    delta = _bias_table(rel, dist_c[None], head[:, None, :]) - c31[:, None, :]
    tct = jnp.stack(_split3(jnp.pad(delta, ((0, 0), (0, LANES - 16), (0, 0)))))
    n = np.arange(n_cmp_pad) - 1
    cs = n * CMP_STRIDE
    bs = np.arange(SEL_BLOCK) * SEL_BLOCK
    ov = np.clip(np.minimum(cs[None, :] + CMP_LEN, bs[:, None] + SEL_BLOCK) - np.maximum(cs[None, :], bs[:, None]),
                 0, CMP_LEN).astype(np.float32) / CMP_LEN
    ov[:, 0] = 0.0
    ov[nblk:, :] = 0.0
    gsel = np.zeros((HPG, LANES, LANES), np.float32)
    for j in range(3):
        for r in range(HPG):
            for g in range(N_KV):
                gsel[r, j * N_HEADS + r * N_KV + g, j * N_KV + g] = 1.0
    rsum = np.zeros((N_KV, rows, N_KV * Q_BLOCK), np.float32)
    for g in range(N_KV):
        rsum[g, np.arange(rows), g * Q_BLOCK + q_idx] = 1.0
    bf = lambda a: jnp.asarray(a, _BF16)
    return dict(tzt=tzt, tct=tct, ot=bf(ov), gsel=bf(gsel), rsum=bf(rsum),
                tri=bf(np.tril(np.ones((SEL_BLOCK, SEL_BLOCK), np.float32))),
                rep=bf(rsum.transpose(0, 2, 1)))


def _gate_expand():
    ex = np.zeros((LANES, 3 * ATTN_DIM), np.float32)
    for j in range(3):
        for r in range(HPG):
            for g in range(N_KV):
                col = j * ATTN_DIM + r * KV_DIM + g * HEAD_DIM
                ex[j * N_HEADS + r * N_KV + g, col:col + HEAD_DIM] = 1.0
    return jnp.asarray(ex, _BF16)


PAGES_PER_STEP = 32
SUB_PAGES = 32


def _attn_sample_kernel(pt_ref, q_ref, ng_ref, kc_ref, vc_ref, knew_ref, wnew_ref, wst_ref, *refs,
                        past_len, n_blk, nb_rows):
    del pt_ref
    page_refs = refs[:PAGES_PER_STEP]
    (bc_ref, blast_ref, c31_ref, bnew_ref, bwin_ref, ot_ref, e64_ref, ex_ref, tri_ref,
     o_ref, qall_ref, gate_ref, oacc_ref, acc_ref, m_ref, l_ref, selt_ref) = refs[PAGES_PER_STEP:]
    k = pl.program_id(1)
    nk = pl.num_programs(1)
    tq = q_ref.shape[1]
    rows = N_KV * HPG * tq
    lane_g = lax.broadcasted_iota(jnp.int32, (tq, KV_DIM), 1) // HEAD_DIM
    rq = lax.broadcasted_iota(jnp.int32, (rows, 1), 0) % tq
    sub_keys = SUB_PAGES * PAGE_SIZE

    def emit(branch, out):
        for g in range(N_KV):
            for r in range(HPG):
                col = r * KV_DIM
                row0 = (g * HPG + r) * tq
                gt = gate_ref[:, branch * ATTN_DIM + col:branch * ATTN_DIM + col + KV_DIM]
                oacc_ref[:, col:col + KV_DIM] += jnp.where(lane_g == g, gt * out[row0:row0 + tq, :], 0.0)

    def pad_rows(x, n):
        if n == x.shape[0]:
            return x
        return jnp.concatenate([x, jnp.zeros((n - x.shape[0], x.shape[1]), x.dtype)], axis=0)

    @pl.when(k == 0)
    def _():
        qf = q_ref[0].astype(_F32)
        pieces = []
        for g in range(N_KV):
            for r in range(HPG):
                pieces.append(jnp.where(lane_g == g, qf[:, r * KV_DIM:(r + 1) * KV_DIM], 0.0))
        qall = jnp.concatenate(pieces, axis=0).astype(_BF16)
        qall_ref[...] = qall
        gs = jax.nn.sigmoid(ng_ref[0])
        gate_ref[...] = sum(_dot(p, ex_ref[...]) for p in _split3(gs))
        oacc_ref[...] = jnp.zeros_like(oacc_ref)

        n_cmp_pad = kc_ref.shape[1]
        jc = lax.broadcasted_iota(jnp.int32, (rows, n_cmp_pad), 1)
        cmp_valid = (jc >= 1) & (CMP_STRIDE * jc + (CMP_LEN - CMP_STRIDE - 1) - rq <= past_len)
        s = _dot_nt(qall, kc_ref[0]) + bc_ref[...]
        (pn,) = _softmax_parts([(s, cmp_valid)])
        emit(0, _dot(pn.astype(_BF16), vc_ref[0]))
        imp_rows = []
        for g in range(N_KV):
            sg = sum(pn[(g * HPG + r) * tq:(g * HPG + r + 1) * tq, :] for r in range(HPG))
            imp_rows += [sg] * HPG
        imp = jnp.concatenate(imp_rows, axis=0)
        imp_t = sum(_dot_nt(ot_ref[...], p) for p in _split3(imp))

        jrow = lax.broadcasted_iota(jnp.int32, imp_t.shape, 0)
        tpos = past_len + lax.broadcasted_iota(jnp.int32, imp_t.shape, 1) % tq
        cur = tpos // SEL_BLOCK
        forced = (jrow == 0) | (jrow == cur) | (jrow == cur - 1)
        valid = jrow * SEL_BLOCK <= tpos
        score = jnp.where(forced, jnp.inf, jnp.where(valid, imp_t, -jnp.inf))
        selt_ref[...] = pad_rows(_threshold_select(score, min(N_SELECT, n_blk), tri_ref[...]), selt_ref.shape[0])

        kn = pad_rows(knew_ref[0, :, :KV_DIM], LANES).astype(_BF16)
        vn = pad_rows(knew_ref[0, :, KV_DIM:], LANES).astype(_BF16)
        cn = lax.broadcasted_iota(jnp.int32, (rows, LANES), 1)
        mk = (cn <= rq) & (cn < tq)
        s = _dot_nt(qall, kn) + bnew_ref[...]
        m = jnp.max(jnp.where(mk, s, NEG), axis=-1, keepdims=True)
        p = jnp.where(mk, jnp.exp(s - m), 0.0)
        m_ref[...] = jnp.broadcast_to(m, m_ref.shape)
        l_ref[...] = jnp.broadcast_to(jnp.sum(p, axis=-1, keepdims=True), l_ref.shape)
        acc_ref[...] = _dot(p.astype(_BF16), vn)

    qall = qall_ref[...]
    blk_per_step = PAGES_PER_STEP * PAGE_SIZE // SEL_BLOCK
    j0 = pl.multiple_of(k * blk_per_step, blk_per_step)
    sel_step = selt_ref[pl.ds(j0, LANES), :].T.astype(_BF16)
    c31 = jnp.concatenate([c31_ref[...]] * (sub_keys // LANES), axis=-1)
    n_sub = PAGES_PER_STEP // SUB_PAGES
    for st in range(n_sub):
        pages = page_refs[st * SUB_PAGES:(st + 1) * SUB_PAGES]
        kt = jnp.concatenate([r[0, :KV_DIM, :] for r in pages], axis=1).astype(_BF16)
        vt = jnp.concatenate([r[0, KV_DIM:, :] for r in pages], axis=1).astype(_BF16)
        mk = _dot(sel_step, e64_ref[:, st * sub_keys:(st + 1) * sub_keys]) > 0.5
        if st == n_sub - 1:
            bias = jnp.where(k == nk - 1, blast_ref[...], c31)
        else:
            bias = c31
        s = jnp.where(mk, _dot(qall, kt) + bias, NEG)
        m_old = m_ref[:, :1]
        m_new = jnp.maximum(m_old, jnp.max(s, axis=-1, keepdims=True))
        alpha = jnp.exp(m_old - m_new)
        p = jnp.exp(s - m_new)
        l_ref[...] = jnp.broadcast_to(alpha * l_ref[:, :1] + jnp.sum(p, axis=-1, keepdims=True), l_ref.shape)
        m_ref[...] = jnp.broadcast_to(m_new, m_ref.shape)
        acc_ref[...] = alpha * acc_ref[...] + _dot_nt(p.astype(_BF16), vt)

    @pl.when(k == nk - 1)
    def _():
        emit(1, acc_ref[...] * (1.0 / l_ref[:, :1]))
        lw = wst_ref.shape[2]
        kw = wst_ref[0, :KV_DIM, :].astype(_BF16)
        vw = wst_ref[0, KV_DIM:, :].astype(_BF16)
        kn = pad_rows(wnew_ref[0, :, :KV_DIM], LANES).astype(_BF16)
        vn = pad_rows(wnew_ref[0, :, KV_DIM:], LANES).astype(_BF16)
        cw = lax.broadcasted_iota(jnp.int32, (rows, lw), 1)
        dw = lw + rq - cw
        cn = lax.broadcasted_iota(jnp.int32, (rows, LANES), 1)
        pw, pnw = _softmax_parts([(_dot(qall, kw) + bwin_ref[...], (dw >= 0) & (dw < WINDOW)),
                                  (_dot_nt(qall, kn) + bnew_ref[...], (cn <= rq) & (cn < tq))])
        emit(2, _dot_nt(pw.astype(_BF16), vw) + _dot(pnw.astype(_BF16), vn))
        o_ref[0] = oacc_ref[...].astype(_BF16)


def _page_specs():
    return [pl.BlockSpec((1, 2 * KV_DIM, PAGE_SIZE),
                         lambda b, k, pt, _p=p: (pt[b, k * PAGES_PER_STEP + _p], 0, 0))
            for p in range(PAGES_PER_STEP)]


def _transposed_rows(a):
    n, rows = a.shape[:2]
    return jnp.transpose(a, (0, 2, 3, 4, 1)).reshape(n, 2 * KV_DIM, rows)


def _attn_sample(q2, ng, kc, vc, kvs_new, kvw_new, win_state, slc_pages, page_table, tabs, past_len):
    bsz, tq, _ = q2.shape
    n_pages = page_table.shape[1]
    assert n_pages % PAGES_PER_STEP == 0 and past_len == n_pages * PAGE_SIZE and past_len % SEL_BLOCK == 0
    n_steps = n_pages // PAGES_PER_STEP
    rows = N_KV * HPG * tq
    assert rows == LANES
    n_blk = -(-(past_len + tq) // SEL_BLOCK)
    nb_rows = tabs["ot"].shape[0]
    blk_per_step = PAGES_PER_STEP * PAGE_SIZE // SEL_BLOCK
    selt_rows = max((n_steps - 1) * blk_per_step + LANES, nb_rows)
    full = lambda a: pl.BlockSpec(a.shape, lambda b, k, pt, _n=a.ndim: (0,) * _n, pipeline_mode=pl.Buffered(1))
    per_b = lambda a: pl.BlockSpec((1,) + a.shape[1:], lambda b, k, pt: (b, 0, 0))
    page_specs = _page_specs()
    names = ["bc", "blast", "c31", "bnew", "bwin", "ot", "e64", "ex", "tri"]
    kern = functools.partial(_attn_sample_kernel, past_len=past_len, n_blk=n_blk, nb_rows=nb_rows)
    grid_spec = pltpu.PrefetchScalarGridSpec(
        num_scalar_prefetch=1,
        grid=(bsz, n_steps),
        in_specs=[per_b(q2), per_b(ng), per_b(kc), per_b(vc), per_b(kvs_new), per_b(kvw_new), per_b(win_state)]
                 + page_specs + [full(tabs[n]) for n in names],
        out_specs=pl.BlockSpec((1, tq, ATTN_DIM), lambda b, k, pt: (b, 0, 0)),
        scratch_shapes=[pltpu.VMEM((rows, KV_DIM), _BF16),
                        pltpu.VMEM((tq, 3 * ATTN_DIM), _F32),
                        pltpu.VMEM((tq, ATTN_DIM), _F32),
                        pltpu.VMEM((rows, KV_DIM), _F32),
                        pltpu.VMEM((rows, LANES), _F32),
                        pltpu.VMEM((rows, LANES), _F32),
                        pltpu.VMEM((selt_rows, rows), _F32)],
    )
    return pl.pallas_call(
        kern,
        grid_spec=grid_spec,
        out_shape=jax.ShapeDtypeStruct((bsz, tq, ATTN_DIM), _BF16),
        compiler_params=_params("parallel", "arbitrary"),
        name="attn_sample",
    )(page_table, q2, ng, kc, vc, kvs_new, kvw_new, win_state, *([slc_pages] * PAGES_PER_STEP),
      *[tabs[n] for n in names])


def _sample_tables(rel_bias, past_len, tq, lw, n_cmp_pad):
    rows = N_KV * HPG * tq
    rel = rel_bias.astype(_F32)
    ridx = np.arange(rows)
    head = ridx // tq
    qi = ridx % tq
    tpos = past_len + qi

    bias_of = lambda dist: _bias_table(rel, dist, head[:, None])

    jc = np.arange(n_cmp_pad)
    bc = bias_of(tpos[:, None] - (CMP_STRIDE * jc[None, :] + CMP_LEN - CMP_STRIDE - 1))
    sub_keys = SUB_PAGES * PAGE_SIZE
    blast = bias_of(tpos[:, None] - (past_len - sub_keys + np.arange(sub_keys))[None, :])
    bnew = bias_of(qi[:, None] - np.arange(LANES)[None, :])
    bwin = bias_of(lw + qi[:, None] - np.arange(lw)[None, :])
    c31 = jnp.broadcast_to(rel[REL_BUCKETS - 1][head][:, None], (rows, LANES))
    n_blk = -(-(past_len + tq) // SEL_BLOCK)
    nb_rows = -(-n_blk // LANES) * LANES
    n = jc - 1
    cs = n * CMP_STRIDE
    bs = np.arange(nb_rows) * SEL_BLOCK
    ov = np.clip(np.minimum(cs[None, :] + CMP_LEN, bs[:, None] + SEL_BLOCK) - np.maximum(cs[None, :], bs[:, None]),
                 0, CMP_LEN).astype(np.float32) / CMP_LEN
    ov[:, 0] = 0.0
    ov[n_blk:, :] = 0.0
    step_keys = PAGES_PER_STEP * PAGE_SIZE
    e64 = (np.arange(step_keys)[None, :] // SEL_BLOCK == np.arange(LANES)[:, None]).astype(np.float32)
    tri = np.tril(np.ones((nb_rows, nb_rows), np.float32))
    return dict(bc=bc, blast=blast, c31=c31, bnew=bnew, bwin=bwin, ot=jnp.asarray(ov, _BF16),
                e64=jnp.asarray(e64, _BF16), ex=_gate_expand(), tri=jnp.asarray(tri, _BF16))


def _layer_weights(w_in, phi_pe, phi_w1, phi_w2, w_attn_out, conv_w, w_conv_out, w_o, w_up, w_down):
    d = w_in.shape[0]
    o_q, o_kc, o_ks, o_kw = 0, ATTN_DIM, ATTN_DIM + 2 * KV_DIM, ATTN_DIM + 4 * KV_DIM
    o_ng = ATTN_DIM + 6 * KV_DIM
    o_glu = o_ng + 3 * N_HEADS
    o_mg = o_glu + 2 * CONV_DIM
    bf = lambda a: a.astype(_BF16)
    wq = w_in[:, o_q:o_kc].reshape(d, N_KV, HPG, HEAD_DIM).transpose(0, 2, 1, 3).reshape(d, ATTN_DIM)
    wng = w_in[:, o_ng:o_glu].reshape(d, N_KV, HPG, 3).transpose(0, 3, 2, 1).reshape(d, 3 * N_HEADS)
    wng = jnp.pad(wng, ((0, 0), (0, LANES - 3 * N_HEADS)))
    w5 = phi_w1.reshape(2, CMP_R, CMP_STRIDE, HEAD_DIM, PHI_HIDDEN)
    eye = jnp.eye(N_KV, dtype=_F32)
    w1bd = jnp.einsum("crsde,gh->csgdrhe", w5, eye).reshape(2, CMP_STRIDE, KV_DIM, CMP_R * KV_DIM)
    w2bd = jnp.einsum("che,gk->cghke", phi_w2, eye).reshape(2, KV_DIM, KV_DIM)
    rep = LANES // PHI_HIDDEN
    pe_b = jnp.broadcast_to(phi_pe.reshape(2, CMP_LEN * HEAD_DIM, 1), (2, CMP_LEN * HEAD_DIM, LANES))
    w1t = jnp.tile(phi_w1, (1, 1, rep))
    wao = w_attn_out.reshape(N_KV, HPG, HEAD_DIM, d).transpose(1, 0, 2, 3).reshape(ATTN_DIM, d)
    half = PROJ_TILE // 2
    wga = w_in[:, o_glu:o_glu + CONV_DIM].reshape(d, CONV_DIM // half, half)
    wgb = w_in[:, o_glu + CONV_DIM:o_mg].reshape(d, CONV_DIM // half, half)
    wglu = jnp.concatenate([wga, wgb], axis=2).reshape(d, 2 * CONV_DIM)
    w_all = jnp.concatenate([wq, w_in[:, o_kc:o_ng], wglu, w_in[:, o_mg:]], axis=1)
    return dict(
        w_all=bf(w_all), wng=bf(wng), w1bd=bf(w1bd), w2bd=bf(w2bd), pe_b=pe_b, w1t=w1t,
        wao=bf(wao), wco=bf(w_conv_out), wo=bf(w_o), wup=bf(w_up), wdown=bf(w_down),
        conv_w=jnp.pad(conv_w, ((0, HALO - CONV_K), (0, 0))))


def _finish(x2, attn2, conv, mg, w, g_post_mix, g_pre_ffn, g_post_ffn, tm):
    mixed = _mix(attn2, conv, mg, w["wao"], w["wco"], tm, MIX_TILE_N)
    x1 = _oproj(mixed, x2, w["wo"], g_post_mix, tm)
    return _ffn(x1, g_pre_ffn, g_post_ffn, w["wup"], w["wdown"], tm, FFN_TILE_F)


def kernel(x_prompt, x_sample, cache_kv_cmp, cache_kv_slc, state_kv_win, state_conv, page_table, w_in, phi_pe,
           phi_w1, phi_w2, rel_bias, w_attn_out, conv_w, conv_b, conv_ln_g, conv_ln_b, w_conv_out, w_o, w_up,
           w_down, g_pre_mix, g_post_mix, g_pre_ffn, g_post_ffn):
    depth = w_in.shape[0]
    bp, tp, d = x_prompt.shape
    bs, ts, _ = x_sample.shape
    n_pages = page_table.shape[1]
    past_len = n_pages * PAGE_SIZE
    lw = state_kv_win.shape[2]
    chunk_cols = CMP_STRIDE * 2 * KV_DIM
    assert ts < CMP_STRIDE and tp % CMP_STRIDE == 0 and lw == WINDOW and tp >= WINDOW

    tabs_p = _prompt_tables(rel_bias, tp)
    tabs_s = _sample_tables(rel_bias, past_len, ts, lw, past_len // CMP_STRIDE)
    yp, ys = x_prompt.reshape(bp * tp, d), x_sample.reshape(bs * ts, d)
    outs = [[] for _ in range(8)]
    row = lambda a: a.reshape(1, -1)
    kv5 = lambda a, b, t: a.reshape(b, t, 2, N_KV, HEAD_DIM)
    kv5_t = lambda a: jnp.transpose(a.reshape(a.shape[0], 2, N_KV, HEAD_DIM, a.shape[2]), (0, 4, 1, 2, 3))
    for l in range(depth):
        w = _layer_weights(w_in[l], phi_pe[l], phi_w1[l], phi_w2[l], w_attn_out[l], conv_w[l], w_conv_out[l],
                           w_o[l], w_up[l], w_down[l])
        gpm, gqm, gpf, gqf = row(g_pre_mix[l]), row(g_post_mix[l]), row(g_pre_ffn[l]), row(g_post_ffn[l])
        cargs = (w["conv_w"], row(conv_b[l]), row(conv_ln_g[l]), row(conv_ln_b[l]))
        cmp_w = (w["w1bd"], w["w2bd"], w["pe_b"], w["w1t"])

        tm = ROW_TILE
        q2, (kvc_t, kvc16), (kvs_t, ks16, vst16), (kvw_t, kw16, vwt16), ng, u, mg = _proj(
            yp, gpm, w["w_all"], w["wng"], PROJ_ROW_TILE, tp)
        n_chunk = tp // CMP_STRIDE
        kc, _, vct = _compress(kvc16.reshape(bp, n_chunk, chunk_cols), *cmp_w)
        attn2 = _attn_prompt(q2.reshape(bp, tp, ATTN_DIM), ng.reshape(bp, tp, LANES),
                             ks16.reshape(bp, tp, KV_DIM), vst16, kw16.reshape(bp, tp, KV_DIM), vwt16,
                             kc, vct, tabs_p)
        u3 = u.reshape(bp, tp, CONV_DIM)
        conv = _conv(u3, u3, *cargs, CONV_TILE_T, True)
        yp = _finish(yp, attn2.reshape(bp * tp, ATTN_DIM), conv.reshape(bp * tp, CONV_DIM), mg, w, gqm, gpf, gqf, tm)
        outs[0].append(kv5_t(kvc_t))
        outs[2].append(kv5_t(kvs_t))
        outs[4].append(kv5_t(kvw_t[:, :, tp - WINDOW:]))
        outs[6].append(u3[:, tp - (CONV_K - 1):])

        tm = bs * ts
        q2, (kvc,), (kvs,), (kvw,), ng, u, mg = _proj(ys, gpm, w["w_all"], w["wng"], tm)
        kc, vc, _ = _compress_paged(_transposed_rows(cache_kv_cmp[l]), page_table, *cmp_w)
        attn2 = _attn_sample(q2.reshape(bs, ts, ATTN_DIM), ng.reshape(bs, ts, LANES), kc, vc,
                             kvs.reshape(bs, ts, 2 * KV_DIM), kvw.reshape(bs, ts, 2 * KV_DIM),
                             _transposed_rows(state_kv_win[l]), _transposed_rows(cache_kv_slc[l]),
                             page_table, tabs_s, past_len)
        u3 = u.reshape(bs, ts, CONV_DIM)
        hist = jnp.pad(state_conv[l], ((0, 0), (HALO - (CONV_K - 1), 0), (0, 0)))
        conv = _conv(u3, hist, *cargs, ts, False)
        ys = _finish(ys, attn2.reshape(bs * ts, ATTN_DIM), conv.reshape(bs * ts, CONV_DIM), mg, w, gqm, gpf, gqf, tm)
        outs[1].append(kv5(kvc, bs, ts))
        outs[3].append(kv5(kvs, bs, ts))
        win_rows = jnp.concatenate([state_kv_win[l], kv5(kvw, bs, ts)], axis=1)
        outs[5].append(win_rows[:, win_rows.shape[1] - min(WINDOW, win_rows.shape[1]):])
        up = jnp.concatenate([state_conv[l], u3], axis=1)
        outs[7].append(up[:, up.shape[1] - (CONV_K - 1):])

    stack = lambda i: jnp.stack(outs[i])
    return (yp.reshape(bp, tp, d), ys.reshape(bs, ts, d), stack(0), stack(1), stack(2), stack(3),
            stack(4), stack(5), stack(6), stack(7))
```

```python
import functools
import math

import jax
import jax.numpy as jnp
import numpy as np
from jax import lax
from jax.experimental import pallas as pl
from jax.experimental.pallas import tpu as pltpu

D_MODEL = 2048
N_HEADS = 16
HEAD_DIM = 64
N_KV = 4
HPG = N_HEADS // N_KV
ATTN_DIM = N_HEADS * HEAD_DIM
KV_DIM = N_KV * HEAD_DIM
CMP_LEN = 32
CMP_STRIDE = 16
CMP_R = CMP_LEN // CMP_STRIDE
PHI_HIDDEN = HEAD_DIM
SEL_BLOCK = 64
N_SELECT = 16
WINDOW = 512
Q_BLOCK = 64
CONV_DIM = D_MODEL // 2
CONV_K = 31
REL_BUCKETS = 32
REL_MAX_DIST = 128
EPS = 1e-6
NEG = -1e30
PAGE_SIZE = 128

LANES = 128
SUBLANES = 8
VMEM_LIMIT_BYTES = 56 * 1024 * 1024

NEAR_KEYS = 384
NEAR_VARIANTS = 6
FAR_TILE = 1024
WIN_FAR_KEYS = 384
MASK_BIG = 1e30
ROW_TILE = 512
PROJ_ROW_TILE = 1024
MIX_ROW_TILE = 1024
MIX_TILE_N = 1024
FFN_TILE_F = 1024
CONV_TILE_T = 256
ROW_CHUNK = 32
HALO = 32

_F32 = jnp.float32
_BF16 = jnp.bfloat16


def _params(*sem):
    return pltpu.CompilerParams(dimension_semantics=sem, vmem_limit_bytes=VMEM_LIMIT_BYTES)


def _dot(a, b):
    return jnp.dot(a, b, preferred_element_type=_F32)


def _dot_nt(a, b):
    return lax.dot_general(a, b, (((1,), (1,)), ((), ())), preferred_element_type=_F32)


def _split3(x):
    hi = x.astype(_BF16)
    r1 = x - hi.astype(_F32)
    mid = r1.astype(_BF16)
    lo = (r1 - mid.astype(_F32)).astype(_BF16)
    return hi, mid, lo


def _rms(x, g):
    return x * lax.rsqrt(jnp.mean(x * x, axis=-1, keepdims=True) + EPS) * g


def _rel_bucket_np(dist):
    n = np.maximum(dist, 0)
    exact = REL_BUCKETS // 2
    logb = exact + (np.log(np.maximum(n, 1).astype(np.float32) / np.float32(exact))
                    / np.float32(math.log(REL_MAX_DIST / exact)) * (REL_BUCKETS - exact)).astype(np.int32)
    return np.where(n < exact, n, np.minimum(logb, REL_BUCKETS - 1)).astype(np.int32)


PROJ_TILE = 512
PROJ_SEGMENTS = (("q", 0, 2), ("kvc", 2, 1), ("kvs", 3, 1), ("kvw", 4, 1), ("glu", 5, 4), ("mg", 9, 8))
PROJ_TILES = 17


def _proj_kernel(x_ref, g_ref, w_ref, wng_ref, *refs, transposed_v):
    h_ref = refs[-1]
    if transposed_v:
        (q_ref, kvc_ref, kvc16_ref, kvs_ref, ks16_ref, vst_ref, kvw_ref, kw16_ref, vwt_ref,
         ng_ref, u_ref, mg_ref) = refs[:-1]
    else:
        q_ref, kvc_ref, kvs_ref, kvw_ref, ng_ref, u_ref, mg_ref = refs[:-1]
    j = pl.program_id(1)
    seg = {name: (lo, lo + n) for name, lo, n in PROJ_SEGMENTS}
    inside = lambda name: (j >= seg[name][0]) & (j < seg[name][1])

    @pl.when(j == 0)
    def _():
        h_ref[...] = _rms(x_ref[...], g_ref[...]).astype(_BF16)
        ng_ref[...] = _dot(h_ref[...], wng_ref[...])

    tm = h_ref.shape[0]
    n_split = 2 if tm % (2 * LANES) == 0 else 1

    def segment(name, epilogue):
        @pl.when(inside(name))
        def _():
            for part in range(n_split):
                rs = slice(part * tm // n_split, (part + 1) * tm // n_split)
                epilogue(rs, _dot(h_ref[rs, :], w_ref[...]))

    def q_out(rs, acc):
        q_ref[rs, :] = (acc * (HEAD_DIM ** -0.5)).astype(_BF16)

    def kvc_out(rs, acc):
        if transposed_v:
            kvc_ref[0, :, rs] = acc.T
            kvc16_ref[rs, :] = acc.astype(_BF16)
        else:
            kvc_ref[rs, :] = acc

    def kv_out(f32_ref, k16_ref, vt_ref):
        def out(rs, acc):
            if transposed_v:
                acc_t = acc.T
                f32_ref[0, :, rs] = acc_t
                k16_ref[rs, :] = acc[:, :KV_DIM].astype(_BF16)
                vt_ref[0, :, rs] = acc_t[KV_DIM:, :].astype(_BF16)
            else:
                f32_ref[rs, :] = acc
        return out

    def glu_out(rs, acc):
        half = PROJ_TILE // 2
        u_ref[rs, :] = acc[:, :half] * jax.nn.sigmoid(acc[:, half:])

    def mg_out(rs, acc):
        mg_ref[rs, :] = jax.nn.sigmoid(acc).astype(_BF16)

    segment("q", q_out)
    segment("kvc", kvc_out)
    segment("kvs", kv_out(kvs_ref, ks16_ref if transposed_v else None, vst_ref if transposed_v else None))
    segment("kvw", kv_out(kvw_ref, kw16_ref if transposed_v else None, vwt_ref if transposed_v else None))
    segment("glu", glu_out)
    segment("mg", mg_out)


def _proj(x, g, w_all, wng, tm, seq_len=None):
    m, d = x.shape
    tn = PROJ_TILE
    assert w_all.shape == (d, PROJ_TILES * tn)
    transposed_v = seq_len is not None
    seg = {name: (lo, n) for name, lo, n in PROJ_SEGMENTS}

    def spec(name, width=tn):
        lo, n = seg[name]
        return pl.BlockSpec((tm, width), lambda i, j: (i, jnp.clip(j - lo, 0, n - 1)))

    f32 = lambda n: jax.ShapeDtypeStruct((m, n), _F32)
    b16 = lambda n: jax.ShapeDtypeStruct((m, n), _BF16)
    if transposed_v:
        assert seq_len % tm == 0
        spb = seq_len // tm
        nb = m // seq_len
        t_spec = lambda rows: pl.BlockSpec((1, rows, tm), lambda i, j: (i // spb, 0, i % spb))
        row_spec = lambda width: pl.BlockSpec((tm, width), lambda i, j: (i, 0))
        f32_t = jax.ShapeDtypeStruct((nb, tn, seq_len), _F32)
        kc_specs, kc_shapes = [t_spec(tn), row_spec(tn)], [f32_t, b16(tn)]
        kv_specs = [t_spec(tn), row_spec(KV_DIM), t_spec(KV_DIM)]
        kv_shapes = [f32_t, b16(KV_DIM), jax.ShapeDtypeStruct((nb, KV_DIM, seq_len), _BF16)]
        kw_specs, kw_shapes = kv_specs, kv_shapes
    else:
        kc_specs, kc_shapes = [spec("kvc")], [f32(tn)]
        kv_specs, kv_shapes = [spec("kvs")], [f32(tn)]
        kw_specs, kw_shapes = [spec("kvw")], [f32(tn)]
    out_specs = ([spec("q")] + kc_specs + kv_specs + kw_specs
                 + [pl.BlockSpec((tm, LANES), lambda i, j: (i, 0)), spec("glu", tn // 2), spec("mg")])
    out_shape = ([b16(seg["q"][1] * tn)] + kc_shapes + kv_shapes + kw_shapes
                 + [f32(LANES), f32(seg["glu"][1] * tn // 2), b16(seg["mg"][1] * tn)])
    outs = pl.pallas_call(
        functools.partial(_proj_kernel, transposed_v=transposed_v),
        grid=(m // tm, PROJ_TILES),
        in_specs=[pl.BlockSpec((tm, d), lambda i, j: (i, 0)),
                  pl.BlockSpec((1, d), lambda i, j: (0, 0)),
                  pl.BlockSpec((d, tn), lambda i, j: (0, j)),
                  pl.BlockSpec((d, LANES), lambda i, j: (0, 0))],
        out_specs=out_specs,
        out_shape=out_shape,
        scratch_shapes=[pltpu.VMEM((tm, d), _BF16)],
        compiler_params=_params("parallel", "arbitrary"),
        name="proj",
    )(x, g, w_all, wng)
    nkc, nkv = len(kc_specs), len(kv_specs)
    q2, kvc = outs[0], tuple(outs[1:1 + nkc])
    kvs, kvw = tuple(outs[1 + nkc:1 + nkc + nkv]), tuple(outs[1 + nkc + nkv:1 + nkc + 2 * nkv])
    ng, u, mg = outs[1 + nkc + 2 * nkv:]
    return q2, kvc, kvs, kvw, ng, u, mg


def _mix_kernel(a_ref, c_ref, ga_ref, gc_ref, wa_ref, wc_ref, o_ref):
    ya = _dot(a_ref[...], wa_ref[...])
    yc = _dot(c_ref[...], wc_ref[...])
    o_ref[...] = (ga_ref[...].astype(_F32) * ya + gc_ref[...].astype(_F32) * yc).astype(_BF16)


def _mix(attn, conv, mg, wao, wco, tm, tn):
    m, ka = attn.shape
    n = wao.shape[1]
    nb = n // tn
    return pl.pallas_call(
        _mix_kernel,
        grid=(m // tm, nb),
        in_specs=[pl.BlockSpec((tm, ka), lambda i, j: (i, 0)),
                  pl.BlockSpec((tm, conv.shape[1]), lambda i, j: (i, 0)),
                  pl.BlockSpec((tm, tn), lambda i, j: (i, j)),
                  pl.BlockSpec((tm, tn), lambda i, j: (i, j + nb)),
                  pl.BlockSpec((ka, tn), lambda i, j: (0, j)),
                  pl.BlockSpec((conv.shape[1], tn), lambda i, j: (0, j))],
        out_specs=pl.BlockSpec((tm, tn), lambda i, j: (i, j)),
        out_shape=jax.ShapeDtypeStruct((m, n), _BF16),
        compiler_params=_params("parallel", "arbitrary"),
        name="mix",
    )(attn, conv, mg, mg, wao, wco)


def _oproj_kernel(mx_ref, x_ref, w_ref, g_ref, o_ref):
    y = _dot(mx_ref[...], w_ref[...])
    o_ref[...] = x_ref[...] + _rms(y, g_ref[...])


def _oproj(mixed, x, wo, g, tm):
    m, d = x.shape
    return pl.pallas_call(
        _oproj_kernel,
        grid=(m // tm,),
        in_specs=[pl.BlockSpec((tm, d), lambda i: (i, 0)),
                  pl.BlockSpec((tm, d), lambda i: (i, 0)),
                  pl.BlockSpec((d, d), lambda i: (0, 0)),
                  pl.BlockSpec((1, d), lambda i: (0, 0))],
        out_specs=pl.BlockSpec((tm, d), lambda i: (i, 0)),
        out_shape=jax.ShapeDtypeStruct((m, d), _F32),
        compiler_params=_params("parallel"),
        name="oproj",
    )(mixed, x, wo, g)


def _ffn_kernel(x_ref, gpre_ref, gpost_ref, wu_ref, wd_ref, o_ref, h_ref, acc_ref):
    j = pl.program_id(1)

    @pl.when(j == 0)
    def _():
        h_ref[...] = _rms(x_ref[...], gpre_ref[...]).astype(_BF16)
        acc_ref[...] = jnp.zeros_like(acc_ref)

    a = jnp.maximum(_dot(h_ref[...], wu_ref[...]), 0.0)
    acc_ref[...] += _dot((a * a).astype(_BF16), wd_ref[...])

    @pl.when(j == pl.num_programs(1) - 1)
    def _():
        o_ref[...] = x_ref[...] + _rms(acc_ref[...], gpost_ref[...])


def _ffn(x, gpre, gpost, wu, wd, tm, tf):
    m, d = x.shape
    f = wu.shape[1]
    return pl.pallas_call(
        _ffn_kernel,
        grid=(m // tm, f // tf),
        in_specs=[pl.BlockSpec((tm, d), lambda i, j: (i, 0)),
                  pl.BlockSpec((1, d), lambda i, j: (0, 0)),
                  pl.BlockSpec((1, d), lambda i, j: (0, 0)),
                  pl.BlockSpec((d, tf), lambda i, j: (0, j)),
                  pl.BlockSpec((tf, d), lambda i, j: (j, 0))],
        out_specs=pl.BlockSpec((tm, d), lambda i, j: (i, 0)),
        out_shape=jax.ShapeDtypeStruct((m, d), _F32),
        scratch_shapes=[pltpu.VMEM((tm, d), _BF16), pltpu.VMEM((tm, d), _F32)],
        compiler_params=_params("parallel", "arbitrary"),
        name="ffn",
    )(x, gpre, gpost, wu, wd)


def _conv_kernel(u_ref, halo_ref, w_ref, b_ref, lg_ref, lb_ref, o_ref, win_ref, *, tt, zero_first):
    c = u_ref.shape[-1]
    halo = halo_ref[0]
    if zero_first:
        halo = jnp.where(pl.program_id(1) == 0, 0.0, halo)
    win_ref[0, 0:HALO, :] = halo
    win_ref[0, HALO:HALO + tt, :] = u_ref[0]
    span = HALO + tt - SUBLANES
    for s in range(1, SUBLANES):
        win_ref[s, 0:span, :] = win_ref[0, s:s + span, :]
    rc = min(ROW_CHUNK, tt)
    off = HALO - (CONV_K - 1)
    for ch in range(tt // rc):
        acc = jnp.zeros((rc, c), _F32) + b_ref[...]
        for k in range(CONV_K):
            s = (off + k) % SUBLANES
            row = ch * rc + off + k - s
            acc = acc + w_ref[k:k + 1, :] * win_ref[s, row:row + rc, :]
        mu = jnp.mean(acc, axis=-1, keepdims=True)
        xc = acc - mu
        var = jnp.mean(xc * xc, axis=-1, keepdims=True)
        y = xc * lax.rsqrt(var + EPS) * lg_ref[...] + lb_ref[...]
        o_ref[0, ch * rc:(ch + 1) * rc, :] = (y * jax.nn.sigmoid(y)).astype(_BF16)


def _conv(u, halo_src, w, b, lg, lb, tt, zero_first):
    bsz, t, c = u.shape
    nhb = tt // HALO
    if zero_first:
        halo_map = lambda bi, ti: (bi, jnp.maximum(ti * nhb - 1, 0), 0)
    else:
        halo_map = lambda bi, ti: (bi, 0, 0)
    kern = functools.partial(_conv_kernel, tt=tt, zero_first=zero_first)
    return pl.pallas_call(
        kern,
        grid=(bsz, t // tt),
        in_specs=[pl.BlockSpec((1, tt, c), lambda bi, ti: (bi, ti, 0)),
                  pl.BlockSpec((1, HALO, c), halo_map),
                  pl.BlockSpec((HALO, c), lambda bi, ti: (0, 0)),
                  pl.BlockSpec((1, c), lambda bi, ti: (0, 0)),
                  pl.BlockSpec((1, c), lambda bi, ti: (0, 0)),
                  pl.BlockSpec((1, c), lambda bi, ti: (0, 0))],
        out_specs=pl.BlockSpec((1, tt, c), lambda bi, ti: (bi, ti, 0)),
        out_shape=jax.ShapeDtypeStruct((bsz, t, c), _BF16),
        scratch_shapes=[pltpu.VMEM((SUBLANES, HALO + tt, c), _F32)],
        compiler_params=_params("parallel", "arbitrary"),
        name="conv",
    )(u, halo_src, w, b, lg, lb)


def _compress_rows(chunk_row, nrow, first_block, w1_ref, w2_ref, pe_ref, w1t_ref, carry_ref, out_refs):
    kc_ref, vc_ref, vct_ref = out_refs
    outs = []
    for c in range(2):
        acc = jnp.zeros((nrow, 2 * KV_DIM), _F32)
        for s in range(CMP_STRIDE):
            acc = acc + _dot(chunk_row(c, s).astype(_BF16), w1_ref[c, s])
        pt = jnp.sum(pe_ref[c] * w1t_ref[c], axis=0, keepdims=True)
        pt = jnp.concatenate([pt] * (KV_DIM // LANES), axis=-1)
        a0 = acc[:, :KV_DIM]
        a1 = acc[:, KV_DIM:]
        first = lax.broadcasted_iota(jnp.int32, (nrow, KV_DIM), 0) == 0
        prev = 0.0 if first_block is True else jnp.where(first_block, 0.0, carry_ref[c])
        a0s = jnp.where(first, prev, pltpu.roll(a0, 1, 0))
        carry_ref[c] = a0[nrow - 1:nrow, :]
        hid = jax.nn.gelu(a0s + a1 + pt)
        outs.append(_dot(hid.astype(_BF16), w2_ref[c]))
    kc_ref[0] = outs[0].astype(_BF16)
    vc_ref[0] = outs[1].astype(_BF16)
    vct_ref[0] = outs[1].T.astype(_BF16)


def _compress_kernel(src_ref, w1_ref, w2_ref, pe_ref, w1t_ref, kc_ref, vc_ref, vct_ref, carry_ref, *, nrow):
    chunk_row = lambda c, s: src_ref[0, :, s * 2 * KV_DIM + c * KV_DIM:s * 2 * KV_DIM + (c + 1) * KV_DIM]
    _compress_rows(chunk_row, nrow, True, w1_ref, w2_ref, pe_ref, w1t_ref, carry_ref, (kc_ref, vc_ref, vct_ref))


def _compress_paged_kernel(pt_ref, *refs, n_src, steps_per_row):
    del pt_ref
    src_refs = refs[:n_src]
    w1_ref, w2_ref, pe_ref, w1t_ref, kc_ref, vc_ref, vct_ref, carry_ref, rows_a, rows_b = refs[n_src:]
    t = pl.program_id(0)
    nrow = n_src * PAGE_SIZE // CMP_STRIDE
    first_block = (jnp.maximum(t - 1, 0) % steps_per_row) == 0

    @pl.when(t == 0)
    def _():
        rows_b[...] = jnp.zeros_like(rows_b)
        carry_ref[...] = jnp.zeros_like(carry_ref)

    def step(fill_ref, read_ref):
        for p, r in enumerate(src_refs):
            for lc in range(2 * KV_DIM // LANES):
                fill_ref[lc, p * PAGE_SIZE:(p + 1) * PAGE_SIZE, :] = r[0, lc * LANES:(lc + 1) * LANES, :].T

        def chunk_row(c, s):
            lcs = range(c * KV_DIM // LANES, (c + 1) * KV_DIM // LANES)
            return jnp.concatenate([read_ref[lc, pl.ds(s, nrow, stride=CMP_STRIDE), :] for lc in lcs], axis=-1)

        _compress_rows(chunk_row, nrow, first_block, w1_ref, w2_ref, pe_ref, w1t_ref, carry_ref,
                       (kc_ref, vc_ref, vct_ref))

    @pl.when(t % 2 == 0)
    def _():
        step(rows_a, rows_b)

    @pl.when(t % 2 == 1)
    def _():
        step(rows_b, rows_a)


def _compress_call(kern, grid, in_specs, out_specs, scratch, n_prefetch, n_batch, n_out_rows, name, args):
    grid_spec = pltpu.PrefetchScalarGridSpec(num_scalar_prefetch=n_prefetch, grid=grid, in_specs=in_specs,
                                             out_specs=out_specs, scratch_shapes=scratch)
    return pl.pallas_call(
        kern,
        grid_spec=grid_spec,
        out_shape=[jax.ShapeDtypeStruct((n_batch, n_out_rows, KV_DIM), _BF16)] * 2
                  + [jax.ShapeDtypeStruct((n_batch, KV_DIM, n_out_rows), _BF16)],
        compiler_params=_params(*(["arbitrary"] * len(grid))),
        name=name,
    )(*args)


def _const_spec(a):
    return pl.BlockSpec(a.shape, lambda *_, _n=a.ndim: (0,) * _n, pipeline_mode=pl.Buffered(1))


def _compress(src, w1bd, w2bd, pe_b, w1t):
    bsz, nrow, cols = src.shape
    consts = (w1bd, w2bd, pe_b, w1t)
    return _compress_call(
        functools.partial(_compress_kernel, nrow=nrow), (bsz,),
        [pl.BlockSpec((1, nrow, cols), lambda b: (b, 0, 0))] + [_const_spec(a) for a in consts],
        [pl.BlockSpec((1, nrow, KV_DIM), lambda b: (b, 0, 0))] * 2 + [pl.BlockSpec((1, KV_DIM, nrow), lambda b: (b, 0, 0))],
        [pltpu.VMEM((2, 1, KV_DIM), _F32)], 0, bsz, nrow, "compress", (src,) + consts)


def _compress_paged(pages, page_table, w1bd, w2bd, pe_b, w1t):
    bsz, n_pages = page_table.shape
    spr = n_pages // PAGES_PER_STEP
    n_steps = bsz * spr
    nrow = PAGES_PER_STEP * PAGE_SIZE // CMP_STRIDE
    consts = (w1bd, w2bd, pe_b, w1t)

    def page_spec(p):
        def index(t, pt):
            tc = jnp.minimum(t, n_steps - 1)
            return (pt[tc // spr, (tc % spr) * PAGES_PER_STEP + p], 0, 0)
        return pl.BlockSpec((1, 2 * KV_DIM, PAGE_SIZE), index)

    prev = lambda t: jnp.maximum(t - 1, 0)
    row_out = pl.BlockSpec((1, nrow, KV_DIM), lambda t, pt: (prev(t) // spr, prev(t) % spr, 0))
    col_out = pl.BlockSpec((1, KV_DIM, nrow), lambda t, pt: (prev(t) // spr, 0, prev(t) % spr))
    rows_buf = pltpu.VMEM((2 * KV_DIM // LANES, PAGES_PER_STEP * PAGE_SIZE, LANES), _F32)
    return _compress_call(
        functools.partial(_compress_paged_kernel, n_src=PAGES_PER_STEP, steps_per_row=spr), (n_steps + 1,),
        [page_spec(p) for p in range(PAGES_PER_STEP)] + [_const_spec(a) for a in consts],
        [row_out, row_out, col_out], [pltpu.VMEM((2, 1, KV_DIM), _F32), rows_buf, rows_buf],
        1, bsz, n_pages * PAGE_SIZE // CMP_STRIDE, "compress_paged",
        (page_table,) + (pages,) * PAGES_PER_STEP + consts)


def _threshold_select(sc, n_sel, tri):
    bits = pltpu.bitcast(sc, jnp.int32)
    key = jnp.where(bits < 0, bits ^ jnp.int32(0x7FFFFFFF), bits)
    count_ge = lambda t: jnp.sum(jnp.where(key >= t, 1.0, 0.0), axis=0, keepdims=True)
    t0 = jnp.where(count_ge(jnp.int32(0)) >= n_sel, jnp.int32(0), jnp.int32(-2 ** 31))

    def body(b, t):
        cand = t | jnp.left_shift(jnp.int32(1), 30 - b)
        return jnp.where(count_ge(cand) >= n_sel, cand, t)

    t = lax.fori_loop(0, 31, body, t0)
    above = jnp.where(key > t, 1.0, 0.0)
    tie = jnp.where(key == t, 1.0, 0.0)
    need = n_sel - jnp.sum(above, axis=0, keepdims=True)
    tie_rank = _dot(tri, tie.astype(_BF16))
    return above + jnp.where(tie_rank <= need, tie, 0.0)


def _softmax_parts(parts):
    ms = [jnp.max(jnp.where(mk, s, NEG), axis=-1, keepdims=True) for s, mk in parts]
    m = functools.reduce(jnp.maximum, ms)
    ps = [jnp.where(mk, jnp.exp(s - m), 0.0) for s, mk in parts]
    l = functools.reduce(lambda a, b: a + b, [jnp.sum(p, axis=-1, keepdims=True) for p in ps])
    inv = 1.0 / jnp.maximum(l, 1e-30)
    return [p * inv for p in ps]


def _softmax_cols(parts, always_valid):
    masked = [jnp.where(mk, s, NEG) for s, mk in parts]
    m = functools.reduce(jnp.maximum, [jnp.max(s, axis=0, keepdims=True) for s in masked])
    if always_valid:
        ps = [jnp.exp(s - m) for s in masked]
    else:
        ps = [jnp.where(mk, jnp.exp(s - m), 0.0) for s, mk in parts]
    l = functools.reduce(lambda a, b: a + b, [jnp.sum(p, axis=0, keepdims=True) for p in ps])
    return ps, 1.0 / jnp.maximum(l, 1e-30)


def _attn_prompt_kernel(q_ref, ng_ref, ks_ref, vst_ref, kw_ref, vwt_ref, kc_ref, vct_ref,
                        tzt_ref, tct_ref, ot_ref, gsel_ref, rsum_ref, rep_ref, tri_ref,
                        o_ref, qzt_ref, otacc_ref, gt_ref, nsel_ref, *, n_cmp_pad, seq_len):
    i = pl.program_id(1)
    qb = Q_BLOCK
    rows = HPG * qb
    lane_g = lax.broadcasted_iota(jnp.int32, (qb, KV_DIM), 1) // HEAD_DIM
    vrows = lambda g: pl.ds(g * HEAD_DIM, HEAD_DIM)

    for g in range(N_KV):
        qz = jnp.concatenate([jnp.where(lane_g == g, q_ref[0, :, r * KV_DIM:(r + 1) * KV_DIM].astype(_F32), 0.0)
                              for r in range(HPG)], axis=0)
        qzt_ref[g] = qz.T.astype(_BF16)

    gparts = _split3(jax.nn.sigmoid(ng_ref[0]))
    gcols = jnp.concatenate([sum(_dot(p, gsel_ref[r]) for p in gparts) for r in range(HPG)], axis=0)
    gt_ref[...] = gcols.T
    otacc_ref[...] = jnp.zeros_like(otacc_ref)

    def emit(branch, g, out_t):
        otacc_ref[g] += gt_ref[pl.ds(branch * N_KV + g, 1), :] * out_t

    groups = range(N_KV)
    h = jnp.minimum(jnp.maximum(i - 3, 0) // 2, (seq_len - NEAR_KEYS) // LANES)
    ns = pl.multiple_of(h * LANES, LANES)
    v = i - 2 * h
    cn = lax.broadcasted_iota(jnp.int32, (NEAR_KEYS, rows), 0)
    lqn = lax.broadcasted_iota(jnp.int32, (NEAR_KEYS, rows), 1) % qb
    causal = cn - lqn <= v * SEL_BLOCK
    near = pl.ds(ns, NEAR_KEYS)

    fs = pl.multiple_of((jnp.maximum(i - WINDOW // SEL_BLOCK, 0) // 2) * LANES, LANES)
    wfar = pl.ds(fs, WIN_FAR_KEYS)
    cf = lax.broadcasted_iota(jnp.int32, (WIN_FAR_KEYS, rows), 0)
    lqf = lax.broadcasted_iota(jnp.int32, (WIN_FAR_KEYS, rows), 1) % qb
    far_ok = (i * qb + lqf - fs - cf < WINDOW) & (fs + cf < ns)
    kw_near, kw_far = kw_ref[0, near, :], kw_ref[0, wfar, :]
    s_wn = [_dot(kw_near, qzt_ref[g]) + tzt_ref[v, g] for g in groups]
    s_wf = [_dot(kw_far, qzt_ref[g]) for g in groups]
    p_w = [_softmax_cols([(s_wn[g], causal), (s_wf[g], far_ok)], True) for g in groups]
    for g in groups:
        (p_near, p_far), inv_w = p_w[g]
        emit(2, g, (_dot(vwt_ref[0, vrows(g), near], p_near.astype(_BF16))
                    + _dot(vwt_ref[0, vrows(g), wfar], p_far.astype(_BF16))) * inv_w)

    jc = lax.broadcasted_iota(jnp.int32, (n_cmp_pad, rows), 0)
    lq = lax.broadcasted_iota(jnp.int32, (n_cmp_pad, rows), 1) % qb
    cmp_valid = (jc >= 1) & (CMP_STRIDE * jc + (CMP_LEN - CMP_STRIDE - 1) - lq <= i * qb)
    mm = lax.broadcasted_iota(jnp.int32, (n_cmp_pad, LANES), 1)
    jj = lax.broadcasted_iota(jnp.int32, (n_cmp_pad, LANES), 0)
    shift_t = jnp.where((mm < 16) & (jj - mm == 4 * i - 8), 1.0, 0.0).astype(_BF16)
    kc_aug = jnp.concatenate([kc_ref[0], shift_t, shift_t, shift_t], axis=1)
    s_c =[_dot(kc_aug, jnp.concatenate([qzt_ref[g], tct_ref[0, g], tct_ref[1, g], tct_ref[2, g]], axis=0))
           for g in groups]
    sm_c = [_softmax_cols([(s, cmp_valid)], False) for s in s_c]
    pn_c = [ps[0] for ps, _ in sm_c]
    inv_c = [inv for _, inv in sm_c]
    for g in groups:
        emit(0, g, _dot(vct_ref[0, vrows(g), :], pn_c[g].astype(_BF16)) * inv_c[g])
    y_c = [sum(_dot(ot_ref[...], p) for p in _split3(pn_c[g])) * inv_c[g] for g in groups]
    imp_t = sum(sum(_dot(p, rsum_ref[g]) for p in _split3(y_c[g])) for g in groups)

    jrow = lax.broadcasted_iota(jnp.int32, imp_t.shape, 0)
    forced = (jrow == 0) | (jrow == i) | (jrow == i - 1)
    score = jnp.where(forced, jnp.inf, jnp.where(jrow <= i, imp_t, -jnp.inf))
    sel_t = _threshold_select(score, N_SELECT, tri_ref[...])

    sel16 = sel_t.astype(_BF16)
    for g in groups:
        neg = (_dot(sel16, rep_ref[g]) - 1.0) * MASK_BIG
        nsel_ref[0, g] = neg
        nsel_ref[1, g] = jnp.where(jrow < 2 * h, neg, -MASK_BIG)

    def add_block_mask(s, far, g, j0, n_blocks):
        return jnp.concatenate([s[jj * SEL_BLOCK:(jj + 1) * SEL_BLOCK, :] + nsel_ref[far, g, pl.ds(j0 + jj, 1), :]
                                for jj in range(n_blocks)], axis=0)

    n_far = (ns + FAR_TILE - 1) // FAR_TILE

    k_near = ks_ref[0, near, :]
    s_n = [add_block_mask(_dot(k_near, qzt_ref[g]) + tzt_ref[v, g], 0, g, 2 * h, NEAR_KEYS // SEL_BLOCK)
           for g in groups]
    s_n = [jnp.where(causal, s, NEG) for s in s_n]
    m0 = [jnp.max(s, axis=0, keepdims=True) for s in s_n]
    p_n = [jnp.exp(s_n[g] - m0[g]) for g in groups]
    l0 = [jnp.sum(p, axis=0, keepdims=True) for p in p_n]
    acc0 = [_dot(vst_ref[0, vrows(g), near], p_n[g].astype(_BF16)) for g in groups]

    def far_body(tau, carry):
        m_old, l_old, acc_old = carry
        k0 = pl.multiple_of(tau * FAR_TILE, FAR_TILE)
        tile = pl.ds(k0, FAR_TILE)
        kt = ks_ref[0, tile, :]
        s_f = [add_block_mask(_dot(kt, qzt_ref[g]), 1, g, tau * (FAR_TILE // SEL_BLOCK), FAR_TILE // SEL_BLOCK)
               for g in groups]
        m_new = [jnp.maximum(m_old[g], jnp.max(s_f[g], axis=0, keepdims=True)) for g in groups]
        alpha = [jnp.exp(m_old[g] - m_new[g]) for g in groups]
        p_f = [jnp.exp(s_f[g] - m_new[g]) for g in groups]
        l_new = [alpha[g] * l_old[g] + jnp.sum(p_f[g], axis=0, keepdims=True) for g in groups]
        acc = [alpha[g] * acc_old[g] + _dot(vst_ref[0, vrows(g), tile], p_f[g].astype(_BF16)) for g in groups]
        return tuple(m_new), tuple(l_new), tuple(acc)

    _, l1, acc1 = lax.fori_loop(0, n_far, far_body, (tuple(m0), tuple(l0), tuple(acc0)))
    for g in groups:
        emit(1, g, acc1[g] * (1.0 / l1[g]))

    res = otacc_ref[...].reshape(N_KV * HEAD_DIM, rows).T
    for r in range(HPG):
        o_ref[0, :, r * KV_DIM:(r + 1) * KV_DIM] = res[r * qb:(r + 1) * qb, :].astype(_BF16)


def _attn_prompt(q2, ng, ks, vst, kw, vwt, kc, vct, tabs):
    bsz, t, _ = q2.shape
    nblk = t // SEL_BLOCK
    n_cmp_pad = kc.shape[1]
    rows = HPG * Q_BLOCK
    assert t % FAR_TILE == 0 and t >= NEAR_KEYS
    names = ["tzt", "tct", "ot", "gsel", "rsum", "rep", "tri"]
    full = lambda a: pl.BlockSpec(a.shape, lambda b, i, _n=a.ndim: (0,) * _n, pipeline_mode=pl.Buffered(1))
    per_b = lambda a: pl.BlockSpec((1,) + a.shape[1:], lambda b, i: (b, 0, 0))
    kern = functools.partial(_attn_prompt_kernel, n_cmp_pad=n_cmp_pad, seq_len=t)
    return pl.pallas_call(
        kern,
        grid=(bsz, nblk),
        in_specs=[pl.BlockSpec((1, Q_BLOCK, ATTN_DIM), lambda b, i: (b, i, 0)),
                  pl.BlockSpec((1, Q_BLOCK, LANES), lambda b, i: (b, i, 0)),
                  per_b(ks), per_b(vst), per_b(kw), per_b(vwt), per_b(kc), per_b(vct)]
                 + [full(tabs[n]) for n in names],
        out_specs=pl.BlockSpec((1, Q_BLOCK, ATTN_DIM), lambda b, i: (b, i, 0)),
        out_shape=jax.ShapeDtypeStruct((bsz, t, ATTN_DIM), _BF16),
        scratch_shapes=[pltpu.VMEM((N_KV, KV_DIM, rows), _BF16),
                        pltpu.VMEM((N_KV, HEAD_DIM, rows), _F32),
                        pltpu.VMEM((LANES, rows), _F32),
                        pltpu.VMEM((2, N_KV, SEL_BLOCK, rows), _F32)],
        compiler_params=_params("parallel", "arbitrary"),
        name="attn_prompt",
    )(q2, ng, ks, vst, kw, vwt, kc, vct, *[tabs[n] for n in names])


def _bias_table(rel, dist, head):
    nmax = max(int(dist.max()), 1) + 1
    bk = _rel_bucket_np(np.arange(nmax))
    rel_h = rel[:, head]
    out = jnp.broadcast_to(rel_h[0], np.broadcast_shapes(dist.shape, head.shape))
    dist = lax.optimization_barrier(jnp.asarray(dist, jnp.int32))
    for b in range(1, int(bk.max()) + 1):
        first = int(np.argmax(bk >= b))
        out = jnp.where(dist >= first, rel_h[b], out)
    return out


def _prompt_tables(rel_bias, t):
    nblk = t // SEL_BLOCK
    n_cmp_pad = t // CMP_STRIDE
    assert nblk <= SEL_BLOCK and n_cmp_pad % LANES == 0
    rows = HPG * Q_BLOCK
    rel = rel_bias.astype(_F32)
    r_idx = np.arange(rows) // Q_BLOCK
    q_idx = np.arange(rows) % Q_BLOCK
    head = np.arange(N_KV)[:, None] * HPG + r_idx[None, :]
    c31 = rel[REL_BUCKETS - 1][head]
    c = np.arange(NEAR_KEYS)
    dist = np.arange(NEAR_VARIANTS)[:, None, None] * SEL_BLOCK + q_idx[None, None, :] - c[None, :, None]
    tzt = _bias_table(rel, dist[:, None], head[None, :, None, :]) - c31[None, :, None, :]
    mmv = np.arange(16)
    dist_c = q_idx[None, :] - CMP_STRIDE * (mmv[:, None] - 8) - (CMP_LEN - CMP_STRIDE - 1)
    delta = _bias_table(rel, dist_c[None], head[:, None, :]) - c31[:, None, :]
    tct = jnp.stack(_split3(jnp.pad(delta, ((0, 0), (0, LANES - 16), (0, 0)))))
    n = np.arange(n_cmp_pad) - 1
    cs = n * CMP_STRIDE
    bs = np.arange(SEL_BLOCK) * SEL_BLOCK
    ov = np.clip(np.minimum(cs[None, :] + CMP_LEN, bs[:, None] + SEL_BLOCK) - np.maximum(cs[None, :], bs[:, None]),
                 0, CMP_LEN).astype(np.float32) / CMP_LEN
    ov[:, 0] = 0.0
    ov[nblk:, :] = 0.0
    gsel = np.zeros((HPG, LANES, LANES), np.float32)
    for j in range(3):
        for r in range(HPG):
            for g in range(N_KV):
                gsel[r, j * N_HEADS + r * N_KV + g, j * N_KV + g] = 1.0
    rsum = np.zeros((N_KV, rows, N_KV * Q_BLOCK), np.float32)
    for g in range(N_KV):
        rsum[g, np.arange(rows), g * Q_BLOCK + q_idx] = 1.0
    bf = lambda a: jnp.asarray(a, _BF16)
    return dict(tzt=tzt, tct=tct, ot=bf(ov), gsel=bf(gsel), rsum=bf(rsum),
                tri=bf(np.tril(np.ones((SEL_BLOCK, SEL_BLOCK), np.float32))),
                rep=bf(rsum.transpose(0, 2, 1)))


def _gate_expand():
    ex = np.zeros((LANES, 3 * ATTN_DIM), np.float32)
    for j in range(3):
        for r in range(HPG):
            for g in range(N_KV):
                col = j * ATTN_DIM + r * KV_DIM + g * HEAD_DIM
                ex[j * N_HEADS + r * N_KV + g, col:col + HEAD_DIM] = 1.0
    return jnp.asarray(ex, _BF16)


PAGES_PER_STEP = 32
SUB_PAGES = 32


def _attn_sample_kernel(pt_ref, q_ref, ng_ref, kc_ref, vc_ref, knew_ref, wnew_ref, wst_ref, *refs,
                        past_len, n_blk):
    del pt_ref
    page_refs = refs[:PAGES_PER_STEP]
    (bc_ref, blast_ref, c31_ref, bnew_ref, bwin_ref, ot_ref, e64_ref, ex_ref, tri_ref,
     o_ref, qall_ref, gate_ref, oacc_ref, acc_ref, m_ref, l_ref, selt_ref) = refs[PAGES_PER_STEP:]
    k = pl.program_id(1)
    nk = pl.num_programs(1)
    tq = q_ref.shape[1]
    rows = N_KV * HPG * tq
    lane_g = lax.broadcasted_iota(jnp.int32, (tq, KV_DIM), 1) // HEAD_DIM
    rq = lax.broadcasted_iota(jnp.int32, (rows, 1), 0) % tq
    sub_keys = SUB_PAGES * PAGE_SIZE

    def emit(branch, out):
        for g in range(N_KV):
            for r in range(HPG):
                col = r * KV_DIM
                row0 = (g * HPG + r) * tq
                gt = gate_ref[:, branch * ATTN_DIM + col:branch * ATTN_DIM + col + KV_DIM]
                oacc_ref[:, col:col + KV_DIM] += jnp.where(lane_g == g, gt * out[row0:row0 + tq, :], 0.0)

    def pad_rows(x, n):
        if n == x.shape[0]:
            return x
        return jnp.concatenate([x, jnp.zeros((n - x.shape[0], x.shape[1]), x.dtype)], axis=0)

    @pl.when(k == 0)
    def _():
        qf = q_ref[0].astype(_F32)
        pieces = []
        for g in range(N_KV):
            for r in range(HPG):
                pieces.append(jnp.where(lane_g == g, qf[:, r * KV_DIM:(r + 1) * KV_DIM], 0.0))
        qall = jnp.concatenate(pieces, axis=0).astype(_BF16)
        qall_ref[...] = qall
        gs = jax.nn.sigmoid(ng_ref[0])
        gate_ref[...] = sum(_dot(p, ex_ref[...]) for p in _split3(gs))
        oacc_ref[...] = jnp.zeros_like(oacc_ref)

        n_cmp_pad = kc_ref.shape[1]
        jc = lax.broadcasted_iota(jnp.int32, (rows, n_cmp_pad), 1)
        cmp_valid = (jc >= 1) & (CMP_STRIDE * jc + (CMP_LEN - CMP_STRIDE - 1) - rq <= past_len)
        s = _dot_nt(qall, kc_ref[0]) + bc_ref[...]
        (pn,) = _softmax_parts([(s, cmp_valid)])
        emit(0, _dot(pn.astype(_BF16), vc_ref[0]))
        imp_rows = []
        for g in range(N_KV):
            sg = sum(pn[(g * HPG + r) * tq:(g * HPG + r + 1) * tq, :] for r in range(HPG))
            imp_rows += [sg] * HPG
        imp = jnp.concatenate(imp_rows, axis=0)
        imp_t = sum(_dot_nt(ot_ref[...], p) for p in _split3(imp))

        jrow = lax.broadcasted_iota(jnp.int32, imp_t.shape, 0)
        tpos = past_len + lax.broadcasted_iota(jnp.int32, imp_t.shape, 1) % tq
        cur = tpos // SEL_BLOCK
        forced = (jrow == 0) | (jrow == cur) | (jrow == cur - 1)
        valid = jrow * SEL_BLOCK <= tpos
        score = jnp.where(forced, jnp.inf, jnp.where(valid, imp_t, -jnp.inf))
        selt_ref[...] = pad_rows(_threshold_select(score, min(N_SELECT, n_blk), tri_ref[...]), selt_ref.shape[0])

        kn = pad_rows(knew_ref[0, :, :KV_DIM], LANES).astype(_BF16)
        vn = pad_rows(knew_ref[0, :, KV_DIM:], LANES).astype(_BF16)
        cn = lax.broadcasted_iota(jnp.int32, (rows, LANES), 1)
        mk = (cn <= rq) & (cn < tq)
        s = _dot_nt(qall, kn) + bnew_ref[...]
        m = jnp.max(jnp.where(mk, s, NEG), axis=-1, keepdims=True)
        p = jnp.where(mk, jnp.exp(s - m), 0.0)
        m_ref[...] = jnp.broadcast_to(m, m_ref.shape)
        l_ref[...] = jnp.broadcast_to(jnp.sum(p, axis=-1, keepdims=True), l_ref.shape)
        acc_ref[...] = _dot(p.astype(_BF16), vn)

    qall = qall_ref[...]
    blk_per_step = PAGES_PER_STEP * PAGE_SIZE // SEL_BLOCK
    j0 = pl.multiple_of(k * blk_per_step, blk_per_step)
    sel_step = selt_ref[pl.ds(j0, LANES), :].T.astype(_BF16)
    c31 = jnp.concatenate([c31_ref[...]] * (sub_keys // LANES), axis=-1)
    n_sub = PAGES_PER_STEP // SUB_PAGES
    for st in range(n_sub):
        pages = page_refs[st * SUB_PAGES:(st + 1) * SUB_PAGES]
        kt = jnp.concatenate([r[0, :KV_DIM, :] for r in pages], axis=1).astype(_BF16)
        vt = jnp.concatenate([r[0, KV_DIM:, :] for r in pages], axis=1).astype(_BF16)
        mk = _dot(sel_step, e64_ref[:, st * sub_keys:(st + 1) * sub_keys]) > 0.5
        if st == n_sub - 1:
            bias = jnp.where(k == nk - 1, blast_ref[...], c31)
        else:
            bias = c31
        s = jnp.where(mk, _dot(qall, kt) + bias, NEG)
        m_old = m_ref[:, :1]
        m_new = jnp.maximum(m_old, jnp.max(s, axis=-1, keepdims=True))
        alpha = jnp.exp(m_old - m_new)
        p = jnp.exp(s - m_new)
        l_ref[...] = jnp.broadcast_to(alpha * l_ref[:, :1] + jnp.sum(p, axis=-1, keepdims=True), l_ref.shape)
        m_ref[...] = jnp.broadcast_to(m_new, m_ref.shape)
        acc_ref[...] = alpha * acc_ref[...] + _dot_nt(p.astype(_BF16), vt)

    @pl.when(k == nk - 1)
    def _():
        emit(1, acc_ref[...] * (1.0 / l_ref[:, :1]))
        lw = wst_ref.shape[2]
        kw = wst_ref[0, :KV_DIM, :].astype(_BF16)
        vw = wst_ref[0, KV_DIM:, :].astype(_BF16)
        kn = pad_rows(wnew_ref[0, :, :KV_DIM], LANES).astype(_BF16)
        vn = pad_rows(wnew_ref[0, :, KV_DIM:], LANES).astype(_BF16)
        cw = lax.broadcasted_iota(jnp.int32, (rows, lw), 1)
        dw = lw + rq - cw
        cn = lax.broadcasted_iota(jnp.int32, (rows, LANES), 1)
        pw, pnw = _softmax_parts([(_dot(qall, kw) + bwin_ref[...], (dw >= 0) & (dw < WINDOW)),
                                  (_dot_nt(qall, kn) + bnew_ref[...], (cn <= rq) & (cn < tq))])
        emit(2, _dot_nt(pw.astype(_BF16), vw) + _dot(pnw.astype(_BF16), vn))
        o_ref[0] = oacc_ref[...].astype(_BF16)


def _page_specs():
    return [pl.BlockSpec((1, 2 * KV_DIM, PAGE_SIZE),
                         lambda b, k, pt, _p=p: (pt[b, k * PAGES_PER_STEP + _p], 0, 0))
            for p in range(PAGES_PER_STEP)]


def _transposed_rows(a):
    n, rows = a.shape[:2]
    return jnp.transpose(a, (0, 2, 3, 4, 1)).reshape(n, 2 * KV_DIM, rows)


def _attn_sample(q2, ng, kc, vc, kvs_new, kvw_new, win_state, slc_pages, page_table, tabs, past_len):
    bsz, tq, _ = q2.shape
    n_pages = page_table.shape[1]
    assert n_pages % PAGES_PER_STEP == 0 and past_len == n_pages * PAGE_SIZE and past_len % SEL_BLOCK == 0
    n_steps = n_pages // PAGES_PER_STEP
    rows = N_KV * HPG * tq
    assert rows == LANES
    n_blk = -(-(past_len + tq) // SEL_BLOCK)
    nb_rows = tabs["ot"].shape[0]
    blk_per_step = PAGES_PER_STEP * PAGE_SIZE // SEL_BLOCK
    selt_rows = max((n_steps - 1) * blk_per_step + LANES, nb_rows)
    full = lambda a: pl.BlockSpec(a.shape, lambda b, k, pt, _n=a.ndim: (0,) * _n, pipeline_mode=pl.Buffered(1))
    per_b = lambda a: pl.BlockSpec((1,) + a.shape[1:], lambda b, k, pt: (b, 0, 0))
    page_specs = _page_specs()
    names = ["bc", "blast", "c31", "bnew", "bwin", "ot", "e64", "ex", "tri"]
    kern = functools.partial(_attn_sample_kernel, past_len=past_len, n_blk=n_blk)
    grid_spec = pltpu.PrefetchScalarGridSpec(
        num_scalar_prefetch=1,
        grid=(bsz, n_steps),
        in_specs=[per_b(q2), per_b(ng), per_b(kc), per_b(vc), per_b(kvs_new), per_b(kvw_new), per_b(win_state)]
                 + page_specs + [full(tabs[n]) for n in names],
        out_specs=pl.BlockSpec((1, tq, ATTN_DIM), lambda b, k, pt: (b, 0, 0)),
        scratch_shapes=[pltpu.VMEM((rows, KV_DIM), _BF16),
                        pltpu.VMEM((tq, 3 * ATTN_DIM), _F32),
                        pltpu.VMEM((tq, ATTN_DIM), _F32),
                        pltpu.VMEM((rows, KV_DIM), _F32),
                        pltpu.VMEM((rows, LANES), _F32),
                        pltpu.VMEM((rows, LANES), _F32),
                        pltpu.VMEM((selt_rows, rows), _F32)],
    )
    return pl.pallas_call(
        kern,
        grid_spec=grid_spec,
        out_shape=jax.ShapeDtypeStruct((bsz, tq, ATTN_DIM), _BF16),
        compiler_params=_params("parallel", "arbitrary"),
        name="attn_sample",
    )(page_table, q2, ng, kc, vc, kvs_new, kvw_new, win_state, *([slc_pages] * PAGES_PER_STEP),
      *[tabs[n] for n in names])


def _sample_tables(rel_bias, past_len, tq, lw, n_cmp_pad):
    rows = N_KV * HPG * tq
    rel = rel_bias.astype(_F32)
    ridx = np.arange(rows)
    head = ridx // tq
    qi = ridx % tq
    tpos = past_len + qi

    bias_of = lambda dist: _bias_table(rel, dist, head[:, None])

    jc = np.arange(n_cmp_pad)
    bc = bias_of(tpos[:, None] - (CMP_STRIDE * jc[None, :] + CMP_LEN - CMP_STRIDE - 1))
    sub_keys = SUB_PAGES * PAGE_SIZE
    blast = bias_of(tpos[:, None] - (past_len - sub_keys + np.arange(sub_keys))[None, :])
    bnew = bias_of(qi[:, None] - np.arange(LANES)[None, :])
    bwin = bias_of(lw + qi[:, None] - np.arange(lw)[None, :])
    c31 = jnp.broadcast_to(rel[REL_BUCKETS - 1][head][:, None], (rows, LANES))
    n_blk = -(-(past_len + tq) // SEL_BLOCK)
    nb_rows = -(-n_blk // LANES) * LANES
    n = jc - 1
    cs = n * CMP_STRIDE
    bs = np.arange(nb_rows) * SEL_BLOCK
    ov = np.clip(np.minimum(cs[None, :] + CMP_LEN, bs[:, None] + SEL_BLOCK) - np.maximum(cs[None, :], bs[:, None]),
                 0, CMP_LEN).astype(np.float32) / CMP_LEN
    ov[:, 0] = 0.0
    ov[n_blk:, :] = 0.0
    step_keys = PAGES_PER_STEP * PAGE_SIZE
    e64 = (np.arange(step_keys)[None, :] // SEL_BLOCK == np.arange(LANES)[:, None]).astype(np.float32)
    tri = np.tril(np.ones((nb_rows, nb_rows), np.float32))
    return dict(bc=bc, blast=blast, c31=c31, bnew=bnew, bwin=bwin, ot=jnp.asarray(ov, _BF16),
                e64=jnp.asarray(e64, _BF16), ex=_gate_expand(), tri=jnp.asarray(tri, _BF16))


def _layer_weights(w_in, phi_pe, phi_w1, phi_w2, w_attn_out, conv_w, w_conv_out, w_o, w_up, w_down):
    d = w_in.shape[0]
    o_q, o_kc, o_ks, o_kw = 0, ATTN_DIM, ATTN_DIM + 2 * KV_DIM, ATTN_DIM + 4 * KV_DIM
    o_ng = ATTN_DIM + 6 * KV_DIM
    o_glu = o_ng + 3 * N_HEADS
    o_mg = o_glu + 2 * CONV_DIM
    bf = lambda a: a.astype(_BF16)
    wq = w_in[:, o_q:o_kc].reshape(d, N_KV, HPG, HEAD_DIM).transpose(0, 2, 1, 3).reshape(d, ATTN_DIM)
    wng = w_in[:, o_ng:o_glu].reshape(d, N_KV, HPG, 3).transpose(0, 3, 2, 1).reshape(d, 3 * N_HEADS)
    wng = jnp.pad(wng, ((0, 0), (0, LANES - 3 * N_HEADS)))
    w5 = phi_w1.reshape(2, CMP_R, CMP_STRIDE, HEAD_DIM, PHI_HIDDEN)
    eye = jnp.eye(N_KV, dtype=_F32)
    w1bd = jnp.einsum("crsde,gh->csgdrhe", w5, eye).reshape(2, CMP_STRIDE, KV_DIM, CMP_R * KV_DIM)
    w2bd = jnp.einsum("che,gk->cghke", phi_w2, eye).reshape(2, KV_DIM, KV_DIM)
    rep = LANES // PHI_HIDDEN
    pe_b = jnp.broadcast_to(phi_pe.reshape(2, CMP_LEN * HEAD_DIM, 1), (2, CMP_LEN * HEAD_DIM, LANES))
    w1t = jnp.tile(phi_w1, (1, 1, rep))
    wao = w_attn_out.reshape(N_KV, HPG, HEAD_DIM, d).transpose(1, 0, 2, 3).reshape(ATTN_DIM, d)
    half = PROJ_TILE // 2
    wga = w_in[:, o_glu:o_glu + CONV_DIM].reshape(d, CONV_DIM // half, half)
    wgb = w_in[:, o_glu + CONV_DIM:o_mg].reshape(d, CONV_DIM // half, half)
    wglu = jnp.concatenate([wga, wgb], axis=2).reshape(d, 2 * CONV_DIM)
    w_all = jnp.concatenate([wq, w_in[:, o_kc:o_ng], wglu, w_in[:, o_mg:]], axis=1)
    return dict(
        w_all=bf(w_all), wng=bf(wng), w1bd=bf(w1bd), w2bd=bf(w2bd), pe_b=pe_b, w1t=w1t,
        wao=bf(wao), wco=bf(w_conv_out), wo=bf(w_o), wup=bf(w_up), wdown=bf(w_down),
        conv_w=jnp.pad(conv_w, ((0, HALO - CONV_K), (0, 0))))


def _finish(x2, attn2, conv, mg, w, g_post_mix, g_pre_ffn, g_post_ffn, tm):
    mixed = _mix(attn2, conv, mg, w["wao"], w["wco"], min(MIX_ROW_TILE, x2.shape[0]), MIX_TILE_N)
    x1 = _oproj(mixed, x2, w["wo"], g_post_mix, tm)
    return _ffn(x1, g_pre_ffn, g_post_ffn, w["wup"], w["wdown"], tm, FFN_TILE_F)


def kernel(x_prompt, x_sample, cache_kv_cmp, cache_kv_slc, state_kv_win, state_conv, page_table, w_in, phi_pe,
           phi_w1, phi_w2, rel_bias, w_attn_out, conv_w, conv_b, conv_ln_g, conv_ln_b, w_conv_out, w_o, w_up,
           w_down, g_pre_mix, g_post_mix, g_pre_ffn, g_post_ffn):
    depth = w_in.shape[0]
    bp, tp, d = x_prompt.shape
    bs, ts, _ = x_sample.shape
    n_pages = page_table.shape[1]
    past_len = n_pages * PAGE_SIZE
    lw = state_kv_win.shape[2]
    chunk_cols = CMP_STRIDE * 2 * KV_DIM
    assert ts < CMP_STRIDE and tp % CMP_STRIDE == 0 and lw == WINDOW and tp >= WINDOW

    tabs_p = _prompt_tables(rel_bias, tp)
    tabs_s = _sample_tables(rel_bias, past_len, ts, lw, past_len // CMP_STRIDE)
    yp, ys = x_prompt.reshape(bp * tp, d), x_sample.reshape(bs * ts, d)
    outs = [[] for _ in range(8)]
    row = lambda a: a.reshape(1, -1)
    kv5 = lambda a, b, t: a.reshape(b, t, 2, N_KV, HEAD_DIM)
    kv5_t = lambda a: jnp.transpose(a.reshape(a.shape[0], 2, N_KV, HEAD_DIM, a.shape[2]), (0, 4, 1, 2, 3))
    for l in range(depth):
        w = _layer_weights(w_in[l], phi_pe[l], phi_w1[l], phi_w2[l], w_attn_out[l], conv_w[l], w_conv_out[l],
                           w_o[l], w_up[l], w_down[l])
        gpm, gqm, gpf, gqf = row(g_pre_mix[l]), row(g_post_mix[l]), row(g_pre_ffn[l]), row(g_post_ffn[l])
        cargs = (w["conv_w"], row(conv_b[l]), row(conv_ln_g[l]), row(conv_ln_b[l]))
        cmp_w = (w["w1bd"], w["w2bd"], w["pe_b"], w["w1t"])

        tm = ROW_TILE
        q2, (kvc_t, kvc16), (kvs_t, ks16, vst16), (kvw_t, kw16, vwt16), ng, u, mg = _proj(
            yp, gpm, w["w_all"], w["wng"], PROJ_ROW_TILE, tp)
        n_chunk = tp // CMP_STRIDE
        kc, _, vct = _compress(kvc16.reshape(bp, n_chunk, chunk_cols), *cmp_w)
        attn2 = _attn_prompt(q2.reshape(bp, tp, ATTN_DIM), ng.reshape(bp, tp, LANES),
                             ks16.reshape(bp, tp, KV_DIM), vst16, kw16.reshape(bp, tp, KV_DIM), vwt16,
                             kc, vct, tabs_p)
        u3 = u.reshape(bp, tp, CONV_DIM)
        conv = _conv(u3, u3, *cargs, CONV_TILE_T, True)
        yp = _finish(yp, attn2.reshape(bp * tp, ATTN_DIM), conv.reshape(bp * tp, CONV_DIM), mg, w, gqm, gpf, gqf, tm)
        outs[0].append(kv5_t(kvc_t))
        outs[2].append(kv5_t(kvs_t))
        outs[4].append(kv5_t(kvw_t[:, :, tp - WINDOW:]))
        outs[6].append(u3[:, tp - (CONV_K - 1):])

        tm = bs * ts
        q2, (kvc,), (kvs,), (kvw,), ng, u, mg = _proj(ys, gpm, w["w_all"], w["wng"], tm)
        kc, vc, _ = _compress_paged(_transposed_rows(cache_kv_cmp[l]), page_table, *cmp_w)
        attn2 = _attn_sample(q2.reshape(bs, ts, ATTN_DIM), ng.reshape(bs, ts, LANES), kc, vc,
                             kvs.reshape(bs, ts, 2 * KV_DIM), kvw.reshape(bs, ts, 2 * KV_DIM),
                             _transposed_rows(state_kv_win[l]), _transposed_rows(cache_kv_slc[l]),
                             page_table, tabs_s, past_len)
        u3 = u.reshape(bs, ts, CONV_DIM)
        hist = jnp.pad(state_conv[l], ((0, 0), (HALO - (CONV_K - 1), 0), (0, 0)))
        conv = _conv(u3, hist, *cargs, ts, False)
        ys = _finish(ys, attn2.reshape(bs * ts, ATTN_DIM), conv.reshape(bs * ts, CONV_DIM), mg, w, gqm, gpf, gqf, tm)
        outs[1].append(kv5(kvc, bs, ts))
        outs[3].append(kv5(kvs, bs, ts))
        win_rows = jnp.concatenate([state_kv_win[l], kv5(kvw, bs, ts)], axis=1)
        outs[5].append(win_rows[:, win_rows.shape[1] - min(WINDOW, win_rows.shape[1]):])
        up = jnp.concatenate([state_conv[l], u3], axis=1)
        outs[7].append(up[:, up.shape[1] - (CONV_K - 1):])

    stack = lambda i: jnp.stack(outs[i])
    return (yp.reshape(bp, tp, d), ys.reshape(bs, ts, d), stack(0), stack(1), stack(2), stack(3),
            stack(4), stack(5), stack(6), stack(7))
```
